```python
import math
import jax
import jax.numpy as jnp
from jax import lax
import numpy as np

D_MODEL = 1024
BATCH = 2
SEQ = 8192
DEPTH = 2
DEC_BATCH = 128
DEC_SEQ = 8
PAST_LEN = 16384
PAGE_SIZE = 128

N_AB_LAYERS = (DEPTH + 1) // 2
N_MLA_LAYERS = DEPTH // 2

SWA_HEADS = 8
SWA_KV_HEADS = 2
SWA_HEAD_DIM = 64
WINDOW = 128
SWA_SCALE = SWA_HEAD_DIM ** -0.5

HGRN_HEADS = 4
HGRN_HEAD_DIM = 128
HGRN_CHUNK = 32

SWA_Q_W = SWA_HEADS * SWA_HEAD_DIM
SWA_KV_W = SWA_KV_HEADS * SWA_HEAD_DIM
HGRN_W = HGRN_HEADS * HGRN_HEAD_DIM
AB_SPLIT_SIZES = (SWA_Q_W, SWA_KV_W, SWA_KV_W, HGRN_W, HGRN_W, HGRN_W, HGRN_W)
AB_IN_W = sum(AB_SPLIT_SIZES)
AB_MIX_W = SWA_Q_W + HGRN_W

MLA_HEADS = 16
MLA_NOPE_DIM = 64
MLA_ROPE_DIM = 32
MLA_V_DIM = 64
MLA_Q_RANK = 512
MLA_KV_RANK = 256
MLA_SCALE = (MLA_NOPE_DIM + MLA_ROPE_DIM) ** -0.5
ROPE_THETA = 10000.0
Q_BLOCK = 128

D_FF = -(-8 * D_MODEL // (3 * 256)) * 256
PLE_DIM = 256
NORM_EPS = 1e-6
NEG_INIT = -1e30

kernel_name = 'hybrid_swa_hgrn2_mla_decoder_step'


def rms_norm(x, g):
    xf = x.astype(jnp.float32)
    y = xf * lax.rsqrt(jnp.mean(xf * xf, axis=-1, keepdims=True) + NORM_EPS)
    return (y * g.astype(jnp.float32)).astype(x.dtype)


def alibi_slopes(n):
    return 2.0 ** (-8.0 * jnp.arange(1, n + 1, dtype=jnp.float32) / n)


def sink_softmax(s, sink):
    m = jnp.maximum(jnp.max(s, axis=-1, keepdims=True), sink)
    p = jnp.exp(s - m)
    return p / (jnp.sum(p, axis=-1, keepdims=True) + jnp.exp(sink - m))


def rope(x, pos):
    half = MLA_ROPE_DIM // 2
    inv_freq = ROPE_THETA ** (-jnp.arange(0, half, dtype=jnp.float32) * 2.0 / MLA_ROPE_DIM)
    ang = pos.astype(jnp.float32)[:, None] * inv_freq[None, :]
    ang = ang.reshape((ang.shape[0],) + (1,) * (x.ndim - 3) + (half,))
    cos, sin = jnp.cos(ang), jnp.sin(ang)
    xf = x.astype(jnp.float32)
    x1, x2 = xf[..., :half], xf[..., half:]
    return jnp.concatenate([x1 * cos - x2 * sin, x2 * cos + x1 * sin], axis=-1).astype(x.dtype)


def swiglu(u, w_gate, w_up, w_down):
    return (jax.nn.silu(u @ w_gate) * (u @ w_up)) @ w_down


def swa_prompt(q, k, v, sinks):
    B, S = q.shape[0], q.shape[1]
    nb = S // WINDOW
    G = SWA_HEADS // SWA_KV_HEADS
    qb = q.reshape(B, nb, WINDOW, SWA_KV_HEADS, G, SWA_HEAD_DIM)
    kb = k.reshape(B, nb, WINDOW, SWA_KV_HEADS, SWA_HEAD_DIM)
    vb = v.reshape(B, nb, WINDOW, SWA_KV_HEADS, SWA_HEAD_DIM)
    shift = lambda a: jnp.concatenate([jnp.zeros_like(a[:, :1]), a[:, :-1]], axis=1)
    kk = jnp.concatenate([shift(kb), kb], axis=2)
    vv = jnp.concatenate([shift(vb), vb], axis=2)
    qpos = WINDOW + jnp.arange(WINDOW)
    kpos = jnp.arange(2 * WINDOW)
    dist = qpos[:, None] - kpos[None, :]
    valid = (dist >= 0) & (dist <= WINDOW)
    valid = valid[None] & ((jnp.arange(nb) > 0)[:, None, None] | (kpos >= WINDOW)[None, None, :])
    slopes = alibi_slopes(SWA_HEADS).reshape(SWA_KV_HEADS, G)[:, :, None, None]
    s = jnp.einsum('bnqhgd,bnkhd->bnhgqk', qb, kk).astype(jnp.float32) * SWA_SCALE
    s = s - slopes * dist.astype(jnp.float32)
    s = jnp.where(valid[None, :, None, None], s, -jnp.inf)
    p = sink_softmax(s, sinks.astype(jnp.float32).reshape(SWA_KV_HEADS, G)[:, :, None, None])
    o = jnp.einsum('bnhgqk,bnkhd->bnqhgd', p, vv.astype(jnp.float32))
    return o.reshape(B, S, SWA_Q_W).astype(q.dtype)


def swa_sample(q, k, v, k_buf, v_buf, sinks):
    B, T = q.shape[0], q.shape[1]
    G = SWA_HEADS // SWA_KV_HEADS
    kk = jnp.concatenate([k_buf.astype(k.dtype), k], axis=1)
    vv = jnp.concatenate([v_buf.astype(v.dtype), v], axis=1)
    qpos = PAST_LEN + jnp.arange(T)
    kpos = PAST_LEN - WINDOW + jnp.arange(WINDOW + T)
    dist = qpos[:, None] - kpos[None, :]
    valid = (dist >= 0) & (dist <= WINDOW)
    slopes = alibi_slopes(SWA_HEADS).reshape(SWA_KV_HEADS, G)[:, :, None, None]
    qg = q.reshape(B, T, SWA_KV_HEADS, G, SWA_HEAD_DIM)
    s = jnp.einsum('bqhgd,bkhd->bhgqk', qg, kk).astype(jnp.float32) * SWA_SCALE
    s = jnp.where(valid, s - slopes * dist.astype(jnp.float32), -jnp.inf)
    p = sink_softmax(s, sinks.astype(jnp.float32).reshape(SWA_KV_HEADS, G)[:, :, None, None])
    o = jnp.einsum('bhgqk,bkhd->bqhgd', p, vv.astype(jnp.float32))
    return o.reshape(B, T, SWA_Q_W).astype(q.dtype), kk[:, T:], vv[:, T:]


def gla_chunkwise(q, k, v, logf, s0):
    B, T, H, K = q.shape
    V = v.shape[-1]
    C = math.gcd(T, HGRN_CHUNK)
    N = T // C
    q, k, v, logf = (a.reshape(B, N, C, H, a.shape[-1]) for a in (q, k, v, logf))
    b = jnp.cumsum(logf, axis=2)
    b_last = b[:, :, -1:]
    q_dec = q * jnp.exp(b)
    att = jnp.einsum('bnthk,bnshk->bnhts', q_dec, k * jnp.exp(-b))
    att = jnp.where(jnp.tril(jnp.ones((C, C), dtype=bool)), att, 0.0)
    o_intra = jnp.einsum('bnhts,bnshv->bnthv', att, v)
    d_state = jnp.einsum('bnshk,bnshv->bnhkv', k * jnp.exp(b_last - b), v)
    decay = jnp.exp(b_last[:, :, 0])

    def step(S, inp):
        d, ds = inp
        return d[..., None] * S + ds, S

    s_fin, s_start = lax.scan(step, s0, (jnp.moveaxis(decay, 1, 0), jnp.moveaxis(d_state, 1, 0)))
    o_inter = jnp.einsum('bnthk,nbhkv->bnthv', q_dec, s_start)
    return (o_intra + o_inter).reshape(B, T, H, V), s_fin


def hgrn2_mixer(q_raw, f_raw, i_raw, g_raw, lb, out_norm, s0):
    B, T, _ = q_raw.shape
    heads = lambda a: a.astype(jnp.float32).reshape(B, T, HGRN_HEADS, HGRN_HEAD_DIM)
    lb = lb.astype(jnp.float32).reshape(HGRN_HEADS, HGRN_HEAD_DIM)
    f = lb + (1.0 - lb) * jax.nn.sigmoid(heads(f_raw))
    o, s_fin = gla_chunkwise(jax.nn.silu(heads(q_raw)), 1.0 - f, heads(i_raw), jnp.log(f), s0.astype(jnp.float32))
    o = rms_norm(o, out_norm) * jax.nn.silu(heads(g_raw))
    return o.reshape(B, T, HGRN_W).astype(q_raw.dtype), s_fin


def ab_mixer(u, w_in, w_out, sinks, lb, out_norm, k_buf, v_buf, s0):
    B, T, _ = u.shape
    z = u @ w_in
    qa, ka, va, qb, fb, ib, gb = jnp.split(z, np.cumsum(AB_SPLIT_SIZES)[:-1].tolist(), axis=-1)
    qa = qa.reshape(B, T, SWA_HEADS, SWA_HEAD_DIM)
    ka = ka.reshape(B, T, SWA_KV_HEADS, SWA_HEAD_DIM)
    va = va.reshape(B, T, SWA_KV_HEADS, SWA_HEAD_DIM)
    if k_buf is None:
        oa = swa_prompt(qa, ka, va, sinks)
        k_new, v_new = ka[:, -WINDOW:], va[:, -WINDOW:]
        s0 = jnp.zeros((B, HGRN_HEADS, HGRN_HEAD_DIM, HGRN_HEAD_DIM), jnp.float32)
    else:
        oa, k_new, v_new = swa_sample(qa, ka, va, k_buf, v_buf, sinks)
    ob, s_new = hgrn2_mixer(qb, fb, ib, gb, lb, out_norm, s0)
    y = jnp.concatenate([oa, ob], axis=-1) @ w_out
    return y, k_new, v_new, s_new


def mla_project(u, pos, w_dq, q_norm, w_uq, w_dkv, kv_norm, w_uk):
    B, T, _ = u.shape
    q = (rms_norm(u @ w_dq, q_norm) @ w_uq).reshape(B, T, MLA_HEADS, MLA_NOPE_DIM + MLA_ROPE_DIM)
    q_lat = jnp.einsum('bthn,chn->bthc', q[..., :MLA_NOPE_DIM], w_uk)
    q_pe = rope(q[..., MLA_NOPE_DIM:], pos)
    kv = u @ w_dkv
    c_kv = rms_norm(kv[..., :MLA_KV_RANK], kv_norm)
    k_pe = rope(kv[..., MLA_KV_RANK:], pos)
    return q_lat, q_pe, c_kv, k_pe


def mla_scores(q_lat, q_pe, c_kv, k_pe):
    s = jnp.einsum('bthc,bkc->bhtk', q_lat, c_kv) + jnp.einsum('bthr,bkr->bhtk', q_pe, k_pe)
    return s.astype(jnp.float32) * MLA_SCALE


def mla_prompt_attn(q_lat, q_pe, c_kv, k_pe, w_uv):
    B, S = q_lat.shape[0], q_lat.shape[1]
    kpos = jnp.arange(S)

    def block(q0):
        ql = lax.dynamic_slice_in_dim(q_lat, q0, Q_BLOCK, axis=1)
        qp = lax.dynamic_slice_in_dim(q_pe, q0, Q_BLOCK, axis=1)
        s = mla_scores(ql, qp, c_kv, k_pe)
        causal = kpos[None, :] <= (q0 + jnp.arange(Q_BLOCK))[:, None]
        p = jax.nn.softmax(jnp.where(causal, s, -jnp.inf), axis=-1)
        o_lat = jnp.einsum('bhtk,bkc->bthc', p, c_kv.astype(jnp.float32))
        return jnp.einsum('bthc,chv->bthv', o_lat, w_uv.astype(jnp.float32))

    o = lax.map(block, jnp.arange(0, S, Q_BLOCK))
    return jnp.moveaxis(o, 0, 1).reshape(B, S, MLA_HEADS * MLA_V_DIM)


def online_softmax_step(carry, s, c_kv):
    m, l, acc = carry
    m_new = jnp.maximum(m, jnp.max(s, axis=-1))
    alpha = jnp.exp(m - m_new)
    p = jnp.exp(s - m_new[..., None])
    acc = alpha[..., None] * acc + jnp.einsum('bhtk,bkc->bhtc', p, c_kv.astype(jnp.float32))
    return (m_new, alpha * l + jnp.sum(p, axis=-1), acc)


def mla_sample_attn(q_lat, q_pe, c_kv, k_pe, ckv_pool, kpe_pool, layer, page_table, w_uv):
    B, T = q_lat.shape[0], q_lat.shape[1]
    init = (jnp.full((B, MLA_HEADS, T), NEG_INIT, jnp.float32),
            jnp.zeros((B, MLA_HEADS, T), jnp.float32),
            jnp.zeros((B, MLA_HEADS, T, MLA_KV_RANK), jnp.float32))

    def page_step(carry, phys):
        ck = ckv_pool[layer, phys]
        kp = kpe_pool[layer, phys]
        return online_softmax_step(carry, mla_scores(q_lat, q_pe, ck, kp), ck), None

    carry, _ = lax.scan(page_step, init, page_table.T)
    causal = jnp.tril(jnp.ones((T, T), dtype=bool))
    s_new = jnp.where(causal, mla_scores(q_lat, q_pe, c_kv, k_pe), -jnp.inf)
    m, l, acc = online_softmax_step(carry, s_new, c_kv)
    o_lat = acc / l[..., None]
    o = jnp.einsum('bhtc,chv->bthv', o_lat, w_uv.astype(jnp.float32))
    return o.reshape(B, T, MLA_HEADS * MLA_V_DIM)


def mla_mixer(u, pos, w_dq, q_norm, w_uq, w_dkv, kv_norm, w_uk, w_uv, w_o, ckv_pool, kpe_pool, layer, page_table):
    q_lat, q_pe, c_kv, k_pe = mla_project(u, pos, w_dq, q_norm, w_uq, w_dkv, kv_norm, w_uk)
    if page_table is None:
        o = mla_prompt_attn(q_lat, q_pe, c_kv, k_pe, w_uv)
    else:
        o = mla_sample_attn(q_lat, q_pe, c_kv, k_pe, ckv_pool, kpe_pool, layer, page_table, w_uv)
    return o.astype(u.dtype) @ w_o, c_kv, k_pe


def setup_inputs(seed: int = 0) -> dict:
    key = jax.random.key(seed)
    ks = iter(jax.random.split(key, 48))
    nrm = lambda shape, scale: jax.random.normal(next(ks), shape, jnp.float32) * scale
    gain = lambda shape: 1.0 + nrm(shape, 0.05)
    n_pages = PAST_LEN // PAGE_SIZE
    n_used = DEC_BATCH * n_pages
    n_pool = n_used + n_used // 4
    d = D_MODEL
    return {
        'x_prompt': nrm((BATCH, SEQ, d), 1.0),
        'x_sample': nrm((DEC_BATCH, DEC_SEQ, d), 1.0),
        'cache_swa_k': nrm((N_AB_LAYERS, DEC_BATCH, WINDOW, SWA_KV_HEADS, SWA_HEAD_DIM), 1.0),
        'cache_swa_v': nrm((N_AB_LAYERS, DEC_BATCH, WINDOW, SWA_KV_HEADS, SWA_HEAD_DIM), 1.0),
        'state_hgrn': nrm((N_AB_LAYERS, DEC_BATCH, HGRN_HEADS, HGRN_HEAD_DIM, HGRN_HEAD_DIM), 0.5),
        'cache_mla_ckv': nrm((N_MLA_LAYERS, n_pool, PAGE_SIZE, MLA_KV_RANK), 1.0),
        'cache_mla_kpe': nrm((N_MLA_LAYERS, n_pool, PAGE_SIZE, MLA_ROPE_DIM), 1.0),
        'page_table': jax.random.permutation(next(ks), n_pool)[:n_used].reshape(DEC_BATCH, n_pages).astype(jnp.int32),
        'p_prompt': nrm((DEPTH, BATCH, SEQ, PLE_DIM), 1.0),
        'p_sample': nrm((DEPTH, DEC_BATCH, DEC_SEQ, PLE_DIM), 1.0),
        'ln_mix_pre': gain((DEPTH, d)),
        'ln_mix_post': gain((DEPTH, d)),
        'ln_ffn_pre': gain((DEPTH, d)),
        'ln_ffn_post': gain((DEPTH, d)),
        'ln_ple': gain((DEPTH, d)),
        'w_ab_in': nrm((N_AB_LAYERS, d, AB_IN_W), d ** -0.5),
        'w_ab_out': nrm((N_AB_LAYERS, AB_MIX_W, d), AB_MIX_W ** -0.5),
        'swa_sinks': nrm((N_AB_LAYERS, SWA_HEADS), 0.5),
        'hgrn_lb_logits': nrm((N_AB_LAYERS + 1, HGRN_W), 0.1),
        'hgrn_out_norm': gain((N_AB_LAYERS, HGRN_HEAD_DIM)),
        'w_mla_dq': nrm((N_MLA_LAYERS, d, MLA_Q_RANK), d ** -0.5),
        'mla_q_norm': gain((N_MLA_LAYERS, MLA_Q_RANK)),
        'w_mla_uq': nrm((N_MLA_LAYERS, MLA_Q_RANK, MLA_HEADS * (MLA_NOPE_DIM + MLA_ROPE_DIM)), MLA_Q_RANK ** -0.5),
        'w_mla_dkv': nrm((N_MLA_LAYERS, d, MLA_KV_RANK + MLA_ROPE_DIM), d ** -0.5),
        'mla_kv_norm': gain((N_MLA_LAYERS, MLA_KV_RANK)),
        'w_mla_uk': nrm((N_MLA_LAYERS, MLA_KV_RANK, MLA_HEADS, MLA_NOPE_DIM), MLA_KV_RANK ** -0.5),
        'w_mla_uv': nrm((N_MLA_LAYERS, MLA_KV_RANK, MLA_HEADS, MLA_V_DIM), MLA_KV_RANK ** -0.5),
        'w_mla_o': nrm((N_MLA_LAYERS, MLA_HEADS * MLA_V_DIM, d), (MLA_HEADS * MLA_V_DIM) ** -0.5),
        'w_ffn_gate': nrm((DEPTH, d, D_FF), d ** -0.5),
        'w_ffn_up': nrm((DEPTH, d, D_FF), d ** -0.5),
        'w_ffn_down': nrm((DEPTH, D_FF, d), D_FF ** -0.5),
        'w_ple_proj': nrm((DEPTH, PLE_DIM, d), PLE_DIM ** -0.5),
        'w_ple_gate': nrm((DEPTH, d, d), d ** -0.5),
    }


def reference(x_prompt, x_sample, cache_swa_k, cache_swa_v, state_hgrn, cache_mla_ckv, cache_mla_kpe,
              page_table, p_prompt, p_sample,
              ln_mix_pre, ln_mix_post, ln_ffn_pre, ln_ffn_post, ln_ple,
              w_ab_in, w_ab_out, swa_sinks, hgrn_lb_logits, hgrn_out_norm,
              w_mla_dq, mla_q_norm, w_mla_uq, w_mla_dkv, mla_kv_norm, w_mla_uk, w_mla_uv, w_mla_o,
              w_ffn_gate, w_ffn_up, w_ffn_down, w_ple_proj, w_ple_gate):
    lb_all = jnp.cumsum(jax.nn.softmax(hgrn_lb_logits.astype(jnp.float32), axis=0), axis=0)
    pos_p = jnp.arange(SEQ)
    pos_s = PAST_LEN + jnp.arange(DEC_SEQ)

    def finish(h, y, i, p_i):
        h = h + rms_norm(y, ln_mix_post[i])
        h = h + rms_norm(swiglu(rms_norm(h, ln_ffn_pre[i]), w_ffn_gate[i], w_ffn_up[i], w_ffn_down[i]), ln_ffn_post[i])
        e = (p_i @ w_ple_proj[i]) * jax.nn.sigmoid(h @ w_ple_gate[i])
        return h + rms_norm(e, ln_ple[i])

    hp, hs = x_prompt, x_sample
    pk, pv, ph, pc, pr = [], [], [], [], []
    sk, sv, sh, sc, sr = [], [], [], [], []
    for i in range(DEPTH):
        j = i // 2
        up = rms_norm(hp, ln_mix_pre[i])
        us = rms_norm(hs, ln_mix_pre[i])
        if i % 2 == 0:
            yp, k_new, v_new, s_new = ab_mixer(up, w_ab_in[j], w_ab_out[j], swa_sinks[j], lb_all[j],
                                               hgrn_out_norm[j], None, None, None)
            pk.append(k_new); pv.append(v_new); ph.append(s_new)
            ys, k_new, v_new, s_new = ab_mixer(us, w_ab_in[j], w_ab_out[j], swa_sinks[j], lb_all[j],
                                               hgrn_out_norm[j], cache_swa_k[j], cache_swa_v[j], state_hgrn[j])
            sk.append(k_new); sv.append(v_new); sh.append(s_new)
        else:
            yp, c_new, r_new = mla_mixer(up, pos_p, w_mla_dq[j], mla_q_norm[j], w_mla_uq[j], w_mla_dkv[j],
                                         mla_kv_norm[j], w_mla_uk[j], w_mla_uv[j], w_mla_o[j],
                                         None, None, j, None)
            pc.append(c_new); pr.append(r_new)
            ys, c_new, r_new = mla_mixer(us, pos_s, w_mla_dq[j], mla_q_norm[j], w_mla_uq[j], w_mla_dkv[j],
                                         mla_kv_norm[j], w_mla_uk[j], w_mla_uv[j], w_mla_o[j],
                                         cache_mla_ckv, cache_mla_kpe, j, page_table)
            sc.append(c_new); sr.append(r_new)
        hp = finish(hp, yp, i, p_prompt[i])
        hs = finish(hs, ys, i, p_sample[i])

    return (hp, hs,
            jnp.stack(pk), jnp.stack(pv), jnp.stack(ph), jnp.stack(pc), jnp.stack(pr),
            jnp.stack(sk), jnp.stack(sv), jnp.stack(sh), jnp.stack(sc), jnp.stack(sr))
```

```python
import functools

import jax
import jax.numpy as jnp
from jax import lax
from jax.experimental import pallas as pl
from jax.experimental.pallas import tpu as pltpu

F32 = jnp.float32
BF16 = jnp.bfloat16

D_MODEL = 1024
BATCH = 2
SEQ = 8192
DEC_BATCH = 128
DEC_SEQ = 8
PAST_LEN = 16384
PAGE_SIZE = 128
N_PAGES = PAST_LEN // PAGE_SIZE

SWA_HEADS = 8
SWA_KV_HEADS = 2
SWA_GROUP = SWA_HEADS // SWA_KV_HEADS
SWA_HEAD_DIM = 64
WINDOW = 128
SWA_SCALE = SWA_HEAD_DIM ** -0.5
SWA_Q_W = SWA_HEADS * SWA_HEAD_DIM
SWA_KV_W = SWA_KV_HEADS * SWA_HEAD_DIM

HGRN_HEADS = 4
HGRN_HEAD_DIM = 128
HGRN_CHUNK = 32
HGRN_W = HGRN_HEADS * HGRN_HEAD_DIM
AB_IN_W = SWA_Q_W + 2 * SWA_KV_W + 4 * HGRN_W

MLA_HEADS = 16
MLA_NOPE_DIM = 64
MLA_ROPE_DIM = 32
MLA_V_DIM = 64
MLA_Q_RANK = 512
MLA_KV_RANK = 256
MLA_SCALE = (MLA_NOPE_DIM + MLA_ROPE_DIM) ** -0.5
ROPE_THETA = 10000.0
MLA_HEAD_PAD = 128

D_FF = 2816
PLE_DIM = 256
NORM_EPS = 1e-6
NEG = -1e30

VMEM_LIMIT_BYTES = 56 * 1024 * 1024

TOKEN_TILE = 512
HGRN_TILE = 256
SAMPLE_BATCH_TILE = 16
MLA_TQ = 512
MLA_TK = 512
MLA_HEAD_GROUP = 4
PAGES_PER_STEP = 16


def _dot(a, b):
    return jnp.dot(a, b, preferred_element_type=F32)


def _dot_nt(a, b):
    return lax.dot_general(a, b, (((1,), (1,)), ((), ())), preferred_element_type=F32)


def _rms(x, g):
    return x * lax.rsqrt(jnp.mean(x * x, axis=-1, keepdims=True) + NORM_EPS) * g


def _sigmoid(x):
    return 1.0 / (1.0 + jnp.exp(-x))


def _silu(x):
    return x * _sigmoid(x)


def _iota(shape, dim):
    return lax.broadcasted_iota(jnp.int32, shape, dim)


def _params(*sem):
    return pltpu.CompilerParams(dimension_semantics=sem, vmem_limit_bytes=VMEM_LIMIT_BYTES)


def _resident(shape):
    nd = len(shape)
    return pl.BlockSpec(shape, lambda *_: (0,) * nd, pipeline_mode=pl.Buffered(1))


def _smem():
    return pl.BlockSpec(memory_space=pltpu.SMEM)


def _ab_in_body(h_ref, g_ref, w_ref, qa_ref, k_ref, v_ref, hg_ref):
    u = _rms(h_ref[...], g_ref[...]).astype(BF16)
    qa_ref[...] = (_dot(u, w_ref[:, :SWA_Q_W]) * SWA_SCALE).astype(qa_ref.dtype)
    k_ref[...] = _dot(u, w_ref[:, SWA_Q_W:SWA_Q_W + SWA_KV_W])
    v_ref[...] = _dot(u, w_ref[:, SWA_Q_W + SWA_KV_W:SWA_Q_W + 2 * SWA_KV_W])
    hg_ref[...] = _dot(u, w_ref[:, SWA_Q_W + 2 * SWA_KV_W:])


def _ab_in(h, g, w, q_dtype):
    t = h.shape[0]
    tm = TOKEN_TILE
    row = lambda n: pl.BlockSpec((tm, n), lambda i: (i, 0))
    return pl.pallas_call(
        _ab_in_body,
        grid=(t // tm,),
        in_specs=[row(D_MODEL), _resident((1, D_MODEL)), _resident((D_MODEL, AB_IN_W))],
        out_specs=[row(SWA_Q_W), row(SWA_KV_W), row(SWA_KV_W), row(4 * HGRN_W)],
        out_shape=[jax.ShapeDtypeStruct((t, SWA_Q_W), q_dtype),
                   jax.ShapeDtypeStruct((t, SWA_KV_W), F32),
                   jax.ShapeDtypeStruct((t, SWA_KV_W), F32),
                   jax.ShapeDtypeStruct((t, 4 * HGRN_W), F32)],
        compiler_params=_params("parallel"),
        name="ab_in",
    )(h, g, w)


def _swa_head(q, keys, vals, dists, valids, slope, sink):
    scores = []
    for k, dist, valid in zip(keys, dists, valids):
        s = _dot_nt(q, k) - slope * dist
        scores.append(jnp.where(valid, s, NEG))
    m = sink
    for s in scores:
        m = jnp.maximum(m, jnp.max(s, axis=-1, keepdims=True))
    den = jnp.exp(sink - m)
    acc = None
    for s, v in zip(scores, vals):
        p = jnp.exp(s - m)
        den = den + jnp.sum(p, axis=-1, keepdims=True)
        pv = _dot(p.astype(BF16), v)
        acc = pv if acc is None else acc + pv
    return acc / den


def _swa_prompt_body(sink_ref, q_ref, kc_ref, kp_ref, vc_ref, vp_ref, o_ref):
    n = pl.program_id(1)
    row = _iota((WINDOW, WINDOW), 0)
    col = _iota((WINDOW, WINDOW), 1)
    dist_c = (row - col).astype(F32)
    dist_p = dist_c + float(WINDOW)
    valid_c = col <= row
    valid_p = col >= row + jnp.where(n > 0, 0, 2 * WINDOW)
    for hk in range(SWA_KV_HEADS):
        ks = slice(hk * SWA_HEAD_DIM, (hk + 1) * SWA_HEAD_DIM)
        kc = kc_ref[0, :, ks].astype(BF16)
        kp = kp_ref[0, :, ks].astype(BF16)
        vc = vc_ref[0, :, ks].astype(BF16)
        vp = vp_ref[0, :, ks].astype(BF16)
        for g in range(SWA_GROUP):
            j = hk * SWA_GROUP + g
            qs = slice(j * SWA_HEAD_DIM, (j + 1) * SWA_HEAD_DIM)
            o = _swa_head(q_ref[0, :, qs], (kp, kc), (vp, vc), (dist_p, dist_c), (valid_p, valid_c),
                          2.0 ** -(j + 1), sink_ref[j])
            o_ref[0, :, qs] = o.astype(BF16)


def _swa_prompt(sinks, qa, k, v):
    nb = SEQ // WINDOW
    cur = lambda w: pl.BlockSpec((1, WINDOW, w), lambda b, n: (b, n, 0))
    prev = lambda w: pl.BlockSpec((1, WINDOW, w), lambda b, n: (b, jnp.maximum(n - 1, 0), 0))
    return pl.pallas_call(
        _swa_prompt_body,
        grid=(BATCH, nb),
        in_specs=[_smem(), cur(SWA_Q_W), cur(SWA_KV_W), prev(SWA_KV_W), cur(SWA_KV_W), prev(SWA_KV_W)],
        out_specs=cur(SWA_Q_W),
        out_shape=jax.ShapeDtypeStruct((BATCH, SEQ, SWA_Q_W), BF16),
        compiler_params=_params("parallel", "arbitrary"),
        name="swa_prompt",
    )(sinks, qa, k, k, v, v)


def _swa_sample_body(sink_ref, q_ref, kn_ref, vn_ref, kb_ref, vb_ref, o_ref, ko_ref, vo_ref, knp_ref, vnp_ref):
    t_new = DEC_SEQ
    row = _iota((t_new, WINDOW), 0)
    col = _iota((t_new, WINDOW), 1)
    dist_b = (row - col).astype(F32) + float(WINDOW)
    valid_b = col >= row
    dist_n = (row - col).astype(F32)
    valid_n = col <= row
    knp_ref[...] = jnp.zeros_like(knp_ref)
    vnp_ref[...] = jnp.zeros_like(vnp_ref)

    def per_seq(b, carry):
        kb = kb_ref[b]
        vb = vb_ref[b]
        kn = kn_ref[b]
        vn = vn_ref[b]
        ko_ref[b, :WINDOW - t_new, :] = kb[t_new:, :]
        ko_ref[b, WINDOW - t_new:, :] = kn
        vo_ref[b, :WINDOW - t_new, :] = vb[t_new:, :]
        vo_ref[b, WINDOW - t_new:, :] = vn
        knp_ref[:t_new, :] = kn
        vnp_ref[:t_new, :] = vn
        kbb = kb.astype(BF16)
        vbb = vb.astype(BF16)
        knb = knp_ref[...].astype(BF16)
        vnb = vnp_ref[...].astype(BF16)
        q = q_ref[b].astype(BF16)
        for hk in range(SWA_KV_HEADS):
            ks = slice(hk * SWA_HEAD_DIM, (hk + 1) * SWA_HEAD_DIM)
            for g in range(SWA_GROUP):
                j = hk * SWA_GROUP + g
                qs = slice(j * SWA_HEAD_DIM, (j + 1) * SWA_HEAD_DIM)
                o = _swa_head(q[:, qs], (kbb[:, ks], knb[:, ks]), (vbb[:, ks], vnb[:, ks]),
                              (dist_b, dist_n), (valid_b, valid_n), 2.0 ** -(j + 1), sink_ref[j])
                o_ref[b, :, qs] = o
        return carry

    lax.fori_loop(0, SAMPLE_BATCH_TILE, per_seq, 0)


def _swa_sample(sinks, qa, kn, vn, kbuf, vbuf):
    bb = SAMPLE_BATCH_TILE
    blk = lambda r, w: pl.BlockSpec((bb, r, w), lambda i: (i, 0, 0))
    return pl.pallas_call(
        _swa_sample_body,
        grid=(DEC_BATCH // bb,),
        in_specs=[_smem(), blk(DEC_SEQ, SWA_Q_W), blk(DEC_SEQ, SWA_KV_W), blk(DEC_SEQ, SWA_KV_W),
                  blk(WINDOW, SWA_KV_W), blk(WINDOW, SWA_KV_W)],
        out_specs=[blk(DEC_SEQ, SWA_Q_W), blk(WINDOW, SWA_KV_W), blk(WINDOW, SWA_KV_W)],
        out_shape=[jax.ShapeDtypeStruct((DEC_BATCH, DEC_SEQ, SWA_Q_W), F32),
                   jax.ShapeDtypeStruct((DEC_BATCH, WINDOW, SWA_KV_W), F32),
                   jax.ShapeDtypeStruct((DEC_BATCH, WINDOW, SWA_KV_W), F32)],
        scratch_shapes=[pltpu.VMEM((WINDOW, SWA_KV_W), F32), pltpu.VMEM((WINDOW, SWA_KV_W), F32)],
        compiler_params=_params("parallel"),
        name="swa_sample",
    )(sinks, qa, kn, vn, kbuf, vbuf)


def _hgrn_lower_bound(lbl_ref, layer):
    logits = lbl_ref[...]
    e = jnp.exp(logits - jnp.max(logits, axis=0, keepdims=True))
    sm = e / jnp.sum(e, axis=0, keepdims=True)
    return jnp.sum(sm[:layer + 1], axis=0, keepdims=True)


def _split3(x):
    hi = x.astype(BF16)
    r = x - hi.astype(F32)
    mid = r.astype(BF16)
    lo = (r - mid.astype(F32)).astype(BF16)
    return hi, mid, lo


def _hgrn_chunk_terms(x, lb, chunk):
    r = x.shape[0]
    qr, fr, ir, gr = (x[:, i * HGRN_W:(i + 1) * HGRN_W] for i in range(4))
    f = lb + (1.0 - lb) * _sigmoid(fr)
    logf = jnp.log(f)
    row = _iota((r, r), 0)
    col = _iota((r, r), 1)
    chunk_start = row & ~(chunk - 1)
    same = (col >= chunk_start) & (col < chunk_start + chunk)
    tril = (col >= chunk_start) & (col <= row)
    parts = _split3(logf)
    tri_m = jnp.where(tril, 1.0, 0.0).astype(BF16)
    blk_m = jnp.where(same, 1.0, 0.0).astype(BF16)
    b = sum(_dot(tri_m, p) for p in parts)
    bl = sum(_dot(blk_m, p) for p in parts)
    k = 1.0 - f
    qd = _silu(qr) * jnp.exp(b)
    kd = k * jnp.exp(-b)
    k2 = k * jnp.exp(bl - b)
    return qd, kd, k2, ir, gr, bl, tril


def _hgrn_intra(qd, kd, v, tril):
    att = jnp.where(tril, _dot_nt(qd.astype(BF16), kd.astype(BF16)), 0.0)
    return _dot(att.astype(BF16), v.astype(BF16))


def _hgrn_finish(o, g, on):
    return (_rms(o, on) * _silu(g)).astype(BF16)


def _hgrn_prompt_body(lbl_ref, on_ref, x_ref, o_ref, s_ref, st_ref):
    t = pl.program_id(1)

    @pl.when(t == 0)
    def _():
        st_ref[...] = jnp.zeros_like(st_ref)

    lb = _hgrn_lower_bound(lbl_ref, 0)
    qd, kd, k2, v, g, bl, tril = _hgrn_chunk_terms(x_ref[0], lb, HGRN_CHUNK)
    rows = _iota((HGRN_TILE, HGRN_HEAD_DIM), 0)
    n_chunks = HGRN_TILE // HGRN_CHUNK
    for h in range(HGRN_HEADS):
        hs = slice(h * HGRN_HEAD_DIM, (h + 1) * HGRN_HEAD_DIM)
        o_intra = _hgrn_intra(qd[:, hs], kd[:, hs], v[:, hs], tril)
        qdb = qd[:, hs].astype(BF16)
        vt = v[:, hs].T.astype(BF16)
        k2h = k2[:, hs]
        blh = bl[:, hs]
        st = st_ref[h]
        o_inter = []
        for c in range(n_chunks):
            r0 = c * HGRN_CHUNK
            o_inter.append(_dot_nt(qdb[r0:r0 + HGRN_CHUNK], st.astype(BF16)))
            in_chunk = (rows >= r0) & (rows < r0 + HGRN_CHUNK)
            d_st = _dot(vt, jnp.where(in_chunk, k2h, 0.0).astype(BF16))
            st = st * jnp.exp(blh[r0:r0 + 1]) + d_st
        st_ref[h] = st
        o = o_intra + jnp.concatenate(o_inter, axis=0)
        o_ref[0, :, hs] = _hgrn_finish(o, g[:, hs], on_ref[...])

    @pl.when(t == pl.num_programs(1) - 1)
    def _():
        for h in range(HGRN_HEADS):
            s_ref[0, h] = st_ref[h].T


def _hgrn_prompt(lbl, on, hg):
    tt = HGRN_TILE
    return pl.pallas_call(
        _hgrn_prompt_body,
        grid=(BATCH, SEQ // tt),
        in_specs=[_resident(lbl.shape), _resident((1, HGRN_HEAD_DIM)),
                  pl.BlockSpec((1, tt, 4 * HGRN_W), lambda b, t: (b, t, 0))],
        out_specs=[pl.BlockSpec((1, tt, HGRN_W), lambda b, t: (b, t, 0)),
                   pl.BlockSpec((1, HGRN_HEADS, HGRN_HEAD_DIM, HGRN_HEAD_DIM), lambda b, t: (b, 0, 0, 0))],
        out_shape=[jax.ShapeDtypeStruct((BATCH, SEQ, HGRN_W), BF16),
                   jax.ShapeDtypeStruct((BATCH, HGRN_HEADS, HGRN_HEAD_DIM, HGRN_HEAD_DIM), F32)],
        scratch_shapes=[pltpu.VMEM((HGRN_HEADS, HGRN_HEAD_DIM, HGRN_HEAD_DIM), F32)],
        compiler_params=_params("parallel", "arbitrary"),
        name="hgrn_prompt",
    )(lbl, on, hg)


def _hgrn_sample_body(lbl_ref, on_ref, x_ref, s0_ref, o_ref, s_ref):
    bb = SAMPLE_BATCH_TILE
    r = bb * DEC_SEQ
    lb = _hgrn_lower_bound(lbl_ref, 0)
    qd, kd, k2, v, g, bl, tril = _hgrn_chunk_terms(x_ref[...], lb, DEC_SEQ)
    cols = _iota((HGRN_HEAD_DIM, r), 1)
    for h in range(HGRN_HEADS):
        hs = slice(h * HGRN_HEAD_DIM, (h + 1) * HGRN_HEAD_DIM)
        o_intra = _hgrn_intra(qd[:, hs], kd[:, hs], v[:, hs], tril)
        qdb = qd[:, hs].astype(BF16)
        vb = v[:, hs].astype(BF16)
        k2t = k2[:, hs].T
        decay_t = jnp.exp(bl[:, hs].T)
        o_inter = []
        for b in range(bb):
            r0 = b * DEC_SEQ
            s0 = s0_ref[b, h]
            o_inter.append(_dot(qdb[r0:r0 + DEC_SEQ], s0.astype(BF16)))
            in_seq = (cols >= r0) & (cols < r0 + DEC_SEQ)
            d_s = _dot(jnp.where(in_seq, k2t, 0.0).astype(BF16), vb)
            s_ref[b, h] = s0 * decay_t[:, r0:r0 + 1] + d_s
        o = o_intra + jnp.concatenate(o_inter, axis=0)
        o_ref[:, hs] = _hgrn_finish(o, g[:, hs], on_ref[...])


def _hgrn_sample(lbl, on, hg, s0):
    bb = SAMPLE_BATCH_TILE
    r = bb * DEC_SEQ
    st = pl.BlockSpec((bb, HGRN_HEADS, HGRN_HEAD_DIM, HGRN_HEAD_DIM), lambda i: (i, 0, 0, 0))
    return pl.pallas_call(
        _hgrn_sample_body,
        grid=(DEC_BATCH // bb,),
        in_specs=[_resident(lbl.shape), _resident((1, HGRN_HEAD_DIM)),
                  pl.BlockSpec((r, 4 * HGRN_W), lambda i: (i, 0)), st],
        out_specs=[pl.BlockSpec((r, HGRN_W), lambda i: (i, 0)), st],
        out_shape=[jax.ShapeDtypeStruct((DEC_BATCH * DEC_SEQ, HGRN_W), BF16),
                   jax.ShapeDtypeStruct(s0.shape, F32)],
        compiler_params=_params("parallel"),
        name="hgrn_sample",
    )(lbl, on, hg, s0)


FFN_CHUNK = D_FF // 2


def _post_body(n_mix, *refs):
    mix_refs = refs[:n_mix]
    (h_ref, p_ref, wo_ref, g_mix, g_pre, wg_ref, wu_ref, wd_ref, g_post, wpp_ref, wpg_ref, g_ple, o_ref) = refs[n_mix:]
    y = None
    r0 = 0
    for m_ref in mix_refs:
        w = m_ref.shape[-1]
        part = _dot(m_ref[...].astype(BF16), wo_ref[r0:r0 + w, :])
        y = part if y is None else y + part
        r0 += w
    h = h_ref[...] + _rms(y, g_mix[...])
    u = _rms(h, g_pre[...]).astype(BF16)
    ff = None
    for c in range(0, D_FF, FFN_CHUNK):
        gate = _dot(u, wg_ref[:, c:c + FFN_CHUNK])
        up = _dot(u, wu_ref[:, c:c + FFN_CHUNK])
        part = _dot((_silu(gate) * up).astype(BF16), wd_ref[c:c + FFN_CHUNK, :])
        ff = part if ff is None else ff + part
    h = h + _rms(ff, g_post[...])
    e = _dot(p_ref[...].astype(BF16), wpp_ref[...]) * _sigmoid(_dot(h.astype(BF16), wpg_ref[...]))
    o_ref[...] = h + _rms(e, g_ple[...])


def _post(mix, h, p, wo, g_mix, g_pre, wg, wu, wd, g_post, wpp, wpg, g_ple):
    t = h.shape[0]
    tm = TOKEN_TILE
    row = lambda n: pl.BlockSpec((tm, n), lambda i: (i, 0))
    gain = _resident((1, D_MODEL))
    return pl.pallas_call(
        functools.partial(_post_body, len(mix)),
        grid=(t // tm,),
        in_specs=[row(m.shape[-1]) for m in mix] + [
            row(D_MODEL), row(PLE_DIM), _resident(wo.shape), gain, gain,
            _resident(wg.shape), _resident(wu.shape), _resident(wd.shape), gain,
            _resident(wpp.shape), _resident(wpg.shape), gain],
        out_specs=row(D_MODEL),
        out_shape=jax.ShapeDtypeStruct((t, D_MODEL), F32),
        compiler_params=_params("parallel"),
        name="post",
    )(*mix, h, p, wo, g_mix, g_pre, wg, wu, wd, g_post, wpp, wpg, g_ple)


def _rope_tables(pos, invf, sgn):
    ang = pos * invf
    return jnp.cos(ang), jnp.sin(ang) * sgn


def _mla_latents(h_ref, g_pre, wdq_ref, g_q, wkv_ref, g_kv, wpe_ref, wpes_ref, cos32, sin32):
    u = _rms(h_ref[...], g_pre[...]).astype(BF16)
    cqn = _rms(_dot(u, wdq_ref[...]), g_q[...]).astype(BF16)
    ckv = _rms(_dot(u, wkv_ref[...]), g_kv[...])
    kpe = _dot(u, wpe_ref[...]) * cos32 + _dot(u, wpes_ref[...]) * sin32
    return cqn, ckv, kpe


def _mla_pre_prompt_body(h_ref, g_pre, wdq_ref, g_q, wq_ref, wqs_ref, wkv_ref, g_kv, wpe_ref, wpes_ref,
                         wuk_ref, epe_ref, wuv_ref, invf_ref, sgn_ref,
                         q_ref, k_ref, v_ref, ckv_ref, kpe_ref):
    tm = TOKEN_TILE
    i = pl.program_id(0)
    pos = ((i * tm + _iota((tm, 1), 0)) & (SEQ - 1)).astype(F32)
    cos128, sin128 = _rope_tables(pos, invf_ref[...], sgn_ref[...])
    rope = slice(MLA_NOPE_DIM, MLA_NOPE_DIM + MLA_ROPE_DIM)
    cqn, ckv, kpe = _mla_latents(h_ref, g_pre, wdq_ref, g_q, wkv_ref, g_kv, wpe_ref, wpes_ref,
                                 cos128[:, rope], sin128[:, rope])
    ckv_ref[...] = ckv
    kpe_ref[...] = kpe
    ckv_b = ckv.astype(BF16)
    kpe_pad = _dot(kpe.astype(BF16), epe_ref[...])
    for h in range(MLA_HEADS):
        hs = slice(h * MLA_HEAD_PAD, (h + 1) * MLA_HEAD_PAD)
        q = _dot(cqn, wq_ref[:, hs]) * cos128 + _dot(cqn, wqs_ref[:, hs]) * sin128
        q_ref[0, h] = (q * MLA_SCALE).astype(BF16)
        k_ref[0, h] = (_dot(ckv_b, wuk_ref[:, hs]) + kpe_pad).astype(BF16)
        v_ref[0, h] = _dot(ckv_b, wuv_ref[:, hs]).astype(BF16)


def _mla_pre_prompt(h, g_pre, wdq, g_q, wq, wqs, wkv, g_kv, wpe, wpes, wuk, epe, wuv, invf, sgn):
    tm = TOKEN_TILE
    per_seq = SEQ // tm
    heads = pl.BlockSpec((1, MLA_HEADS, tm, MLA_HEAD_PAD), lambda i: (i // per_seq, 0, i % per_seq, 0))
    head_shape = jax.ShapeDtypeStruct((BATCH, MLA_HEADS, SEQ, MLA_HEAD_PAD), BF16)
    row = lambda n: pl.BlockSpec((tm, n), lambda i: (i, 0))
    consts = [g_pre, wdq, g_q, wq, wqs, wkv, g_kv, wpe, wpes, wuk, epe, wuv, invf, sgn]
    return pl.pallas_call(
        _mla_pre_prompt_body,
        grid=(h.shape[0] // tm,),
        in_specs=[row(D_MODEL)] + [_resident(c.shape) for c in consts],
        out_specs=[heads, heads, heads, row(MLA_KV_RANK), row(MLA_ROPE_DIM)],
        out_shape=[head_shape, head_shape, head_shape,
                   jax.ShapeDtypeStruct((h.shape[0], MLA_KV_RANK), F32),
                   jax.ShapeDtypeStruct((h.shape[0], MLA_ROPE_DIM), F32)],
        compiler_params=_params("parallel"),
        name="mla_pre_prompt",
    )(h, *consts)


def _softmax_step(s, v, m_ref, l_ref, acc_ref, idx):
    m_prev = m_ref[idx]
    m_new = jnp.maximum(m_prev, jnp.max(s, axis=-1, keepdims=True))
    alpha = jnp.exp(m_prev - m_new)
    p = jnp.exp(s - m_new)
    l_ref[idx] = alpha * l_ref[idx] + jnp.sum(p, axis=-1, keepdims=True)
    acc_ref[idx] = alpha * acc_ref[idx] + _dot(p.astype(BF16), v)
    m_ref[idx] = m_new


def _mla_prompt_attn_body(q_ref, k_ref, v_ref, o_ref, m_ref, l_ref, acc_ref):
    i = pl.program_id(2)
    m_ref[...] = jnp.full_like(m_ref, NEG)
    l_ref[...] = jnp.zeros_like(l_ref)
    acc_ref[...] = jnp.zeros_like(acc_ref)
    causal = _iota((MLA_TQ, MLA_TK), 1) <= _iota((MLA_TQ, MLA_TK), 0)

    def kv_tile(j, masked):
        k0 = pl.multiple_of(j * MLA_TK, MLA_TK)
        for hh in range(MLA_HEAD_GROUP):
            s = _dot_nt(q_ref[0, hh], k_ref[0, hh, pl.ds(k0, MLA_TK), :])
            if masked:
                s = jnp.where(causal, s, NEG)
            _softmax_step(s, v_ref[0, hh, pl.ds(k0, MLA_TK), :], m_ref, l_ref, acc_ref, hh)

    def below_diagonal(j, carry):
        kv_tile(j, False)
        return carry

    lax.fori_loop(0, i, below_diagonal, 0)
    kv_tile(i, True)
    for hh in range(MLA_HEAD_GROUP):
        o = acc_ref[hh] / l_ref[hh]
        o_ref[0, :, hh * MLA_V_DIM:(hh + 1) * MLA_V_DIM] = o[:, :MLA_V_DIM].astype(BF16)


def _mla_prompt_attn(q, k, v):
    hg = MLA_HEAD_GROUP
    kv_spec = pl.BlockSpec((1, hg, SEQ, MLA_HEAD_PAD), lambda b, g, i: (b, g, 0, 0))
    return pl.pallas_call(
        _mla_prompt_attn_body,
        grid=(BATCH, MLA_HEADS // hg, SEQ // MLA_TQ),
        in_specs=[pl.BlockSpec((1, hg, MLA_TQ, MLA_HEAD_PAD), lambda b, g, i: (b, g, i, 0)), kv_spec, kv_spec],
        out_specs=pl.BlockSpec((1, MLA_TQ, hg * MLA_V_DIM), lambda b, g, i: (b, i, g)),
        out_shape=jax.ShapeDtypeStruct((BATCH, SEQ, MLA_HEADS * MLA_V_DIM), BF16),
        scratch_shapes=[pltpu.VMEM((hg, MLA_TQ, 1), F32), pltpu.VMEM((hg, MLA_TQ, 1), F32),
                        pltpu.VMEM((hg, MLA_TQ, MLA_HEAD_PAD), F32)],
        compiler_params=_params("parallel", "parallel", "arbitrary"),
        name="mla_prompt_attn",
    )(q, k, v)


def _mla_pre_sample_body(h_ref, g_pre, wdq_ref, g_q, wn_ref, wr_ref, wrs_ref, wukt_ref, wkv_ref, g_kv,
                         wpe_ref, wpes_ref, invf_ref, sgn_ref,
                         ql_ref, qp_ref, ckv_ref, kpe_ref):
    tm = TOKEN_TILE
    pos = (PAST_LEN + (_iota((tm, 1), 0) & (DEC_SEQ - 1))).astype(F32)
    cos32, sin32 = _rope_tables(pos, invf_ref[...], sgn_ref[...])
    cqn, ckv, kpe = _mla_latents(h_ref, g_pre, wdq_ref, g_q, wkv_ref, g_kv, wpe_ref, wpes_ref, cos32, sin32)
    ckv_ref[...] = ckv
    kpe_ref[...] = kpe
    n_seq = tm // DEC_SEQ
    for h in range(MLA_HEADS):
        rows = slice(h * DEC_SEQ, (h + 1) * DEC_SEQ)
        q_nope = _dot(cqn, wn_ref[h]).astype(BF16)
        q_lat = _dot(q_nope, wukt_ref[h]) * MLA_SCALE
        ql_ref[:, rows, :] = q_lat.reshape(n_seq, DEC_SEQ, MLA_KV_RANK)
        q_pe = (_dot(cqn, wr_ref[h]) * cos32 + _dot(cqn, wrs_ref[h]) * sin32) * MLA_SCALE
        qp_ref[:, rows, :] = q_pe.reshape(n_seq, DEC_SEQ, MLA_ROPE_DIM)


def _mla_pre_sample(h, g_pre, wdq, g_q, wn, wr, wrs, wukt, wkv, g_kv, wpe, wpes, invf, sgn):
    tm = TOKEN_TILE
    n_seq = tm // DEC_SEQ
    rows = MLA_HEADS * DEC_SEQ
    row = lambda n: pl.BlockSpec((tm, n), lambda i: (i, 0))
    consts = [g_pre, wdq, g_q, wn, wr, wrs, wukt, wkv, g_kv, wpe, wpes, invf, sgn]
    return pl.pallas_call(
        _mla_pre_sample_body,
        grid=(h.shape[0] // tm,),
        in_specs=[row(D_MODEL)] + [_resident(c.shape) for c in consts],
        out_specs=[pl.BlockSpec((n_seq, rows, MLA_KV_RANK), lambda i: (i, 0, 0)),
                   pl.BlockSpec((n_seq, rows, MLA_ROPE_DIM), lambda i: (i, 0, 0)),
                   row(MLA_KV_RANK), row(MLA_ROPE_DIM)],
        out_shape=[jax.ShapeDtypeStruct((DEC_BATCH, rows, MLA_KV_RANK), F32),
                   jax.ShapeDtypeStruct((DEC_BATCH, rows, MLA_ROPE_DIM), F32),
                   jax.ShapeDtypeStruct((h.shape[0], MLA_KV_RANK), F32),
                   jax.ShapeDtypeStruct((h.shape[0], MLA_ROPE_DIM), F32)],
        compiler_params=_params("parallel"),
        name="mla_pre_sample",
    )(h, *consts)


def _mla_sample_attn_body(pt_ref, ql_ref, qp_ref, cn_ref, pn_ref, *rest):
    npg = PAGES_PER_STEP
    ck_refs = rest[:npg]
    kp_refs = rest[npg:2 * npg]
    o_ref, m_ref, l_ref, acc_ref, cnp_ref, pnp_ref = rest[2 * npg:]
    g = pl.program_id(1)
    rows = MLA_HEADS * DEC_SEQ

    @pl.when(g == 0)
    def _():
        m_ref[...] = jnp.full_like(m_ref, NEG)
        l_ref[...] = jnp.zeros_like(l_ref)
        acc_ref[...] = jnp.zeros_like(acc_ref)

    ql = ql_ref[...].astype(BF16)
    qp = qp_ref[...].astype(BF16)
    pages =[r[...].astype(BF16) for r in ck_refs]
    s = jnp.concatenate([_dot_nt(ql, pages[j]) + _dot_nt(qp, kp_refs[j][...].astype(BF16))
                         for j in range(npg)], axis=-1)
    m_prev = m_ref[...]
    m_new = jnp.maximum(m_prev, jnp.max(s, axis=-1, keepdims=True))
    alpha = jnp.exp(m_prev - m_new)
    p = jnp.exp(s - m_new).astype(BF16)
    l_ref[...] = alpha * l_ref[...] + jnp.sum(p.astype(F32), axis=-1, keepdims=True)
    pv = _dot(p[:, :PAGE_SIZE], pages[0])
    for j in range(1, npg):
        pv = pv + _dot(p[:, j * PAGE_SIZE:(j + 1) * PAGE_SIZE], pages[j])
    acc_ref[...] = alpha * acc_ref[...] + pv
    m_ref[...] = m_new

    @pl.when(g == pl.num_programs(1) - 1)
    def _():
        cnp_ref[...] = jnp.zeros_like(cnp_ref)
        pnp_ref[...] = jnp.zeros_like(pnp_ref)
        cnp_ref[:DEC_SEQ, :] = cn_ref[...]
        pnp_ref[:DEC_SEQ, :] = pn_ref[...]
        cn = cnp_ref[...].astype(BF16)
        t_q = _iota((rows, PAGE_SIZE), 0) & (DEC_SEQ - 1)
        s_new = jnp.where(_iota((rows, PAGE_SIZE), 1) <= t_q,
                          _dot_nt(ql, cn) + _dot_nt(qp, pnp_ref[...].astype(BF16)), NEG)
        m_prev = m_ref[...]
        m_new = jnp.maximum(m_prev, jnp.max(s_new, axis=-1, keepdims=True))
        alpha = jnp.exp(m_prev - m_new)
        p = jnp.exp(s_new - m_new)
        l_fin = alpha * l_ref[...] + jnp.sum(p, axis=-1, keepdims=True)
        acc = alpha * acc_ref[...] + _dot(p.astype(BF16), cn)
        o_ref[...] = (acc / l_fin).astype(BF16)


def _mla_sample_attn(page_table, ql, qp, ckv_new, kpe_new, ckv_pool, kpe_pool):
    npg = PAGES_PER_STEP
    rows = MLA_HEADS * DEC_SEQ
    seq = lambda r, w: pl.BlockSpec((None, r, w), lambda b, g, pt: (b, 0, 0))

    def page(width, j):
        return pl.BlockSpec((None, None, PAGE_SIZE, width),
                            lambda b, g, pt: (0, pt[b * N_PAGES + g * npg + j], 0, 0))

    grid_spec = pltpu.PrefetchScalarGridSpec(
        num_scalar_prefetch=1,
        grid=(DEC_BATCH, N_PAGES // npg),
        in_specs=[seq(rows, MLA_KV_RANK), seq(rows, MLA_ROPE_DIM), seq(DEC_SEQ, MLA_KV_RANK), seq(DEC_SEQ, MLA_ROPE_DIM)]
        + [page(MLA_KV_RANK, j) for j in range(npg)] + [page(MLA_ROPE_DIM, j) for j in range(npg)],
        out_specs=seq(rows, MLA_KV_RANK),
        scratch_shapes=[pltpu.VMEM((rows, 1), F32), pltpu.VMEM((rows, 1), F32), pltpu.VMEM((rows, MLA_KV_RANK), F32),
                        pltpu.VMEM((PAGE_SIZE, MLA_KV_RANK), F32), pltpu.VMEM((PAGE_SIZE, MLA_ROPE_DIM), F32)],
    )
    return pl.pallas_call(
        _mla_sample_attn_body,
        grid_spec=grid_spec,
        out_shape=jax.ShapeDtypeStruct((DEC_BATCH, rows, MLA_KV_RANK), BF16),
        compiler_params=_params("parallel", "arbitrary"),
        name="mla_sample_attn",
    )(page_table, ql, qp, ckv_new, kpe_new, *([ckv_pool] * npg), *([kpe_pool] * npg))


def _mla_sample_out_body(x_ref, w_ref, o_ref):
    x = x_ref[...].reshape(DEC_BATCH * 2 * DEC_SEQ, MLA_KV_RANK)
    y = _dot(x, w_ref[0]).reshape(DEC_BATCH, 2 * DEC_SEQ, 2 * MLA_V_DIM)
    first_head = _iota((DEC_BATCH, DEC_SEQ, 2 * MLA_V_DIM), 2) < MLA_V_DIM
    o = jnp.where(first_head, y[:, :DEC_SEQ], y[:, DEC_SEQ:])
    o_ref[...] = o.reshape(DEC_BATCH * DEC_SEQ, 2 * MLA_V_DIM)


def _mla_sample_out(o_lat, wuv_pairs):
    return pl.pallas_call(
        _mla_sample_out_body,
        grid=(MLA_HEADS // 2,),
        in_specs=[pl.BlockSpec((DEC_BATCH, 2 * DEC_SEQ, MLA_KV_RANK), lambda j: (0, j, 0)),
                  pl.BlockSpec((1, MLA_KV_RANK, 2 * MLA_V_DIM), lambda j: (j, 0, 0))],
        out_specs=pl.BlockSpec((DEC_BATCH * DEC_SEQ, 2 * MLA_V_DIM), lambda j: (0, j)),
        out_shape=jax.ShapeDtypeStruct((DEC_BATCH * DEC_SEQ, MLA_HEADS * MLA_V_DIM), F32),
        compiler_params=_params("parallel"),
        name="mla_sample_out",
    )(o_lat, wuv_pairs)


def _swap_halves(w):
    half = w.shape[-1] // 2
    return jnp.concatenate([w[..., half:], w[..., :half]], axis=-1)


def _rope_consts():
    half = MLA_ROPE_DIM // 2
    inv_freq = ROPE_THETA ** (-jnp.arange(0, half, dtype=F32) * 2.0 / MLA_ROPE_DIM)
    invf32 = jnp.concatenate([inv_freq, inv_freq])[None]
    sgn32 = jnp.concatenate([-jnp.ones((half,), F32), jnp.ones((half,), F32)])[None]
    pad = lambda a: jnp.pad(a, ((0, 0), (MLA_NOPE_DIM, MLA_HEAD_PAD - MLA_NOPE_DIM - MLA_ROPE_DIM)))
    return invf32, sgn32, pad(invf32), pad(sgn32)


def _mla_weights(w_uq, w_dkv, w_uk, w_uv):
    wq = w_uq.reshape(MLA_Q_RANK, MLA_HEADS, MLA_NOPE_DIM + MLA_ROPE_DIM)
    wq_nope, wq_rope = wq[..., :MLA_NOPE_DIM], wq[..., MLA_NOPE_DIM:]
    tail = MLA_HEAD_PAD - MLA_NOPE_DIM - MLA_ROPE_DIM
    pad_last = lambda a, lo, hi: jnp.pad(a, [(0, 0)] * (a.ndim - 1) + [(lo, hi)])
    flat = lambda a: a.reshape(a.shape[0], MLA_HEADS * MLA_HEAD_PAD).astype(BF16)
    w = {}
    w["wq"] = flat(pad_last(wq, 0, tail))
    w["wqs"] = flat(pad_last(_swap_halves(wq_rope), MLA_NOPE_DIM, tail))
    w["wuk"] = flat(pad_last(w_uk, 0, MLA_HEAD_PAD - MLA_NOPE_DIM))
    w["wuv"] = flat(pad_last(w_uv, 0, MLA_HEAD_PAD - MLA_V_DIM))
    w["epe"] = pad_last(jnp.eye(MLA_ROPE_DIM, dtype=F32), MLA_NOPE_DIM, tail).astype(BF16)
    w["wn"] = jnp.moveaxis(wq_nope, 1, 0).astype(BF16)
    w["wr"] = jnp.moveaxis(wq_rope, 1, 0).astype(BF16)
    w["wrs"] = jnp.moveaxis(_swap_halves(wq_rope), 1, 0).astype(BF16)
    w["wukt"] = jnp.transpose(w_uk, (1, 2, 0)).astype(BF16)
    w["wkv"] = w_dkv[:, :MLA_KV_RANK].astype(BF16)
    w["wpe"] = w_dkv[:, MLA_KV_RANK:].astype(BF16)
    w["wpes"] = _swap_halves(w_dkv[:, MLA_KV_RANK:]).astype(BF16)
    w["wuv_pairs"] = jnp.transpose(w_uv.reshape(MLA_KV_RANK, MLA_HEADS // 2, 2 * MLA_V_DIM),
                                   (1, 0, 2)).astype(BF16)
    return w


def kernel(x_prompt, x_sample, cache_swa_k, cache_swa_v, state_hgrn, cache_mla_ckv, cache_mla_kpe, page_table, p_prompt, p_sample, ln_mix_pre, ln_mix_post, ln_ffn_pre, ln_ffn_post, ln_ple, w_ab_in, w_ab_out, swa_sinks, hgrn_lb_logits, hgrn_out_norm, w_mla_dq, mla_q_norm, w_mla_uq, w_mla_dkv, mla_kv_norm, w_mla_uk, w_mla_uv, w_mla_o, w_ffn_gate, w_ffn_up, w_ffn_down, w_ple_proj, w_ple_gate):
    tp = BATCH * SEQ
    ts = DEC_BATCH * DEC_SEQ
    bf = lambda a: a.astype(BF16)
    gain = lambda a, i: a[i][None].astype(F32)

    def finish(mix, h, p, i, wo):
        return _post(mix, h, p, wo, gain(ln_mix_post, i), gain(ln_ffn_pre, i), bf(w_ffn_gate[i]), bf(w_ffn_up[i]),
                     bf(w_ffn_down[i]), gain(ln_ffn_post, i), bf(w_ple_proj[i]), bf(w_ple_gate[i]), gain(ln_ple, i))

    hp = x_prompt.reshape(tp, D_MODEL)
    hs = x_sample.reshape(ts, D_MODEL)
    pp = p_prompt.reshape(2, tp, PLE_DIM)
    ps = p_sample.reshape(2, ts, PLE_DIM)

    w_in = bf(w_ab_in[0])
    w_out = bf(w_ab_out[0])
    g0 = gain(ln_mix_pre, 0)
    sinks = swa_sinks[0].astype(F32)
    lbl = hgrn_lb_logits.astype(F32)
    on = hgrn_out_norm[0][None].astype(F32)

    qa, ka, va, hg = _ab_in(hp, g0, w_in, BF16)
    oa = _swa_prompt(sinks, qa.reshape(BATCH, SEQ, SWA_Q_W), ka.reshape(BATCH, SEQ, SWA_KV_W),
                     va.reshape(BATCH, SEQ, SWA_KV_W))
    ob, hgrn_p = _hgrn_prompt(lbl, on, hg.reshape(BATCH, SEQ, 4 * HGRN_W))
    swa_k_p = ka.reshape(BATCH, SEQ, SWA_KV_HEADS, SWA_HEAD_DIM)[:, -WINDOW:]
    swa_v_p = va.reshape(BATCH, SEQ, SWA_KV_HEADS, SWA_HEAD_DIM)[:, -WINDOW:]
    hp = finish([oa.reshape(tp, SWA_Q_W), ob.reshape(tp, HGRN_W)], hp, pp[0], 0, w_out)

    qa, ka, va, hg = _ab_in(hs, g0, w_in, F32)
    oa, swa_k_s, swa_v_s = _swa_sample(
        sinks, qa.reshape(DEC_BATCH, DEC_SEQ, SWA_Q_W), ka.reshape(DEC_BATCH, DEC_SEQ, SWA_KV_W),
        va.reshape(DEC_BATCH, DEC_SEQ, SWA_KV_W), cache_swa_k[0].reshape(DEC_BATCH, WINDOW, SWA_KV_W),
        cache_swa_v[0].reshape(DEC_BATCH, WINDOW, SWA_KV_W))
    ob, hgrn_s = _hgrn_sample(lbl, on, hg, state_hgrn[0])
    hs = finish([oa.reshape(ts, SWA_Q_W), ob], hs, ps[0], 0, w_out)

    mw = _mla_weights(w_mla_uq[0], w_mla_dkv[0], w_mla_uk[0], w_mla_uv[0])
    invf32, sgn32, invf128, sgn128 = _rope_consts()
    g1 = gain(ln_mix_pre, 1)
    wdq = bf(w_mla_dq[0])
    g_q = mla_q_norm[0][None].astype(F32)
    g_kv = mla_kv_norm[0][None].astype(F32)
    w_o = bf(w_mla_o[0])

    q, k, v, ckv_p, kpe_p = _mla_pre_prompt(hp, g1, wdq, g_q, mw["wq"], mw["wqs"], mw["wkv"], g_kv, mw["wpe"],
                                            mw["wpes"], mw["wuk"], mw["epe"], mw["wuv"], invf128, sgn128)
    o = _mla_prompt_attn(q, k, v)
    hp = finish([o.reshape(tp, MLA_HEADS * MLA_V_DIM)], hp, pp[1], 1, w_o)

    ql, qp, ckv_s, kpe_s = _mla_pre_sample(hs, g1, wdq, g_q, mw["wn"], mw["wr"], mw["wrs"], mw["wukt"], mw["wkv"],
                                           g_kv, mw["wpe"], mw["wpes"], invf32, sgn32)
    o_lat = _mla_sample_attn(page_table.reshape(-1), ql, qp, ckv_s.reshape(DEC_BATCH, DEC_SEQ, MLA_KV_RANK),
                             kpe_s.reshape(DEC_BATCH, DEC_SEQ, MLA_ROPE_DIM), cache_mla_ckv, cache_mla_kpe)
    o = _mla_sample_out(o_lat, mw["wuv_pairs"])
    hs = finish([o], hs, ps[1], 1, w_o)

    kv5 = lambda a, n: a.reshape(1, n, WINDOW, SWA_KV_HEADS, SWA_HEAD_DIM)
    return (hp.reshape(BATCH, SEQ, D_MODEL), hs.reshape(DEC_BATCH, DEC_SEQ, D_MODEL),
            swa_k_p[None], swa_v_p[None], hgrn_p[None],
            ckv_p.reshape(1, BATCH, SEQ, MLA_KV_RANK), kpe_p.reshape(1, BATCH, SEQ, MLA_ROPE_DIM),
            kv5(swa_k_s, DEC_BATCH), kv5(swa_v_s, DEC_BATCH), hgrn_s[None],
            ckv_s.reshape(1, DEC_BATCH, DEC_SEQ, MLA_KV_RANK), kpe_s.reshape(1, DEC_BATCH, DEC_SEQ, MLA_ROPE_DIM))
```

```python
import functools

import jax
import jax.numpy as jnp
from jax import lax
from jax.experimental import pallas as pl
from jax.experimental.pallas import tpu as pltpu

F32 = jnp.float32
BF16 = jnp.bfloat16

D_MODEL = 1024
BATCH = 2
SEQ = 8192
DEC_BATCH = 128
DEC_SEQ = 8
PAST_LEN = 16384
PAGE_SIZE = 128
N_PAGES = PAST_LEN // PAGE_SIZE

SWA_HEADS = 8
SWA_KV_HEADS = 2
SWA_GROUP = SWA_HEADS // SWA_KV_HEADS
SWA_HEAD_DIM = 64
WINDOW = 128
SWA_SCALE = SWA_HEAD_DIM ** -0.5
SWA_Q_W = SWA_HEADS * SWA_HEAD_DIM
SWA_KV_W = SWA_KV_HEADS * SWA_HEAD_DIM

HGRN_HEADS = 4
HGRN_HEAD_DIM = 128
HGRN_CHUNK = 32
HGRN_W = HGRN_HEADS * HGRN_HEAD_DIM
AB_IN_W = SWA_Q_W + 2 * SWA_KV_W + 4 * HGRN_W

MLA_HEADS = 16
MLA_NOPE_DIM = 64
MLA_ROPE_DIM = 32
MLA_V_DIM = 64
MLA_Q_RANK = 512
MLA_KV_RANK = 256
MLA_SCALE = (MLA_NOPE_DIM + MLA_ROPE_DIM) ** -0.5
ROPE_THETA = 10000.0
MLA_HEAD_PAD = 128

D_FF = 2816
PLE_DIM = 256
NORM_EPS = 1e-6
NEG = -1e30
LOG2_E = 1.4426950408889634

VMEM_LIMIT_BYTES = 56 * 1024 * 1024

TOKEN_TILE = 512
HGRN_TILE = 256
SAMPLE_BATCH_TILE = 16
MLA_TQ = 512
MLA_TK = 512
MLA_HEAD_GROUP = 8
MLA_VT_ROWS = 80
PAGES_PER_STEP = 32
SAMPLE_PAGE_GROUPS = 2


def _dot(a, b):
    return jnp.dot(a, b, preferred_element_type=F32)


def _dot_nt(a, b):
    return lax.dot_general(a, b, (((1,), (1,)), ((), ())), preferred_element_type=F32)


def _rms(x, g):
    return x * lax.rsqrt(jnp.mean(x * x, axis=-1, keepdims=True) + NORM_EPS) * g


def _sigmoid(x):
    return 1.0 / (1.0 + jnp.exp(-x))


def _silu(x):
    return x * _sigmoid(x)


def _iota(shape, dim):
    return lax.broadcasted_iota(jnp.int32, shape, dim)


def _params(*sem):
    return pltpu.CompilerParams(dimension_semantics=sem, vmem_limit_bytes=VMEM_LIMIT_BYTES)


def _resident(shape):
    nd = len(shape)
    return pl.BlockSpec(shape, lambda *_: (0,) * nd, pipeline_mode=pl.Buffered(1))


def _smem():
    return pl.BlockSpec(memory_space=pltpu.SMEM)


def _ab_in_body(h_ref, g_ref, w_ref, qa_ref, k_ref, v_ref, hg_ref):
    u = _rms(h_ref[...], g_ref[...]).astype(BF16)
    qa_ref[...] = (_dot(u, w_ref[:, :SWA_Q_W]) * SWA_SCALE).astype(qa_ref.dtype)
    k_ref[...] = _dot(u, w_ref[:, SWA_Q_W:SWA_Q_W + SWA_KV_W])
    v_ref[...] = _dot(u, w_ref[:, SWA_Q_W + SWA_KV_W:SWA_Q_W + 2 * SWA_KV_W])
    hg_ref[...] = _dot(u, w_ref[:, SWA_Q_W + 2 * SWA_KV_W:])


def _ab_in(h, g, w, q_dtype):
    t = h.shape[0]
    tm = TOKEN_TILE
    row = lambda n: pl.BlockSpec((tm, n), lambda i: (i, 0))
    return pl.pallas_call(
        _ab_in_body,
        grid=(t // tm,),
        in_specs=[row(D_MODEL), _resident((1, D_MODEL)), _resident((D_MODEL, AB_IN_W))],
        out_specs=[row(SWA_Q_W), row(SWA_KV_W), row(SWA_KV_W), row(4 * HGRN_W)],
        out_shape=[jax.ShapeDtypeStruct((t, SWA_Q_W), q_dtype),
                   jax.ShapeDtypeStruct((t, SWA_KV_W), F32),
                   jax.ShapeDtypeStruct((t, SWA_KV_W), F32),
                   jax.ShapeDtypeStruct((t, 4 * HGRN_W), F32)],
        compiler_params=_params("parallel"),
        name="ab_in",
    )(h, g, w)


def _swa_head(q, keys, vals, dists, valids, slope, sink):
    scores = []
    for k, dist, valid in zip(keys, dists, valids):
        s = _dot_nt(q, k) - slope * dist
        scores.append(jnp.where(valid, s, NEG))
    m = sink
    for s in scores:
        m = jnp.maximum(m, jnp.max(s, axis=-1, keepdims=True))
    den = jnp.exp(sink - m)
    acc = None
    for s, v in zip(scores, vals):
        p = jnp.exp(s - m)
        den = den + jnp.sum(p, axis=-1, keepdims=True)
        pv = _dot(p.astype(BF16), v)
        acc = pv if acc is None else acc + pv
    return acc / den


def _swa_prompt_body(sink_ref, q_ref, kc_ref, kp_ref, vc_ref, vp_ref, o_ref):
    n = pl.program_id(1)
    row = _iota((WINDOW, WINDOW), 0)
    col = _iota((WINDOW, WINDOW), 1)
    dist_c = (row - col).astype(F32)
    dist_p = dist_c + float(WINDOW)
    valid_c = col <= row
    valid_p = col >= row + jnp.where(n > 0, 0, 2 * WINDOW)
    for hk in range(SWA_KV_HEADS):
        ks = slice(hk * SWA_HEAD_DIM, (hk + 1) * SWA_HEAD_DIM)
        kc = kc_ref[0, :, ks].astype(BF16)
        kp = kp_ref[0, :, ks].astype(BF16)
        vc = vc_ref[0, :, ks].astype(BF16)
        vp = vp_ref[0, :, ks].astype(BF16)
        for g in range(SWA_GROUP):
            j = hk * SWA_GROUP + g
            qs = slice(j * SWA_HEAD_DIM, (j + 1) * SWA_HEAD_DIM)
            o = _swa_head(q_ref[0, :, qs], (kp, kc), (vp, vc), (dist_p, dist_c), (valid_p, valid_c),
                          2.0 ** -(j + 1), sink_ref[j])
            o_ref[0, :, qs] = o.astype(BF16)


def _swa_prompt(sinks, qa, k, v):
    nb = SEQ // WINDOW
    cur = lambda w: pl.BlockSpec((1, WINDOW, w), lambda b, n: (b, n, 0))
    prev = lambda w: pl.BlockSpec((1, WINDOW, w), lambda b, n: (b, jnp.maximum(n - 1, 0), 0))
    return pl.pallas_call(
        _swa_prompt_body,
        grid=(BATCH, nb),
        in_specs=[_smem(), cur(SWA_Q_W), cur(SWA_KV_W), prev(SWA_KV_W), cur(SWA_KV_W), prev(SWA_KV_W)],
        out_specs=cur(SWA_Q_W),
        out_shape=jax.ShapeDtypeStruct((BATCH, SEQ, SWA_Q_W), BF16),
        compiler_params=_params("parallel", "arbitrary"),
        name="swa_prompt",
    )(sinks, qa, k, k, v, v)


def _swa_sample_body(sink_ref, q_ref, kn_ref, vn_ref, kb_ref, vb_ref, o_ref, ko_ref, vo_ref):
    bb = SAMPLE_BATCH_TILE
    t_new = DEC_SEQ
    r = bb * t_new
    nk = bb * WINDOW
    kb3, vb3, kn3, vn3 = kb_ref[...], vb_ref[...], kn_ref[...], vn_ref[...]
    ko_ref[:, :WINDOW - t_new, :] = kb3[:, t_new:, :]
    ko_ref[:, WINDOW - t_new:, :] = kn3
    vo_ref[:, :WINDOW - t_new, :] = vb3[:, t_new:, :]
    vo_ref[:, WINDOW - t_new:, :] = vn3
    kb = kb3.reshape(nk, SWA_KV_W).astype(BF16)
    vb = vb3.reshape(nk, SWA_KV_W).astype(BF16)
    kn = kn3.reshape(r, SWA_KV_W).astype(BF16)
    vn = vn3.reshape(r, SWA_KV_W).astype(BF16)

    row = _iota((r, nk), 0)
    col = _iota((r, nk), 1)
    t_q = row & (t_new - 1)
    slot = col & (WINDOW - 1)
    dist_b = (t_q - slot + WINDOW).astype(F32)
    valid_b = ((row >> 3) == (col >> 7)) & (slot >= t_q)
    row_n = _iota((r, r), 0)
    col_n = _iota((r, r), 1)
    dist_n = ((row_n & (t_new - 1)) - (col_n & (t_new - 1))).astype(F32)
    valid_n = ((row_n >> 3) == (col_n >> 3)) & (col_n <= row_n)
    q = q_ref[...].astype(BF16)
    for hk in range(SWA_KV_HEADS):
        ks = slice(hk * SWA_HEAD_DIM, (hk + 1) * SWA_HEAD_DIM)
        for g in range(SWA_GROUP):
            j = hk * SWA_GROUP + g
            qs = slice(j * SWA_HEAD_DIM, (j + 1) * SWA_HEAD_DIM)
            o_ref[:, qs] = _swa_head(q[:, qs], (kb[:, ks], kn[:, ks]), (vb[:, ks], vn[:, ks]),
                                     (dist_b, dist_n), (valid_b, valid_n), 2.0 ** -(j + 1), sink_ref[j])


def _swa_sample(sinks, qa, kn, vn, kbuf, vbuf):
    bb = SAMPLE_BATCH_TILE
    assert DEC_SEQ == 8 and WINDOW == 128
    blk = lambda r, w: pl.BlockSpec((bb, r, w), lambda i: (i, 0, 0))
    rows = pl.BlockSpec((bb * DEC_SEQ, SWA_Q_W), lambda i: (i, 0))
    return pl.pallas_call(
        _swa_sample_body,
        grid=(DEC_BATCH // bb,),
        in_specs=[_smem(), rows, blk(DEC_SEQ, SWA_KV_W), blk(DEC_SEQ, SWA_KV_W),
                  blk(WINDOW, SWA_KV_W), blk(WINDOW, SWA_KV_W)],
        out_specs=[rows, blk(WINDOW, SWA_KV_W), blk(WINDOW, SWA_KV_W)],
        out_shape=[jax.ShapeDtypeStruct((DEC_BATCH * DEC_SEQ, SWA_Q_W), F32),
                   jax.ShapeDtypeStruct((DEC_BATCH, WINDOW, SWA_KV_W), F32),
                   jax.ShapeDtypeStruct((DEC_BATCH, WINDOW, SWA_KV_W), F32)],
        compiler_params=_params("parallel"),
        name="swa_sample",
    )(sinks, qa, kn, vn, kbuf, vbuf)


def _hgrn_lower_bound(lbl_ref, layer):
    logits = lbl_ref[...]
    e = jnp.exp(logits - jnp.max(logits, axis=0, keepdims=True))
    sm = e / jnp.sum(e, axis=0, keepdims=True)
    return jnp.sum(sm[:layer + 1], axis=0, keepdims=True)


def _split3(x):
    hi = x.astype(BF16)
    r = x - hi.astype(F32)
    mid = r.astype(BF16)
    lo = (r - mid.astype(F32)).astype(BF16)
    return hi, mid, lo


def _hgrn_chunk_terms(x, lb, chunk):
    r = x.shape[0]
    qr, fr, ir, gr = (x[:, i * HGRN_W:(i + 1) * HGRN_W] for i in range(4))
    f = lb + (1.0 - lb) * _sigmoid(fr)
    logf = jnp.log(f)
    row = _iota((r, r), 0)
    col = _iota((r, r), 1)
    chunk_start = row & ~(chunk - 1)
    same = (col >= chunk_start) & (col < chunk_start + chunk)
    tril = (col >= chunk_start) & (col <= row)
    parts = _split3(logf)
    tri_m = jnp.where(tril, 1.0, 0.0).astype(BF16)
    blk_m = jnp.where(same, 1.0, 0.0).astype(BF16)
    b = sum(_dot(tri_m, p) for p in parts)
    bl = sum(_dot(blk_m, p) for p in parts)
    k = 1.0 - f
    qd = _silu(qr) * jnp.exp(b)
    kd = k * jnp.exp(-b)
    k2 = k * jnp.exp(bl - b)
    return qd, kd, k2, ir, gr, bl, tril


def _hgrn_intra(qd, kd, v, tril):
    att = jnp.where(tril, _dot_nt(qd.astype(BF16), kd.astype(BF16)), 0.0)
    return _dot(att.astype(BF16), v.astype(BF16))


def _hgrn_finish(o, g, on):
    return (_rms(o, on) * _silu(g)).astype(BF16)


def _hgrn_prompt_body(lbl_ref, on_ref, x_ref, o_ref, s_ref, st_ref):
    t = pl.program_id(1)

    @pl.when(t == 0)
    def _():
        st_ref[...] = jnp.zeros_like(st_ref)

    lb = _hgrn_lower_bound(lbl_ref, 0)
    qd, kd, k2, v, g, bl, tril = _hgrn_chunk_terms(x_ref[0], lb, HGRN_CHUNK)
    rows = _iota((HGRN_TILE, HGRN_HEAD_DIM), 0)
    n_chunks = HGRN_TILE // HGRN_CHUNK
    for h in range(HGRN_HEADS):
        hs = slice(h * HGRN_HEAD_DIM, (h + 1) * HGRN_HEAD_DIM)
        o_intra = _hgrn_intra(qd[:, hs], kd[:, hs], v[:, hs], tril)
        qdb = qd[:, hs].astype(BF16)
        vt = v[:, hs].T.astype(BF16)
        k2h = k2[:, hs]
        blh = bl[:, hs]
        st = st_ref[h]
        o_inter = []
        for c in range(n_chunks):
            r0 = c * HGRN_CHUNK
            o_inter.append(_dot_nt(qdb[r0:r0 + HGRN_CHUNK], st.astype(BF16)))
            in_chunk = (rows >= r0) & (rows < r0 + HGRN_CHUNK)
            d_st = _dot(vt, jnp.where(in_chunk, k2h, 0.0).astype(BF16))
            st = st * jnp.exp(blh[r0:r0 + 1]) + d_st
        st_ref[h] = st
        o = o_intra + jnp.concatenate(o_inter, axis=0)
        o_ref[0, :, hs] = _hgrn_finish(o, g[:, hs], on_ref[...])

    @pl.when(t == pl.num_programs(1) - 1)
    def _():
        for h in range(HGRN_HEADS):
            s_ref[0, h] = st_ref[h].T


def _hgrn_prompt(lbl, on, hg):
    tt = HGRN_TILE
    return pl.pallas_call(
        _hgrn_prompt_body,
        grid=(BATCH, SEQ // tt),
        in_specs=[_resident(lbl.shape), _resident((1, HGRN_HEAD_DIM)),
                  pl.BlockSpec((1, tt, 4 * HGRN_W), lambda b, t: (b, t, 0))],
        out_specs=[pl.BlockSpec((1, tt, HGRN_W), lambda b, t: (b, t, 0)),
                   pl.BlockSpec((1, HGRN_HEADS, HGRN_HEAD_DIM, HGRN_HEAD_DIM), lambda b, t: (b, 0, 0, 0))],
        out_shape=[jax.ShapeDtypeStruct((BATCH, SEQ, HGRN_W), BF16),
                   jax.ShapeDtypeStruct((BATCH, HGRN_HEADS, HGRN_HEAD_DIM, HGRN_HEAD_DIM), F32)],
        scratch_shapes=[pltpu.VMEM((HGRN_HEADS, HGRN_HEAD_DIM, HGRN_HEAD_DIM), F32)],
        compiler_params=_params("parallel", "arbitrary"),
        name="hgrn_prompt",
    )(lbl, on, hg)


def _hgrn_sample_body(lbl_ref, on_ref, x_ref, s0_ref, o_ref, s_ref):
    bb = SAMPLE_BATCH_TILE
    r = bb * DEC_SEQ
    lb = _hgrn_lower_bound(lbl_ref, 0)
    qd, kd, k2, v, g, bl, tril = _hgrn_chunk_terms(x_ref[...], lb, DEC_SEQ)
    cols = _iota((HGRN_HEAD_DIM, r), 1)
    for h in range(HGRN_HEADS):
        hs = slice(h * HGRN_HEAD_DIM, (h + 1) * HGRN_HEAD_DIM)
        o_intra = _hgrn_intra(qd[:, hs], kd[:, hs], v[:, hs], tril)
        qdb = qd[:, hs].astype(BF16)
        vb = v[:, hs].astype(BF16)
        k2t = k2[:, hs].T
        decay_t = jnp.exp(bl[:, hs].T)
        o_inter = []
        for b in range(bb):
            r0 = b * DEC_SEQ
            s0 = s0_ref[b, h]
            o_inter.append(_dot(qdb[r0:r0 + DEC_SEQ], s0.astype(BF16)))
            in_seq = (cols >= r0) & (cols < r0 + DEC_SEQ)
            d_s = _dot(jnp.where(in_seq, k2t, 0.0).astype(BF16), vb)
            s_ref[b, h] = s0 * decay_t[:, r0:r0 + 1] + d_s
        o = o_intra + jnp.concatenate(o_inter, axis=0)
        o_ref[:, hs] = _hgrn_finish(o, g[:, hs], on_ref[...])


def _hgrn_sample(lbl, on, hg, s0):
    bb = SAMPLE_BATCH_TILE
    r = bb * DEC_SEQ
    st = pl.BlockSpec((bb, HGRN_HEADS, HGRN_HEAD_DIM, HGRN_HEAD_DIM), lambda i: (i, 0, 0, 0))
    return pl.pallas_call(
        _hgrn_sample_body,
        grid=(DEC_BATCH // bb,),
        in_specs=[_resident(lbl.shape), _resident((1, HGRN_HEAD_DIM)),
                  pl.BlockSpec((r, 4 * HGRN_W), lambda i: (i, 0)), st],
        out_specs=[pl.BlockSpec((r, HGRN_W), lambda i: (i, 0)), st],
        out_shape=[jax.ShapeDtypeStruct((DEC_BATCH * DEC_SEQ, HGRN_W), BF16),
                   jax.ShapeDtypeStruct(s0.shape, F32)],
        compiler_params=_params("parallel"),
        name="hgrn_sample",
    )(lbl, on, hg, s0)


FFN_CHUNK = D_FF // 2


def _post_body(n_mix, *refs):
    mix_refs = refs[:n_mix]
    (h_ref, p_ref, wo_ref, g_mix, g_pre, wg_ref, wu_ref, wd_ref, g_post, wpp_ref, wpg_ref, g_ple, o_ref) = refs[n_mix:]
    y = None
    r0 = 0
    for m_ref in mix_refs:
        w = m_ref.shape[-1]
        part = _dot(m_ref[...].astype(BF16), wo_ref[r0:r0 + w, :])
        y = part if y is None else y + part
        r0 += w
    h = h_ref[...] + _rms(y, g_mix[...])
    u = _rms(h, g_pre[...]).astype(BF16)
    ff = None
    for c in range(0, D_FF, FFN_CHUNK):
        gate = _dot(u, wg_ref[:, c:c + FFN_CHUNK])
        up = _dot(u, wu_ref[:, c:c + FFN_CHUNK])
        part = _dot((_silu(gate) * up).astype(BF16), wd_ref[c:c + FFN_CHUNK, :])
        ff = part if ff is None else ff + part
    h = h + _rms(ff, g_post[...])
    e = _dot(p_ref[...].astype(BF16), wpp_ref[...]) * _sigmoid(_dot(h.astype(BF16), wpg_ref[...]))
    o_ref[...] = h + _rms(e, g_ple[...])


def _post(mix, h, p, wo, g_mix, g_pre, wg, wu, wd, g_post, wpp, wpg, g_ple):
    t = h.shape[0]
    tm = TOKEN_TILE
    row = lambda n: pl.BlockSpec((tm, n), lambda i: (i, 0))
    gain = _resident((1, D_MODEL))
    return pl.pallas_call(
        functools.partial(_post_body, len(mix)),
        grid=(t // tm,),
        in_specs=[row(m.shape[-1]) for m in mix] + [
            row(D_MODEL), row(PLE_DIM), _resident(wo.shape), gain, gain,
            _resident(wg.shape), _resident(wu.shape), _resident(wd.shape), gain,
            _resident(wpp.shape), _resident(wpg.shape), gain],
        out_specs=row(D_MODEL),
        out_shape=jax.ShapeDtypeStruct((t, D_MODEL), F32),
        compiler_params=_params("parallel"),
        name="post",
    )(*mix, h, p, wo, g_mix, g_pre, wg, wu, wd, g_post, wpp, wpg, g_ple)


def _rope_tables(pos, invf, sgn):
    ang = pos * invf
    return jnp.cos(ang), jnp.sin(ang) * sgn


def _mla_latents(h_ref, g_pre, wdq_ref, g_q, wkv_ref, g_kv, wpe_ref, wpes_ref, cos32, sin32):
    u = _rms(h_ref[...], g_pre[...]).astype(BF16)
    cqn = _rms(_dot(u, wdq_ref[...]), g_q[...]).astype(BF16)
    ckv = _rms(_dot(u, wkv_ref[...]), g_kv[...])
    kpe = _dot(u, wpe_ref[...]) * cos32 + _dot(u, wpes_ref[...]) * sin32
    return cqn, ckv, kpe


def _mla_pre_prompt_body(h_ref, g_pre, wdq_ref, g_q, wq_ref, wqs_ref, wkv_ref, g_kv, wpe_ref, wpes_ref,
                         wuk_ref, epe_ref, wuvt_ref, ones_ref, invf_ref, sgn_ref,
                         q_ref, k_ref, vt_ref, ckv_ref, kpe_ref):
    tm = TOKEN_TILE
    i = pl.program_id(0)
    pos = ((i * tm + _iota((tm, 1), 0)) & (SEQ - 1)).astype(F32)
    cos128, sin128 = _rope_tables(pos, invf_ref[...], sgn_ref[...])
    rope = slice(MLA_NOPE_DIM, MLA_NOPE_DIM + MLA_ROPE_DIM)
    cqn, ckv, kpe = _mla_latents(h_ref, g_pre, wdq_ref, g_q, wkv_ref, g_kv, wpe_ref, wpes_ref,
                                 cos128[:, rope], sin128[:, rope])
    ckv_ref[...] = ckv
    kpe_ref[...] = kpe
    ckv_b = ckv.astype(BF16)
    kpe_pad = _dot(kpe.astype(BF16), epe_ref[...])
    vt_ref[0, 0] = (_dot_nt(wuvt_ref[...], ckv_b) + ones_ref[...]).astype(BF16)
    for h in range(MLA_HEADS):
        hs = slice(h * MLA_HEAD_PAD, (h + 1) * MLA_HEAD_PAD)
        q = _dot(cqn, wq_ref[:, hs]) * cos128 + _dot(cqn, wqs_ref[:, hs]) * sin128
        q_ref[0, h] = (q * (MLA_SCALE * LOG2_E)).astype(BF16)
        k_ref[0, h] = (_dot(ckv_b, wuk_ref[:, hs]) + kpe_pad).astype(BF16)


def _mla_pre_prompt(h, g_pre, wdq, g_q, wq, wqs, wkv, g_kv, wpe, wpes, wuk, epe, wuvt, ones_col, invf, sgn):
    tm = TOKEN_TILE
    assert tm == MLA_TK
    per_seq = SEQ // tm
    heads = pl.BlockSpec((1, MLA_HEADS, tm, MLA_HEAD_PAD), lambda i: (i // per_seq, 0, i % per_seq, 0))
    head_shape = jax.ShapeDtypeStruct((BATCH, MLA_HEADS, SEQ, MLA_HEAD_PAD), BF16)
    vt_rows = MLA_HEADS * MLA_VT_ROWS
    row = lambda n: pl.BlockSpec((tm, n), lambda i: (i, 0))
    consts = [g_pre, wdq, g_q, wq, wqs, wkv, g_kv, wpe, wpes, wuk, epe, wuvt, ones_col, invf, sgn]
    return pl.pallas_call(
        _mla_pre_prompt_body,
        grid=(h.shape[0] // tm,),
        in_specs=[row(D_MODEL)] + [_resident(c.shape) for c in consts],
        out_specs=[heads, heads,
                   pl.BlockSpec((1, 1, vt_rows, tm), lambda i: (i // per_seq, i % per_seq, 0, 0)),
                   row(MLA_KV_RANK), row(MLA_ROPE_DIM)],
        out_shape=[head_shape, head_shape,
                   jax.ShapeDtypeStruct((BATCH, per_seq, vt_rows, tm), BF16),
                   jax.ShapeDtypeStruct((h.shape[0], MLA_KV_RANK), F32),
                   jax.ShapeDtypeStruct((h.shape[0], MLA_ROPE_DIM), F32)],
        compiler_params=_params("parallel"),
        name="mla_pre_prompt",
    )(h, *consts)


def _mla_prompt_attn_body(q_ref, k_ref, vt_ref, o_ref, m_ref, acc_ref):
    i = pl.program_id(2)
    m_ref[...] = jnp.full_like(m_ref, NEG)
    acc_ref[...] = jnp.zeros_like(acc_ref)
    causal = _iota((MLA_TK, MLA_TQ), 0) <= _iota((MLA_TK, MLA_TQ), 1)

    def kv_tile(j, masked):
        k0 = pl.multiple_of(j * MLA_TK, MLA_TK)

        def scores(hh):
            s = _dot_nt(k_ref[0, hh, pl.ds(k0, MLA_TK), :], q_ref[0, hh])
            return jnp.where(causal, s, NEG) if masked else s

        def softmax(hh, s):
            m_prev = m_ref[hh]
            m_new = jnp.maximum(m_prev, jnp.max(s, axis=0, keepdims=True))
            m_ref[hh] = m_new
            return jnp.exp2(s - m_new).astype(BF16), jnp.exp2(m_prev - m_new)

        def values(hh, p, alpha):
            vt = vt_ref[0, j, hh * MLA_VT_ROWS:(hh + 1) * MLA_VT_ROWS, :]
            acc_ref[hh] = alpha * acc_ref[hh] + _dot(vt, p)

        s_of, p_of = {}, {}
        for step in range(MLA_HEAD_GROUP + 2):
            if step < MLA_HEAD_GROUP:
                s_of[step] = scores(step)
            if 0 <= step - 1 < MLA_HEAD_GROUP:
                p_of[step - 1] = softmax(step - 1, s_of.pop(step - 1))
            if 0 <= step - 2 < MLA_HEAD_GROUP:
                values(step - 2, *p_of.pop(step - 2))

    def below_diagonal(j, carry):
        kv_tile(j, False)
        return carry

    lax.fori_loop(0, i, below_diagonal, 0)
    kv_tile(i, True)
    for pair in range(MLA_HEAD_GROUP // 2):
        heads = []
        for hh in (2 * pair, 2 * pair + 1):
            acc = acc_ref[hh]
            heads.append(acc[:MLA_V_DIM] / acc[MLA_V_DIM:MLA_V_DIM + 1])
        o_t = jnp.concatenate(heads, axis=0)
        o_ref[0, :, pair * 2 * MLA_V_DIM:(pair + 1) * 2 * MLA_V_DIM] = o_t.T.astype(BF16)


def _mla_prompt_attn(q, k, vt):
    hg = MLA_HEAD_GROUP
    assert MLA_TQ == MLA_TK and hg % 2 == 0
    return pl.pallas_call(
        _mla_prompt_attn_body,
        grid=(BATCH, MLA_HEADS // hg, SEQ // MLA_TQ),
        in_specs=[pl.BlockSpec((1, hg, MLA_TQ, MLA_HEAD_PAD), lambda b, g, i: (b, g, i, 0)),
                  pl.BlockSpec((1, hg, SEQ, MLA_HEAD_PAD), lambda b, g, i: (b, g, 0, 0),
                               pipeline_mode=pl.Buffered(1)),
                  pl.BlockSpec((1, SEQ // MLA_TK, hg * MLA_VT_ROWS, MLA_TK), lambda b, g, i: (b, 0, g, 0),
                               pipeline_mode=pl.Buffered(1))],
        out_specs=pl.BlockSpec((1, MLA_TQ, hg * MLA_V_DIM), lambda b, g, i: (b, i, g)),
        out_shape=jax.ShapeDtypeStruct((BATCH, SEQ, MLA_HEADS * MLA_V_DIM), BF16),
        scratch_shapes=[pltpu.VMEM((hg, 1, MLA_TQ), F32), pltpu.VMEM((hg, MLA_VT_ROWS, MLA_TQ), F32)],
        compiler_params=_params("parallel", "parallel", "arbitrary"),
        name="mla_prompt_attn",
    )(q, k, vt)


def _mla_pre_sample_body(h_ref, g_pre, wdq_ref, g_q, wn_ref, wr_ref, wrs_ref, wukt_ref, wkv_ref, g_kv,
                         wpe_ref, wpes_ref, invf_ref, sgn_ref,
                         ql_ref, qp_ref, ckv_ref, kpe_ref):
    tm = TOKEN_TILE
    pos = (PAST_LEN + (_iota((tm, 1), 0) & (DEC_SEQ - 1))).astype(F32)
    cos32, sin32 = _rope_tables(pos, invf_ref[...], sgn_ref[...])
    cqn, ckv, kpe = _mla_latents(h_ref, g_pre, wdq_ref, g_q, wkv_ref, g_kv, wpe_ref, wpes_ref, cos32, sin32)
    ckv_ref[...] = ckv
    kpe_ref[...] = kpe
    n_seq = tm // DEC_SEQ
    for h in range(MLA_HEADS):
        rows = slice(h * DEC_SEQ, (h + 1) * DEC_SEQ)
        q_nope = _dot(cqn, wn_ref[h]).astype(BF16)
        q_lat = _dot(q_nope, wukt_ref[h]) * (MLA_SCALE * LOG2_E)
        ql_ref[:, rows, :] = q_lat.reshape(n_seq, DEC_SEQ, MLA_KV_RANK)
        q_pe = (_dot(cqn, wr_ref[h]) * cos32 + _dot(cqn, wrs_ref[h]) * sin32) * (MLA_SCALE * LOG2_E)
        qp_ref[:, rows, :] = q_pe.reshape(n_seq, DEC_SEQ, MLA_ROPE_DIM)


def _mla_pre_sample(h, g_pre, wdq, g_q, wn, wr, wrs, wukt, wkv, g_kv, wpe, wpes, invf, sgn):
    tm = TOKEN_TILE
    n_seq = tm // DEC_SEQ
    rows = MLA_HEADS * DEC_SEQ
    row = lambda n: pl.BlockSpec((tm, n), lambda i: (i, 0))
    consts = [g_pre, wdq, g_q, wn, wr, wrs, wukt, wkv, g_kv, wpe, wpes, invf, sgn]
    return pl.pallas_call(
        _mla_pre_sample_body,
        grid=(h.shape[0] // tm,),
        in_specs=[row(D_MODEL)] + [_resident(c.shape) for c in consts],
        out_specs=[pl.BlockSpec((n_seq, rows, MLA_KV_RANK), lambda i: (i, 0, 0)),
                   pl.BlockSpec((n_seq, rows, MLA_ROPE_DIM), lambda i: (i, 0, 0)),
                   row(MLA_KV_RANK), row(MLA_ROPE_DIM)],
        out_shape=[jax.ShapeDtypeStruct((DEC_BATCH, rows, MLA_KV_RANK), F32),
                   jax.ShapeDtypeStruct((DEC_BATCH, rows, MLA_ROPE_DIM), F32),
                   jax.ShapeDtypeStruct((h.shape[0], MLA_KV_RANK), F32),
                   jax.ShapeDtypeStruct((h.shape[0], MLA_ROPE_DIM), F32)],
        compiler_params=_params("parallel"),
        name="mla_pre_sample",
    )(h, *consts)


def _mla_sample_attn_body(pt_ref, ql_ref, qp_ref, cn_ref, pn_ref, *rest):
    npg = PAGES_PER_STEP
    ck_refs = rest[:npg]
    kp_refs = rest[npg:2 * npg]
    o_ref, m_ref, l_ref, acc_ref, cnp_ref, pnp_ref = rest[2 * npg:]
    g = pl.program_id(1)
    rows = MLA_HEADS * DEC_SEQ

    @pl.when(g == 0)
    def _():
        m_ref[...] = jnp.full_like(m_ref, NEG)
        l_ref[...] = jnp.zeros_like(l_ref)
        acc_ref[...] = jnp.zeros_like(acc_ref)

    ql = ql_ref[...].astype(BF16)
    qp = qp_ref[...].astype(BF16)

    def update(s, keys):
        m_prev = m_ref[...]
        m_new = jnp.maximum(m_prev, jnp.max(s, axis=-1, keepdims=True))
        alpha = jnp.exp2(m_prev - m_new)
        p = jnp.exp2(s - m_new)
        l_ref[...] = alpha * l_ref[...] + jnp.sum(p, axis=-1, keepdims=True)
        acc_ref[...] = alpha * acc_ref[...] + _dot(p.astype(BF16), keys)
        m_ref[...] = m_new

    per_group = npg // SAMPLE_PAGE_GROUPS
    groups = []
    for c in range(SAMPLE_PAGE_GROUPS):
        sel = range(c * per_group, (c + 1) * per_group)
        kl = jnp.concatenate([ck_refs[j][...].astype(BF16) for j in sel], axis=0)
        kpt = jnp.concatenate([kp_refs[j][...].astype(BF16) for j in sel], axis=1)
        groups.append((_dot_nt(ql, kl) + _dot(qp, kpt), kl))
    for s, kl in groups:
        update(s, kl)

    @pl.when(g == pl.num_programs(1) - 1)
    def _():
        cnp_ref[...] = jnp.zeros_like(cnp_ref)
        pnp_ref[...] = jnp.zeros_like(pnp_ref)
        cnp_ref[:DEC_SEQ, :] = cn_ref[...]
        pnp_ref[:DEC_SEQ, :] = pn_ref[...]
        cn = cnp_ref[...].astype(BF16)
        t_q = _iota((rows, PAGE_SIZE), 0) & (DEC_SEQ - 1)
        s_new = jnp.where(_iota((rows, PAGE_SIZE), 1) <= t_q,
                          _dot_nt(ql, cn) + _dot_nt(qp, pnp_ref[...].astype(BF16)), NEG)
        update(s_new, cn)
        o_ref[...] = (acc_ref[...] / l_ref[...]).astype(BF16)


def _mla_sample_attn(page_table, ql, qp, ckv_new, kpe_new, ckv_pool, kpe_pool):
    npg = PAGES_PER_STEP
    rows = MLA_HEADS * DEC_SEQ
    seq = lambda r, w: pl.BlockSpec((None, r, w), lambda b, g, pt: (b, 0, 0))

    def page(shape, j):
        return pl.BlockSpec((None, None) + shape, lambda b, g, pt: (0, pt[b * N_PAGES + g * npg + j], 0, 0))

    grid_spec = pltpu.PrefetchScalarGridSpec(
        num_scalar_prefetch=1,
        grid=(DEC_BATCH, N_PAGES // npg),
        in_specs=[seq(rows, MLA_KV_RANK), seq(rows, MLA_ROPE_DIM), seq(DEC_SEQ, MLA_KV_RANK), seq(DEC_SEQ, MLA_ROPE_DIM)]
        + [page((PAGE_SIZE, MLA_KV_RANK), j) for j in range(npg)]
        + [page((MLA_ROPE_DIM, PAGE_SIZE), j) for j in range(npg)],
        out_specs=seq(rows, MLA_KV_RANK),
        scratch_shapes=[pltpu.VMEM((rows, 1), F32), pltpu.VMEM((rows, 1), F32), pltpu.VMEM((rows, MLA_KV_RANK), F32),
                        pltpu.VMEM((PAGE_SIZE, MLA_KV_RANK), F32), pltpu.VMEM((PAGE_SIZE, MLA_ROPE_DIM), F32)],
    )
    return pl.pallas_call(
        _mla_sample_attn_body,
        grid_spec=grid_spec,
        out_shape=jax.ShapeDtypeStruct((DEC_BATCH, rows, MLA_KV_RANK), BF16),
        compiler_params=_params("parallel", "arbitrary"),
        name="mla_sample_attn",
    )(page_table, ql, qp, ckv_new, kpe_new, *([ckv_pool] * npg), *([kpe_pool] * npg))


def _mla_sample_out_body(x_ref, w_ref, o_ref):
    x = x_ref[...].reshape(DEC_BATCH * 2 * DEC_SEQ, MLA_KV_RANK)
    y = _dot(x, w_ref[0]).reshape(DEC_BATCH, 2 * DEC_SEQ, 2 * MLA_V_DIM)
    first_head = _iota((DEC_BATCH, DEC_SEQ, 2 * MLA_V_DIM), 2) < MLA_V_DIM
    o = jnp.where(first_head, y[:, :DEC_SEQ], y[:, DEC_SEQ:])
    o_ref[...] = o.reshape(DEC_BATCH * DEC_SEQ, 2 * MLA_V_DIM)


def _mla_sample_out(o_lat, wuv_pairs):
    return pl.pallas_call(
        _mla_sample_out_body,
        grid=(MLA_HEADS // 2,),
        in_specs=[pl.BlockSpec((DEC_BATCH, 2 * DEC_SEQ, MLA_KV_RANK), lambda j: (0, j, 0)),
                  pl.BlockSpec((1, MLA_KV_RANK, 2 * MLA_V_DIM), lambda j: (j, 0, 0))],
        out_specs=pl.BlockSpec((DEC_BATCH * DEC_SEQ, 2 * MLA_V_DIM), lambda j: (0, j)),
        out_shape=jax.ShapeDtypeStruct((DEC_BATCH * DEC_SEQ, MLA_HEADS * MLA_V_DIM), F32),
        compiler_params=_params("parallel"),
        name="mla_sample_out",
    )(o_lat, wuv_pairs)


def _swap_halves(w):
    half = w.shape[-1] // 2
    return jnp.concatenate([w[..., half:], w[..., :half]], axis=-1)


def _rope_consts():
    half = MLA_ROPE_DIM // 2
    inv_freq = ROPE_THETA ** (-jnp.arange(0, half, dtype=F32) * 2.0 / MLA_ROPE_DIM)
    invf32 = jnp.concatenate([inv_freq, inv_freq])[None]
    sgn32 = jnp.concatenate([-jnp.ones((half,), F32), jnp.ones((half,), F32)])[None]
    pad = lambda a: jnp.pad(a, ((0, 0), (MLA_NOPE_DIM, MLA_HEAD_PAD - MLA_NOPE_DIM - MLA_ROPE_DIM)))
    return invf32, sgn32, pad(invf32), pad(sgn32)


def _mla_weights(w_uq, w_dkv, w_uk, w_uv):
    wq = w_uq.reshape(MLA_Q_RANK, MLA_HEADS, MLA_NOPE_DIM + MLA_ROPE_DIM)
    wq_nope, wq_rope = wq[..., :MLA_NOPE_DIM], wq[..., MLA_NOPE_DIM:]
    tail = MLA_HEAD_PAD - MLA_NOPE_DIM - MLA_ROPE_DIM
    pad_last = lambda a, lo, hi: jnp.pad(a, [(0, 0)] * (a.ndim - 1) + [(lo, hi)])
    flat = lambda a: a.reshape(a.shape[0], MLA_HEADS * MLA_HEAD_PAD).astype(BF16)
    w = {}
    w["wq"] = flat(pad_last(wq, 0, tail))
    w["wqs"] = flat(pad_last(_swap_halves(wq_rope), MLA_NOPE_DIM, tail))
    w["wuk"] = flat(pad_last(w_uk, 0, MLA_HEAD_PAD - MLA_NOPE_DIM))
    wuvt = jnp.transpose(w_uv, (1, 2, 0))
    w["wuvt"] = jnp.pad(wuvt, ((0, 0), (0, MLA_VT_ROWS - MLA_V_DIM), (0, 0))).reshape(
        MLA_HEADS * MLA_VT_ROWS, MLA_KV_RANK).astype(BF16)
    w["ones_col"] = jnp.tile((jnp.arange(MLA_VT_ROWS) == MLA_V_DIM).astype(F32), MLA_HEADS)[:, None]
    w["epe"] = pad_last(jnp.eye(MLA_ROPE_DIM, dtype=F32), MLA_NOPE_DIM, tail).astype(BF16)
    w["wn"] = jnp.moveaxis(wq_nope, 1, 0).astype(BF16)
    w["wr"] = jnp.moveaxis(wq_rope, 1, 0).astype(BF16)
    w["wrs"] = jnp.moveaxis(_swap_halves(wq_rope), 1, 0).astype(BF16)
    w["wukt"] = jnp.transpose(w_uk, (1, 2, 0)).astype(BF16)
    w["wkv"] = w_dkv[:, :MLA_KV_RANK].astype(BF16)
    w["wpe"] = w_dkv[:, MLA_KV_RANK:].astype(BF16)
    w["wpes"] = _swap_halves(w_dkv[:, MLA_KV_RANK:]).astype(BF16)
    w["wuv_pairs"] = jnp.transpose(w_uv.reshape(MLA_KV_RANK, MLA_HEADS // 2, 2 * MLA_V_DIM),
                                   (1, 0, 2)).astype(BF16)
    return w


def kernel(x_prompt, x_sample, cache_swa_k, cache_swa_v, state_hgrn, cache_mla_ckv, cache_mla_kpe, page_table, p_prompt, p_sample, ln_mix_pre, ln_mix_post, ln_ffn_pre, ln_ffn_post, ln_ple, w_ab_in, w_ab_out, swa_sinks, hgrn_lb_logits, hgrn_out_norm, w_mla_dq, mla_q_norm, w_mla_uq, w_mla_dkv, mla_kv_norm, w_mla_uk, w_mla_uv, w_mla_o, w_ffn_gate, w_ffn_up, w_ffn_down, w_ple_proj, w_ple_gate):
    tp = BATCH * SEQ
    ts = DEC_BATCH * DEC_SEQ
    bf = lambda a: a.astype(BF16)
    gain = lambda a, i: a[i][None].astype(F32)

    def finish(mix, h, p, i, wo):
        return _post(mix, h, p, wo, gain(ln_mix_post, i), gain(ln_ffn_pre, i), bf(w_ffn_gate[i]), bf(w_ffn_up[i]),
                     bf(w_ffn_down[i]), gain(ln_ffn_post, i), bf(w_ple_proj[i]), bf(w_ple_gate[i]), gain(ln_ple, i))

    hp = x_prompt.reshape(tp, D_MODEL)
    hs = x_sample.reshape(ts, D_MODEL)
    pp = p_prompt.reshape(2, tp, PLE_DIM)
    ps = p_sample.reshape(2, ts, PLE_DIM)

    w_in = bf(w_ab_in[0])
    w_out = bf(w_ab_out[0])
    g0 = gain(ln_mix_pre, 0)
    sinks = swa_sinks[0].astype(F32)
    lbl = hgrn_lb_logits.astype(F32)
    on = hgrn_out_norm[0][None].astype(F32)

    qa, ka, va, hg = _ab_in(hp, g0, w_in, BF16)
    oa = _swa_prompt(sinks, qa.reshape(BATCH, SEQ, SWA_Q_W), ka.reshape(BATCH, SEQ, SWA_KV_W),
                     va.reshape(BATCH, SEQ, SWA_KV_W))
    ob, hgrn_p = _hgrn_prompt(lbl, on, hg.reshape(BATCH, SEQ, 4 * HGRN_W))
    swa_k_p = ka.reshape(BATCH, SEQ, SWA_KV_HEADS, SWA_HEAD_DIM)[:, -WINDOW:]
    swa_v_p = va.reshape(BATCH, SEQ, SWA_KV_HEADS, SWA_HEAD_DIM)[:, -WINDOW:]
    hp = finish([oa.reshape(tp, SWA_Q_W), ob.reshape(tp, HGRN_W)], hp, pp[0], 0, w_out)

    qa, ka, va, hg = _ab_in(hs, g0, w_in, F32)
    oa, swa_k_s, swa_v_s = _swa_sample(
        sinks, qa, ka.reshape(DEC_BATCH, DEC_SEQ, SWA_KV_W),
        va.reshape(DEC_BATCH, DEC_SEQ, SWA_KV_W), cache_swa_k[0].reshape(DEC_BATCH, WINDOW, SWA_KV_W),
        cache_swa_v[0].reshape(DEC_BATCH, WINDOW, SWA_KV_W))
    ob, hgrn_s = _hgrn_sample(lbl, on, hg, state_hgrn[0])
    hs = finish([oa.reshape(ts, SWA_Q_W), ob], hs, ps[0], 0, w_out)

    mw = _mla_weights(w_mla_uq[0], w_mla_dkv[0], w_mla_uk[0], w_mla_uv[0])
    invf32, sgn32, invf128, sgn128 = _rope_consts()
    g1 = gain(ln_mix_pre, 1)
    wdq = bf(w_mla_dq[0])
    g_q = mla_q_norm[0][None].astype(F32)
    g_kv = mla_kv_norm[0][None].astype(F32)
    w_o = bf(w_mla_o[0])

    q, k, vt, ckv_p, kpe_p = _mla_pre_prompt(hp, g1, wdq, g_q, mw["wq"], mw["wqs"], mw["wkv"], g_kv, mw["wpe"],
                                             mw["wpes"], mw["wuk"], mw["epe"], mw["wuvt"], mw["ones_col"],
                                             invf128, sgn128)
    o = _mla_prompt_attn(q, k, vt)
    hp = finish([o.reshape(tp, MLA_HEADS * MLA_V_DIM)], hp, pp[1], 1, w_o)

    ql, qp, ckv_s, kpe_s = _mla_pre_sample(hs, g1, wdq, g_q, mw["wn"], mw["wr"], mw["wrs"], mw["wukt"], mw["wkv"],
                                           g_kv, mw["wpe"], mw["wpes"], invf32, sgn32)
    o_lat = _mla_sample_attn(page_table.reshape(-1), ql, qp, ckv_s.reshape(DEC_BATCH, DEC_SEQ, MLA_KV_RANK),
                             kpe_s.reshape(DEC_BATCH, DEC_SEQ, MLA_ROPE_DIM), cache_mla_ckv,
                             jnp.swapaxes(cache_mla_kpe, 2, 3))
    o = _mla_sample_out(o_lat, mw["wuv_pairs"])
    hs = finish([o], hs, ps[1], 1, w_o)

    kv5 = lambda a, n: a.reshape(1, n, WINDOW, SWA_KV_HEADS, SWA_HEAD_DIM)
    return (hp.reshape(BATCH, SEQ, D_MODEL), hs.reshape(DEC_BATCH, DEC_SEQ, D_MODEL),
            swa_k_p[None], swa_v_p[None], hgrn_p[None],
            ckv_p.reshape(1, BATCH, SEQ, MLA_KV_RANK), kpe_p.reshape(1, BATCH, SEQ, MLA_ROPE_DIM),
            kv5(swa_k_s, DEC_BATCH), kv5(swa_v_s, DEC_BATCH), hgrn_s[None],
            ckv_s.reshape(1, DEC_BATCH, DEC_SEQ, MLA_KV_RANK), kpe_s.reshape(1, DEC_BATCH, DEC_SEQ, MLA_ROPE_DIM))
```

```python
import functools

import jax
import jax.numpy as jnp
from jax import lax
from jax.experimental import pallas as pl
from jax.experimental.pallas import tpu as pltpu

F32 = jnp.float32
BF16 = jnp.bfloat16

D_MODEL = 1024
BATCH = 2
SEQ = 8192
DEC_BATCH = 128
DEC_SEQ = 8
PAST_LEN = 16384
PAGE_SIZE = 128
N_PAGES = PAST_LEN // PAGE_SIZE

SWA_HEADS = 8
SWA_KV_HEADS = 2
SWA_GROUP = SWA_HEADS // SWA_KV_HEADS
SWA_HEAD_DIM = 64
WINDOW = 128
SWA_SCALE = SWA_HEAD_DIM ** -0.5
SWA_Q_W = SWA_HEADS * SWA_HEAD_DIM
SWA_KV_W = SWA_KV_HEADS * SWA_HEAD_DIM

HGRN_HEADS = 4
HGRN_HEAD_DIM = 128
HGRN_CHUNK = 32
HGRN_W = HGRN_HEADS * HGRN_HEAD_DIM
AB_IN_W = SWA_Q_W + 2 * SWA_KV_W + 4 * HGRN_W

MLA_HEADS = 16
MLA_NOPE_DIM = 64
MLA_ROPE_DIM = 32
MLA_V_DIM = 64
MLA_Q_RANK = 512
MLA_KV_RANK = 256
MLA_SCALE = (MLA_NOPE_DIM + MLA_ROPE_DIM) ** -0.5
ROPE_THETA = 10000.0
MLA_HEAD_PAD = 128

D_FF = 2816
PLE_DIM = 256
NORM_EPS = 1e-6
NEG = -1e30
LOG2_E = 1.4426950408889634

VMEM_LIMIT_BYTES = 56 * 1024 * 1024

TOKEN_TILE = 512
HGRN_TILE = 256
SWA_TQ = 256
SAMPLE_BATCH_TILE = 16
MLA_TQ = 512
MLA_TK = 512
MLA_HEAD_GROUP = 8
MLA_VT_ROWS = 80
PAGES_PER_GROUP = 32
PAGES_PER_CHUNK = 16
PAGE_RING = 3


def _dot(a, b):
    return jnp.dot(a, b, preferred_element_type=F32)


def _dot_nt(a, b):
    return lax.dot_general(a, b, (((1,), (1,)), ((), ())), preferred_element_type=F32)


def _rms(x, g):
    return x * lax.rsqrt(jnp.mean(x * x, axis=-1, keepdims=True) + NORM_EPS) * g


def _sigmoid(x):
    return 1.0 / (1.0 + jnp.exp(-x))


def _silu(x):
    return x * _sigmoid(x)


def _iota(shape, dim):
    return lax.broadcasted_iota(jnp.int32, shape, dim)


def _params(*sem):
    return pltpu.CompilerParams(dimension_semantics=sem, vmem_limit_bytes=VMEM_LIMIT_BYTES)


def _resident(shape):
    nd = len(shape)
    return pl.BlockSpec(shape, lambda *_: (0,) * nd, pipeline_mode=pl.Buffered(1))


def _smem():
    return pl.BlockSpec(memory_space=pltpu.SMEM)


def _ab_in_body(h_ref, g_ref, w_ref, qa_ref, k_ref, v_ref, hg_ref):
    u = _rms(h_ref[...], g_ref[...]).astype(BF16)
    qa_ref[...] = (_dot(u, w_ref[:, :SWA_Q_W]) * SWA_SCALE).astype(qa_ref.dtype)
    k_ref[...] = _dot(u, w_ref[:, SWA_Q_W:SWA_Q_W + SWA_KV_W])
    v_ref[...] = _dot(u, w_ref[:, SWA_Q_W + SWA_KV_W:SWA_Q_W + 2 * SWA_KV_W])
    hg_ref[...] = _dot(u, w_ref[:, SWA_Q_W + 2 * SWA_KV_W:])


def _ab_in(h, g, w, q_dtype):
    t = h.shape[0]
    tm = TOKEN_TILE
    row = lambda n: pl.BlockSpec((tm, n), lambda i: (i, 0))
    return pl.pallas_call(
        _ab_in_body,
        grid=(t // tm,),
        in_specs=[row(D_MODEL), _resident((1, D_MODEL)), _resident((D_MODEL, AB_IN_W))],
        out_specs=[row(SWA_Q_W), row(SWA_KV_W), row(SWA_KV_W), row(4 * HGRN_W)],
        out_shape=[jax.ShapeDtypeStruct((t, SWA_Q_W), q_dtype),
                   jax.ShapeDtypeStruct((t, SWA_KV_W), F32),
                   jax.ShapeDtypeStruct((t, SWA_KV_W), F32),
                   jax.ShapeDtypeStruct((t, 4 * HGRN_W), F32)],
        compiler_params=_params("parallel"),
        name="ab_in",
    )(h, g, w)


def _swa_head(q, keys, vals, dists, valids, slope, sink):
    scores = []
    for k, dist, valid in zip(keys, dists, valids):
        s = _dot_nt(q, k) - slope * dist
        scores.append(jnp.where(valid, s, NEG))
    m = sink
    for s in scores:
        m = jnp.maximum(m, jnp.max(s, axis=-1, keepdims=True))
    den = jnp.exp(sink - m)
    acc = None
    for s, v in zip(scores, vals):
        p = jnp.exp(s - m)
        den = den + jnp.sum(p, axis=-1, keepdims=True)
        pv = _dot(p.astype(BF16), v)
        acc = pv if acc is None else acc + pv
    return acc / den


def _swa_prompt_body(sink_ref, q_ref, kc_ref, kp_ref, vc_ref, vp_ref, o_ref):
    i = pl.program_id(1)
    nk = WINDOW + SWA_TQ
    kk = jnp.concatenate([kp_ref[0], kc_ref[0]], axis=0).astype(BF16)
    vv_t = jnp.concatenate([vp_ref[0], vc_ref[0]], axis=0).T.astype(BF16)
    key = _iota((nk, SWA_TQ), 0)
    qry = _iota((nk, SWA_TQ), 1)
    dist_i = qry + WINDOW - key
    valid = (dist_i >= 0) & (dist_i <= WINDOW) & (key >= jnp.where(i > 0, 0, WINDOW))
    dist = dist_i.astype(F32)
    q = q_ref[0]
    heads = []
    for hk in range(SWA_KV_HEADS):
        ks = slice(hk * SWA_HEAD_DIM, (hk + 1) * SWA_HEAD_DIM)
        for g in range(SWA_GROUP):
            j = hk * SWA_GROUP + g
            s = _dot_nt(kk[:, ks], q[:, j * SWA_HEAD_DIM:(j + 1) * SWA_HEAD_DIM]) - 2.0 ** -(j + 1) * dist
            s = jnp.where(valid, s, NEG)
            m = jnp.maximum(jnp.max(s, axis=0, keepdims=True), sink_ref[j])
            p = jnp.exp(s - m)
            den = jnp.sum(p, axis=0, keepdims=True) + jnp.exp(sink_ref[j] - m)
            heads.append(_dot(vv_t[ks, :], p.astype(BF16)) / den)
    for pair in range(SWA_HEADS // 2):
        o_t = jnp.concatenate(heads[2 * pair:2 * pair + 2], axis=0)
        o_ref[0, :, pair * 2 * SWA_HEAD_DIM:(pair + 1) * 2 * SWA_HEAD_DIM] = o_t.T.astype(BF16)


def _swa_prompt(sinks, qa, k, v):
    per_tile = SWA_TQ // WINDOW
    cur = lambda w: pl.BlockSpec((1, SWA_TQ, w), lambda b, i: (b, i, 0))
    prev = lambda w: pl.BlockSpec((1, WINDOW, w), lambda b, i: (b, jnp.maximum(i * per_tile - 1, 0), 0))
    return pl.pallas_call(
        _swa_prompt_body,
        grid=(BATCH, SEQ // SWA_TQ),
        in_specs=[_smem(), cur(SWA_Q_W), cur(SWA_KV_W), prev(SWA_KV_W), cur(SWA_KV_W), prev(SWA_KV_W)],
        out_specs=cur(SWA_Q_W),
        out_shape=jax.ShapeDtypeStruct((BATCH, SEQ, SWA_Q_W), BF16),
        compiler_params=_params("parallel", "arbitrary"),
        name="swa_prompt",
    )(sinks, qa, k, k, v, v)


def _swa_sample_body(sink_ref, q_ref, kn_ref, vn_ref, kb_ref, vb_ref, o_ref, ko_ref, vo_ref):
    bb = SAMPLE_BATCH_TILE
    t_new = DEC_SEQ
    r = bb * t_new
    nk = bb * WINDOW
    kb3, vb3, kn3, vn3 = kb_ref[...], vb_ref[...], kn_ref[...], vn_ref[...]
    ko_ref[:, :WINDOW - t_new, :] = kb3[:, t_new:, :]
    ko_ref[:, WINDOW - t_new:, :] = kn3
    vo_ref[:, :WINDOW - t_new, :] = vb3[:, t_new:, :]
    vo_ref[:, WINDOW - t_new:, :] = vn3
    kb = kb3.reshape(nk, SWA_KV_W).astype(BF16)
    vb = vb3.reshape(nk, SWA_KV_W).astype(BF16)
    kn = kn3.reshape(r, SWA_KV_W).astype(BF16)
    vn = vn3.reshape(r, SWA_KV_W).astype(BF16)

    row = _iota((r, nk), 0)
    col = _iota((r, nk), 1)
    t_q = row & (t_new - 1)
    slot = col & (WINDOW - 1)
    dist_b = (t_q - slot + WINDOW).astype(F32)
    valid_b = ((row >> 3) == (col >> 7)) & (slot >= t_q)
    row_n = _iota((r, r), 0)
    col_n = _iota((r, r), 1)
    dist_n = ((row_n & (t_new - 1)) - (col_n & (t_new - 1))).astype(F32)
    valid_n = ((row_n >> 3) == (col_n >> 3)) & (col_n <= row_n)
    q = q_ref[...].astype(BF16)
    for hk in range(SWA_KV_HEADS):
        ks = slice(hk * SWA_HEAD_DIM, (hk + 1) * SWA_HEAD_DIM)
        for g in range(SWA_GROUP):
            j = hk * SWA_GROUP + g
            qs = slice(j * SWA_HEAD_DIM, (j + 1) * SWA_HEAD_DIM)
            o_ref[:, qs] = _swa_head(q[:, qs], (kb[:, ks], kn[:, ks]), (vb[:, ks], vn[:, ks]),
                                     (dist_b, dist_n), (valid_b, valid_n), 2.0 ** -(j + 1), sink_ref[j])


def _swa_sample(sinks, qa, kn, vn, kbuf, vbuf):
    bb = SAMPLE_BATCH_TILE
    assert DEC_SEQ == 8 and WINDOW == 128
    blk = lambda r, w: pl.BlockSpec((bb, r, w), lambda i: (i, 0, 0))
    rows = pl.BlockSpec((bb * DEC_SEQ, SWA_Q_W), lambda i: (i, 0))
    return pl.pallas_call(
        _swa_sample_body,
        grid=(DEC_BATCH // bb,),
        in_specs=[_smem(), rows, blk(DEC_SEQ, SWA_KV_W), blk(DEC_SEQ, SWA_KV_W),
                  blk(WINDOW, SWA_KV_W), blk(WINDOW, SWA_KV_W)],
        out_specs=[rows, blk(WINDOW, SWA_KV_W), blk(WINDOW, SWA_KV_W)],
        out_shape=[jax.ShapeDtypeStruct((DEC_BATCH * DEC_SEQ, SWA_Q_W), F32),
                   jax.ShapeDtypeStruct((DEC_BATCH, WINDOW, SWA_KV_W), F32),
                   jax.ShapeDtypeStruct((DEC_BATCH, WINDOW, SWA_KV_W), F32)],
        compiler_params=_params("parallel"),
        name="swa_sample",
    )(sinks, qa, kn, vn, kbuf, vbuf)


def _hgrn_lower_bound(lbl_ref, layer):
    logits = lbl_ref[...]
    e = jnp.exp(logits - jnp.max(logits, axis=0, keepdims=True))
    sm = e / jnp.sum(e, axis=0, keepdims=True)
    return jnp.sum(sm[:layer + 1], axis=0, keepdims=True)


def _split3(x):
    hi = x.astype(BF16)
    r = x - hi.astype(F32)
    mid = r.astype(BF16)
    lo = (r - mid.astype(F32)).astype(BF16)
    return hi, mid, lo


def _hgrn_chunk_terms(x, lb, chunk):
    r = x.shape[0]
    qr, fr, ir, gr = (x[:, i * HGRN_W:(i + 1) * HGRN_W] for i in range(4))
    f = lb + (1.0 - lb) * _sigmoid(fr)
    logf = jnp.log(f)
    row = _iota((r, r), 0)
    col = _iota((r, r), 1)
    chunk_start = row & ~(chunk - 1)
    same = (col >= chunk_start) & (col < chunk_start + chunk)
    tril = (col >= chunk_start) & (col <= row)
    parts = _split3(logf)
    tri_m = jnp.where(tril, 1.0, 0.0).astype(BF16)
    blk_m = jnp.where(same, 1.0, 0.0).astype(BF16)
    b = sum(_dot(tri_m, p) for p in parts)
    bl = sum(_dot(blk_m, p) for p in parts)
    k = 1.0 - f
    qd = _silu(qr) * jnp.exp(b)
    kd = k * jnp.exp(-b)
    k2 = k * jnp.exp(bl - b)
    return qd, kd, k2, ir, gr, bl, tril


def _hgrn_intra(qd, kd, v, tril):
    att = jnp.where(tril, _dot_nt(qd.astype(BF16), kd.astype(BF16)), 0.0)
    return _dot(att.astype(BF16), v.astype(BF16))


def _hgrn_finish(o, g, on):
    return (_rms(o, on) * _silu(g)).astype(BF16)


def _hgrn_prompt_body(lbl_ref, on_ref, x_ref, o_ref, s_ref, st_ref):
    t = pl.program_id(1)

    @pl.when(t == 0)
    def _():
        st_ref[...] = jnp.zeros_like(st_ref)

    lb = _hgrn_lower_bound(lbl_ref, 0)
    qd, kd, k2, v, g, bl, tril = _hgrn_chunk_terms(x_ref[0], lb, HGRN_CHUNK)
    rows = _iota((HGRN_TILE, HGRN_HEAD_DIM), 0)
    n_chunks = HGRN_TILE // HGRN_CHUNK
    for h in range(HGRN_HEADS):
        hs = slice(h * HGRN_HEAD_DIM, (h + 1) * HGRN_HEAD_DIM)
        o_intra = _hgrn_intra(qd[:, hs], kd[:, hs], v[:, hs], tril)
        qdb = qd[:, hs].astype(BF16)
        vt = v[:, hs].T.astype(BF16)
        k2h = k2[:, hs]
        blh = bl[:, hs]
        st = st_ref[h]
        o_inter = []
        for c in range(n_chunks):
            r0 = c * HGRN_CHUNK
            o_inter.append(_dot_nt(qdb[r0:r0 + HGRN_CHUNK], st.astype(BF16)))
            in_chunk = (rows >= r0) & (rows < r0 + HGRN_CHUNK)
            d_st = _dot(vt, jnp.where(in_chunk, k2h, 0.0).astype(BF16))
            st = st * jnp.exp(blh[r0:r0 + 1]) + d_st
        st_ref[h] = st
        o = o_intra + jnp.concatenate(o_inter, axis=0)
        o_ref[0, :, hs] = _hgrn_finish(o, g[:, hs], on_ref[...])

    @pl.when(t == pl.num_programs(1) - 1)
    def _():
        for h in range(HGRN_HEADS):
            s_ref[0, h] = st_ref[h].T


def _hgrn_prompt(lbl, on, hg):
    tt = HGRN_TILE
    return pl.pallas_call(
        _hgrn_prompt_body,
        grid=(BATCH, SEQ // tt),
        in_specs=[_resident(lbl.shape), _resident((1, HGRN_HEAD_DIM)),
                  pl.BlockSpec((1, tt, 4 * HGRN_W), lambda b, t: (b, t, 0))],
        out_specs=[pl.BlockSpec((1, tt, HGRN_W), lambda b, t: (b, t, 0)),
                   pl.BlockSpec((1, HGRN_HEADS, HGRN_HEAD_DIM, HGRN_HEAD_DIM), lambda b, t: (b, 0, 0, 0))],
        out_shape=[jax.ShapeDtypeStruct((BATCH, SEQ, HGRN_W), BF16),
                   jax.ShapeDtypeStruct((BATCH, HGRN_HEADS, HGRN_HEAD_DIM, HGRN_HEAD_DIM), F32)],
        scratch_shapes=[pltpu.VMEM((HGRN_HEADS, HGRN_HEAD_DIM, HGRN_HEAD_DIM), F32)],
        compiler_params=_params("parallel", "arbitrary"),
        name="hgrn_prompt",
    )(lbl, on, hg)


def _hgrn_sample_body(lbl_ref, on_ref, x_ref, s0_ref, o_ref, s_ref):
    bb = SAMPLE_BATCH_TILE
    r = bb * DEC_SEQ
    lb = _hgrn_lower_bound(lbl_ref, 0)
    qd, kd, k2, v, g, bl, tril = _hgrn_chunk_terms(x_ref[...], lb, DEC_SEQ)
    cols = _iota((HGRN_HEAD_DIM, r), 1)
    for h in range(HGRN_HEADS):
        hs = slice(h * HGRN_HEAD_DIM, (h + 1) * HGRN_HEAD_DIM)
        o_intra = _hgrn_intra(qd[:, hs], kd[:, hs], v[:, hs], tril)
        qdb = qd[:, hs].astype(BF16)
        vb = v[:, hs].astype(BF16)
        k2t = k2[:, hs].T
        decay_t = jnp.exp(bl[:, hs].T)
        o_inter = []
        for b in range(bb):
            r0 = b * DEC_SEQ
            s0 = s0_ref[b, h]
            o_inter.append(_dot(qdb[r0:r0 + DEC_SEQ], s0.astype(BF16)))
            in_seq = (cols >= r0) & (cols < r0 + DEC_SEQ)
            d_s = _dot(jnp.where(in_seq, k2t, 0.0).astype(BF16), vb)
            s_ref[b, h] = s0 * decay_t[:, r0:r0 + 1] + d_s
        o = o_intra + jnp.concatenate(o_inter, axis=0)
        o_ref[:, hs] = _hgrn_finish(o, g[:, hs], on_ref[...])


def _hgrn_sample(lbl, on, hg, s0):
    bb = SAMPLE_BATCH_TILE
    r = bb * DEC_SEQ
    st = pl.BlockSpec((bb, HGRN_HEADS, HGRN_HEAD_DIM, HGRN_HEAD_DIM), lambda i: (i, 0, 0, 0))
    return pl.pallas_call(
        _hgrn_sample_body,
        grid=(DEC_BATCH // bb,),
        in_specs=[_resident(lbl.shape), _resident((1, HGRN_HEAD_DIM)),
                  pl.BlockSpec((r, 4 * HGRN_W), lambda i: (i, 0)), st],
        out_specs=[pl.BlockSpec((r, HGRN_W), lambda i: (i, 0)), st],
        out_shape=[jax.ShapeDtypeStruct((DEC_BATCH * DEC_SEQ, HGRN_W), BF16),
                   jax.ShapeDtypeStruct(s0.shape, F32)],
        compiler_params=_params("parallel"),
        name="hgrn_sample",
    )(lbl, on, hg, s0)


FFN_CHUNK = D_FF // 2


def _post_body(n_mix, *refs):
    mix_refs = refs[:n_mix]
    (h_ref, p_ref, wo_ref, g_mix, g_pre, wg_ref, wu_ref, wd_ref, g_post, wpp_ref, wpg_ref, g_ple, o_ref) = refs[n_mix:]
    y = None
    r0 = 0
    for m_ref in mix_refs:
        w = m_ref.shape[-1]
        part = _dot(m_ref[...].astype(BF16), wo_ref[r0:r0 + w, :])
        y = part if y is None else y + part
        r0 += w
    h = h_ref[...] + _rms(y, g_mix[...])
    u = _rms(h, g_pre[...]).astype(BF16)
    ff = None
    for c in range(0, D_FF, FFN_CHUNK):
        gate = _dot(u, wg_ref[:, c:c + FFN_CHUNK])
        up = _dot(u, wu_ref[:, c:c + FFN_CHUNK])
        part = _dot((_silu(gate) * up).astype(BF16), wd_ref[c:c + FFN_CHUNK, :])
        ff = part if ff is None else ff + part
    h = h + _rms(ff, g_post[...])
    e = _dot(p_ref[...].astype(BF16), wpp_ref[...]) * _sigmoid(_dot(h.astype(BF16), wpg_ref[...]))
    o_ref[...] = h + _rms(e, g_ple[...])


def _post(mix, h, p, wo, g_mix, g_pre, wg, wu, wd, g_post, wpp, wpg, g_ple):
    t = h.shape[0]
    tm = TOKEN_TILE
    row = lambda n: pl.BlockSpec((tm, n), lambda i: (i, 0))
    gain = _resident((1, D_MODEL))
    return pl.pallas_call(
        functools.partial(_post_body, len(mix)),
        grid=(t // tm,),
        in_specs=[row(m.shape[-1]) for m in mix] + [
            row(D_MODEL), row(PLE_DIM), _resident(wo.shape), gain, gain,
            _resident(wg.shape), _resident(wu.shape), _resident(wd.shape), gain,
            _resident(wpp.shape), _resident(wpg.shape), gain],
        out_specs=row(D_MODEL),
        out_shape=jax.ShapeDtypeStruct((t, D_MODEL), F32),
        compiler_params=_params("parallel"),
        name="post",
    )(*mix, h, p, wo, g_mix, g_pre, wg, wu, wd, g_post, wpp, wpg, g_ple)


def _rope_tables(pos, invf, sgn):
    ang = pos * invf
    return jnp.cos(ang), jnp.sin(ang) * sgn


def _mla_latents(h_ref, g_pre, wdq_ref, g_q, wkv_ref, g_kv, wpe_ref, wpes_ref, cos32, sin32):
    u = _rms(h_ref[...], g_pre[...]).astype(BF16)
    cqn = _rms(_dot(u, wdq_ref[...]), g_q[...]).astype(BF16)
    ckv = _rms(_dot(u, wkv_ref[...]), g_kv[...])
    kpe = _dot(u, wpe_ref[...]) * cos32 + _dot(u, wpes_ref[...]) * sin32
    return cqn, ckv, kpe


def _mla_pre_prompt_body(h_ref, g_pre, wdq_ref, g_q, wq_ref, wqs_ref, wkv_ref, g_kv, wpe_ref, wpes_ref,
                         wuk_ref, epe_ref, wuvt_ref, ones_ref, invf_ref, sgn_ref,
                         q_ref, k_ref, vt_ref, ckv_ref, kpe_ref):
    tm = TOKEN_TILE
    i = pl.program_id(0)
    pos = ((i * tm + _iota((tm, 1), 0)) & (SEQ - 1)).astype(F32)
    cos128, sin128 = _rope_tables(pos, invf_ref[...], sgn_ref[...])
    rope = slice(MLA_NOPE_DIM, MLA_NOPE_DIM + MLA_ROPE_DIM)
    cqn, ckv, kpe = _mla_latents(h_ref, g_pre, wdq_ref, g_q, wkv_ref, g_kv, wpe_ref, wpes_ref,
                                 cos128[:, rope], sin128[:, rope])
    ckv_ref[...] = ckv
    kpe_ref[...] = kpe
    ckv_b = ckv.astype(BF16)
    kpe_pad = _dot(kpe.astype(BF16), epe_ref[...])
    vt_ref[0, 0] = (_dot_nt(wuvt_ref[...], ckv_b) + ones_ref[...]).astype(BF16)
    for h in range(MLA_HEADS):
        hs = slice(h * MLA_HEAD_PAD, (h + 1) * MLA_HEAD_PAD)
        q = _dot(cqn, wq_ref[:, hs]) * cos128 + _dot(cqn, wqs_ref[:, hs]) * sin128
        q_ref[0, h] = (q * (MLA_SCALE * LOG2_E)).astype(BF16)
        k_ref[0, h] = (_dot(ckv_b, wuk_ref[:, hs]) + kpe_pad).astype(BF16)


def _mla_pre_prompt(h, g_pre, wdq, g_q, wq, wqs, wkv, g_kv, wpe, wpes, wuk, epe, wuvt, ones_col, invf, sgn):
    tm = TOKEN_TILE
    assert tm == MLA_TK
    per_seq = SEQ // tm
    heads = pl.BlockSpec((1, MLA_HEADS, tm, MLA_HEAD_PAD), lambda i: (i // per_seq, 0, i % per_seq, 0))
    head_shape = jax.ShapeDtypeStruct((BATCH, MLA_HEADS, SEQ, MLA_HEAD_PAD), BF16)
    vt_rows = MLA_HEADS * MLA_VT_ROWS
    row = lambda n: pl.BlockSpec((tm, n), lambda i: (i, 0))
    consts = [g_pre, wdq, g_q, wq, wqs, wkv, g_kv, wpe, wpes, wuk, epe, wuvt, ones_col, invf, sgn]
    return pl.pallas_call(
        _mla_pre_prompt_body,
        grid=(h.shape[0] // tm,),
        in_specs=[row(D_MODEL)] + [_resident(c.shape) for c in consts],
        out_specs=[heads, heads,
                   pl.BlockSpec((1, 1, vt_rows, tm), lambda i: (i // per_seq, i % per_seq, 0, 0)),
                   row(MLA_KV_RANK), row(MLA_ROPE_DIM)],
        out_shape=[head_shape, head_shape,
                   jax.ShapeDtypeStruct((BATCH, per_seq, vt_rows, tm), BF16),
                   jax.ShapeDtypeStruct((h.shape[0], MLA_KV_RANK), F32),
                   jax.ShapeDtypeStruct((h.shape[0], MLA_ROPE_DIM), F32)],
        compiler_params=_params("parallel"),
        name="mla_pre_prompt",
    )(h, *consts)


def _mla_prompt_attn_body(q_ref, k_ref, vt_ref, o_ref, m_ref, acc_ref):
    i = pl.program_id(2)
    m_ref[...] = jnp.full_like(m_ref, NEG)
    acc_ref[...] = jnp.zeros_like(acc_ref)
    causal = _iota((MLA_TK, MLA_TQ), 0) <= _iota((MLA_TK, MLA_TQ), 1)

    def kv_tile(j, masked):
        k0 = pl.multiple_of(j * MLA_TK, MLA_TK)

        def scores(hh):
            s = _dot_nt(k_ref[0, hh, pl.ds(k0, MLA_TK), :], q_ref[0, hh])
            return jnp.where(causal, s, NEG) if masked else s

        def softmax(hh, s):
            m_prev = m_ref[hh]
            m_new = jnp.maximum(m_prev, jnp.max(s, axis=0, keepdims=True))
            m_ref[hh] = m_new
            return jnp.exp2(s - m_new).astype(BF16), jnp.exp2(m_prev - m_new)

        def values(hh, p, alpha):
            vt = vt_ref[0, j, hh * MLA_VT_ROWS:(hh + 1) * MLA_VT_ROWS, :]
            acc_ref[hh] = alpha * acc_ref[hh] + _dot(vt, p)

        s_of, p_of = {}, {}
        for step in range(MLA_HEAD_GROUP + 2):
            if step < MLA_HEAD_GROUP:
                s_of[step] = scores(step)
            if 0 <= step - 1 < MLA_HEAD_GROUP:
                p_of[step - 1] = softmax(step - 1, s_of.pop(step - 1))
            if 0 <= step - 2 < MLA_HEAD_GROUP:
                values(step - 2, *p_of.pop(step - 2))

    def below_diagonal(j, carry):
        kv_tile(j, False)
        return carry

    lax.fori_loop(0, i, below_diagonal, 0)
    kv_tile(i, True)
    for pair in range(MLA_HEAD_GROUP // 2):
        heads = []
        for hh in (2 * pair, 2 * pair + 1):
            acc = acc_ref[hh]
            heads.append(acc[:MLA_V_DIM] / acc[MLA_V_DIM:MLA_V_DIM + 1])
        o_t = jnp.concatenate(heads, axis=0)
        o_ref[0, :, pair * 2 * MLA_V_DIM:(pair + 1) * 2 * MLA_V_DIM] = o_t.T.astype(BF16)


def _mla_prompt_attn(q, k, vt):
    hg = MLA_HEAD_GROUP
    assert MLA_TQ == MLA_TK and hg % 2 == 0
    return pl.pallas_call(
        _mla_prompt_attn_body,
        grid=(BATCH, MLA_HEADS // hg, SEQ // MLA_TQ),
        in_specs=[pl.BlockSpec((1, hg, MLA_TQ, MLA_HEAD_PAD), lambda b, g, i: (b, g, i, 0)),
                  pl.BlockSpec((1, hg, SEQ, MLA_HEAD_PAD), lambda b, g, i: (b, g, 0, 0),
                               pipeline_mode=pl.Buffered(1)),
                  pl.BlockSpec((1, SEQ // MLA_TK, hg * MLA_VT_ROWS, MLA_TK), lambda b, g, i: (b, 0, g, 0),
                               pipeline_mode=pl.Buffered(1))],
        out_specs=pl.BlockSpec((1, MLA_TQ, hg * MLA_V_DIM), lambda b, g, i: (b, i, g)),
        out_shape=jax.ShapeDtypeStruct((BATCH, SEQ, MLA_HEADS * MLA_V_DIM), BF16),
        scratch_shapes=[pltpu.VMEM((hg, 1, MLA_TQ), F32), pltpu.VMEM((hg, MLA_VT_ROWS, MLA_TQ), F32)],
        compiler_params=_params("parallel", "parallel", "arbitrary"),
        name="mla_prompt_attn",
    )(q, k, vt)


def _mla_pre_sample_body(h_ref, g_pre, wdq_ref, g_q, wn_ref, wr_ref, wrs_ref, wukt_ref, wkv_ref, g_kv,
                         wpe_ref, wpes_ref, invf_ref, sgn_ref,
                         ql_ref, qp_ref, ckv_ref, kpe_ref):
    tm = TOKEN_TILE
    pos = (PAST_LEN + (_iota((tm, 1), 0) & (DEC_SEQ - 1))).astype(F32)
    cos32, sin32 = _rope_tables(pos, invf_ref[...], sgn_ref[...])
    cqn, ckv, kpe = _mla_latents(h_ref, g_pre, wdq_ref, g_q, wkv_ref, g_kv, wpe_ref, wpes_ref, cos32, sin32)
    ckv_ref[...] = ckv
    kpe_ref[...] = kpe
    n_seq = tm // DEC_SEQ
    for h in range(MLA_HEADS):
        rows = slice(h * DEC_SEQ, (h + 1) * DEC_SEQ)
        q_nope = _dot(cqn, wn_ref[h]).astype(BF16)
        q_lat = _dot(q_nope, wukt_ref[h]) * (MLA_SCALE * LOG2_E)
        ql_ref[:, rows, :] = q_lat.reshape(n_seq, DEC_SEQ, MLA_KV_RANK)
        q_pe = (_dot(cqn, wr_ref[h]) * cos32 + _dot(cqn, wrs_ref[h]) * sin32) * (MLA_SCALE * LOG2_E)
        qp_ref[:, rows, :] = q_pe.reshape(n_seq, DEC_SEQ, MLA_ROPE_DIM)


def _mla_pre_sample(h, g_pre, wdq, g_q, wn, wr, wrs, wukt, wkv, g_kv, wpe, wpes, invf, sgn):
    tm = TOKEN_TILE
    n_seq = tm // DEC_SEQ
    rows = MLA_HEADS * DEC_SEQ
    row = lambda n: pl.BlockSpec((tm, n), lambda i: (i, 0))
    consts = [g_pre, wdq, g_q, wn, wr, wrs, wukt, wkv, g_kv, wpe, wpes, invf, sgn]
    return pl.pallas_call(
        _mla_pre_sample_body,
        grid=(h.shape[0] // tm,),
        in_specs=[row(D_MODEL)] + [_resident(c.shape) for c in consts],
        out_specs=[pl.BlockSpec((n_seq, rows, MLA_KV_RANK), lambda i: (i, 0, 0)),
                   pl.BlockSpec((n_seq, rows, MLA_ROPE_DIM), lambda i: (i, 0, 0)),
                   row(MLA_KV_RANK), row(MLA_ROPE_DIM)],
        out_shape=[jax.ShapeDtypeStruct((DEC_BATCH, rows, MLA_KV_RANK), F32),
                   jax.ShapeDtypeStruct((DEC_BATCH, rows, MLA_ROPE_DIM), F32),
                   jax.ShapeDtypeStruct((h.shape[0], MLA_KV_RANK), F32),
                   jax.ShapeDtypeStruct((h.shape[0], MLA_ROPE_DIM), F32)],
        compiler_params=_params("parallel"),
        name="mla_pre_sample",
    )(h, *consts)


def _mla_sample_attn_body(pt_ref, ql_ref, qp_ref, cn_ref, pn_ref, ckv_hbm, kpe_hbm, o_ref,
                          m_ref, l_ref, acc_ref, cnp_ref, pnp_ref, ck_buf, kp_buf, ck_sem, kp_sem):
    npg = PAGES_PER_GROUP
    groups_per_seq = N_PAGES // npg
    total_groups = DEC_BATCH * groups_per_seq
    b = pl.program_id(0)
    rows = MLA_HEADS * DEC_SEQ

    def page_copies(page_of, slot):
        copies = []
        for j in range(npg):
            page = page_of(j)
            copies.append(pltpu.make_async_copy(
                ckv_hbm.at[0, page], ck_buf.at[slot, pl.ds(j * PAGE_SIZE, PAGE_SIZE), :], ck_sem.at[slot]))
            copies.append(pltpu.make_async_copy(kpe_hbm.at[0, page], kp_buf.at[slot, j], kp_sem.at[slot]))
        return copies

    def start_group(t, slot):
        for c in page_copies(lambda j: pt_ref[t * npg + j], slot):
            c.start()

    def wait_group(slot):
        for c in page_copies(lambda j: 0, slot):
            c.wait()

    @pl.when(b == 0)
    def _():
        for t0 in range(PAGE_RING):
            start_group(t0, t0)

    m_ref[...] = jnp.full_like(m_ref, NEG)
    l_ref[...] = jnp.zeros_like(l_ref)
    acc_ref[...] = jnp.zeros_like(acc_ref)
    ql = ql_ref[...].astype(BF16)
    qp = qp_ref[...].astype(BF16)

    def update(s, keys):
        m_prev = m_ref[...]
        m_new = jnp.maximum(m_prev, jnp.max(s, axis=-1, keepdims=True))
        alpha = jnp.exp2(m_prev - m_new)
        p = jnp.exp2(s - m_new)
        l_ref[...] = alpha * l_ref[...] + jnp.sum(p, axis=-1, keepdims=True)
        acc_ref[...] = alpha * acc_ref[...] + _dot(p.astype(BF16), keys)
        m_ref[...] = m_new

    def group(g, carry):
        t = b * groups_per_seq + g
        slot = t % PAGE_RING
        wait_group(slot)
        def scores(c):
            pages = range(c * PAGES_PER_CHUNK, (c + 1) * PAGES_PER_CHUNK)
            kl = ck_buf[slot, pages[0] * PAGE_SIZE:(pages[-1] + 1) * PAGE_SIZE, :].astype(BF16)
            kpt = jnp.concatenate([kp_buf[slot, j].astype(BF16) for j in pages], axis=1)
            return _dot_nt(ql, kl) + _dot(qp, kpt), kl

        n_chunks = npg // PAGES_PER_CHUNK
        nxt = scores(0)
        for c in range(n_chunks):
            cur = nxt
            if c + 1 < n_chunks:
                nxt = scores(c + 1)
            update(*cur)

        @pl.when(t + PAGE_RING < total_groups)
        def _():
            start_group(t + PAGE_RING, slot)

        return carry

    lax.fori_loop(0, groups_per_seq, group, 0)

    cnp_ref[...] = jnp.zeros_like(cnp_ref)
    pnp_ref[...] = jnp.zeros_like(pnp_ref)
    cnp_ref[:DEC_SEQ, :] = cn_ref[...]
    pnp_ref[:DEC_SEQ, :] = pn_ref[...]
    cn = cnp_ref[...].astype(BF16)
    t_q = _iota((rows, PAGE_SIZE), 0) & (DEC_SEQ - 1)
    s_new = jnp.where(_iota((rows, PAGE_SIZE), 1) <= t_q,
                      _dot_nt(ql, cn) + _dot_nt(qp, pnp_ref[...].astype(BF16)), NEG)
    update(s_new, cn)
    o_ref[...] = (acc_ref[...] / l_ref[...]).astype(BF16)


def _mla_sample_attn(page_table, ql, qp, ckv_new, kpe_new, ckv_pool, kpe_pool):
    npg = PAGES_PER_GROUP
    rows = MLA_HEADS * DEC_SEQ
    assert N_PAGES % npg == 0 and npg % PAGES_PER_CHUNK == 0
    seq = lambda r, w: pl.BlockSpec((None, r, w), lambda b, pt: (b, 0, 0))
    hbm = pl.BlockSpec(memory_space=pl.ANY)
    grid_spec = pltpu.PrefetchScalarGridSpec(
        num_scalar_prefetch=1,
        grid=(DEC_BATCH,),
        in_specs=[seq(rows, MLA_KV_RANK), seq(rows, MLA_ROPE_DIM), seq(DEC_SEQ, MLA_KV_RANK),
                  seq(DEC_SEQ, MLA_ROPE_DIM), hbm, hbm],
        out_specs=seq(rows, MLA_KV_RANK),
        scratch_shapes=[pltpu.VMEM((rows, 1), F32), pltpu.VMEM((rows, 1), F32), pltpu.VMEM((rows, MLA_KV_RANK), F32),
                        pltpu.VMEM((PAGE_SIZE, MLA_KV_RANK), F32), pltpu.VMEM((PAGE_SIZE, MLA_ROPE_DIM), F32),
                        pltpu.VMEM((PAGE_RING, npg * PAGE_SIZE, MLA_KV_RANK), F32),
                        pltpu.VMEM((PAGE_RING, npg, MLA_ROPE_DIM, PAGE_SIZE), F32),
                        pltpu.SemaphoreType.DMA((PAGE_RING,)), pltpu.SemaphoreType.DMA((PAGE_RING,))],
    )
    return pl.pallas_call(
        _mla_sample_attn_body,
        grid_spec=grid_spec,
        out_shape=jax.ShapeDtypeStruct((DEC_BATCH, rows, MLA_KV_RANK), BF16),
        compiler_params=_params("arbitrary"),
        name="mla_sample_attn",
    )(page_table, ql, qp, ckv_new, kpe_new, ckv_pool, kpe_pool)


def _mla_sample_out_body(x_ref, w_ref, o_ref):
    x = x_ref[...].reshape(DEC_BATCH * 2 * DEC_SEQ, MLA_KV_RANK)
    y = _dot(x, w_ref[0]).reshape(DEC_BATCH, 2 * DEC_SEQ, 2 * MLA_V_DIM)
    first_head = _iota((DEC_BATCH, DEC_SEQ, 2 * MLA_V_DIM), 2) < MLA_V_DIM
    o = jnp.where(first_head, y[:, :DEC_SEQ], y[:, DEC_SEQ:])
    o_ref[...] = o.reshape(DEC_BATCH * DEC_SEQ, 2 * MLA_V_DIM)


def _mla_sample_out(o_lat, wuv_pairs):
    return pl.pallas_call(
        _mla_sample_out_body,
        grid=(MLA_HEADS // 2,),
        in_specs=[pl.BlockSpec((DEC_BATCH, 2 * DEC_SEQ, MLA_KV_RANK), lambda j: (0, j, 0)),
                  pl.BlockSpec((1, MLA_KV_RANK, 2 * MLA_V_DIM), lambda j: (j, 0, 0))],
        out_specs=pl.BlockSpec((DEC_BATCH * DEC_SEQ, 2 * MLA_V_DIM), lambda j: (0, j)),
        out_shape=jax.ShapeDtypeStruct((DEC_BATCH * DEC_SEQ, MLA_HEADS * MLA_V_DIM), F32),
        compiler_params=_params("parallel"),
        name="mla_sample_out",
    )(o_lat, wuv_pairs)


def _swap_halves(w):
    half = w.shape[-1] // 2
    return jnp.concatenate([w[..., half:], w[..., :half]], axis=-1)


def _rope_consts():
    half = MLA_ROPE_DIM // 2
    inv_freq = ROPE_THETA ** (-jnp.arange(0, half, dtype=F32) * 2.0 / MLA_ROPE_DIM)
    invf32 = jnp.concatenate([inv_freq, inv_freq])[None]
    sgn32 = jnp.concatenate([-jnp.ones((half,), F32), jnp.ones((half,), F32)])[None]
    pad = lambda a: jnp.pad(a, ((0, 0), (MLA_NOPE_DIM, MLA_HEAD_PAD - MLA_NOPE_DIM - MLA_ROPE_DIM)))
    return invf32, sgn32, pad(invf32), pad(sgn32)


def _mla_weights(w_uq, w_dkv, w_uk, w_uv):
    wq = w_uq.reshape(MLA_Q_RANK, MLA_HEADS, MLA_NOPE_DIM + MLA_ROPE_DIM)
    wq_nope, wq_rope = wq[..., :MLA_NOPE_DIM], wq[..., MLA_NOPE_DIM:]
    tail = MLA_HEAD_PAD - MLA_NOPE_DIM - MLA_ROPE_DIM
    pad_last = lambda a, lo, hi: jnp.pad(a, [(0, 0)] * (a.ndim - 1) + [(lo, hi)])
    flat = lambda a: a.reshape(a.shape[0], MLA_HEADS * MLA_HEAD_PAD).astype(BF16)
    w = {}
    w["wq"] = flat(pad_last(wq, 0, tail))
    w["wqs"] = flat(pad_last(_swap_halves(wq_rope), MLA_NOPE_DIM, tail))
    w["wuk"] = flat(pad_last(w_uk, 0, MLA_HEAD_PAD - MLA_NOPE_DIM))
    wuvt = jnp.transpose(w_uv, (1, 2, 0))
    w["wuvt"] = jnp.pad(wuvt, ((0, 0), (0, MLA_VT_ROWS - MLA_V_DIM), (0, 0))).reshape(
        MLA_HEADS * MLA_VT_ROWS, MLA_KV_RANK).astype(BF16)
    w["ones_col"] = jnp.tile((jnp.arange(MLA_VT_ROWS) == MLA_V_DIM).astype(F32), MLA_HEADS)[:, None]
    w["epe"] = pad_last(jnp.eye(MLA_ROPE_DIM, dtype=F32), MLA_NOPE_DIM, tail).astype(BF16)
    w["wn"] = jnp.moveaxis(wq_nope, 1, 0).astype(BF16)
    w["wr"] = jnp.moveaxis(wq_rope, 1, 0).astype(BF16)
    w["wrs"] = jnp.moveaxis(_swap_halves(wq_rope), 1, 0).astype(BF16)
    w["wukt"] = jnp.transpose(w_uk, (1, 2, 0)).astype(BF16)
    w["wkv"] = w_dkv[:, :MLA_KV_RANK].astype(BF16)
    w["wpe"] = w_dkv[:, MLA_KV_RANK:].astype(BF16)
    w["wpes"] = _swap_halves(w_dkv[:, MLA_KV_RANK:]).astype(BF16)
    w["wuv_pairs"] = jnp.transpose(w_uv.reshape(MLA_KV_RANK, MLA_HEADS // 2, 2 * MLA_V_DIM),
                                   (1, 0, 2)).astype(BF16)
    return w


def kernel(x_prompt, x_sample, cache_swa_k, cache_swa_v, state_hgrn, cache_mla_ckv, cache_mla_kpe, page_table, p_prompt, p_sample, ln_mix_pre, ln_mix_post, ln_ffn_pre, ln_ffn_post, ln_ple, w_ab_in, w_ab_out, swa_sinks, hgrn_lb_logits, hgrn_out_norm, w_mla_dq, mla_q_norm, w_mla_uq, w_mla_dkv, mla_kv_norm, w_mla_uk, w_mla_uv, w_mla_o, w_ffn_gate, w_ffn_up, w_ffn_down, w_ple_proj, w_ple_gate):
    tp = BATCH * SEQ
    ts = DEC_BATCH * DEC_SEQ
    bf = lambda a: a.astype(BF16)
    gain = lambda a, i: a[i][None].astype(F32)

    def finish(mix, h, p, i, wo):
        return _post(mix, h, p, wo, gain(ln_mix_post, i), gain(ln_ffn_pre, i), bf(w_ffn_gate[i]), bf(w_ffn_up[i]),
                     bf(w_ffn_down[i]), gain(ln_ffn_post, i), bf(w_ple_proj[i]), bf(w_ple_gate[i]), gain(ln_ple, i))

    hp = x_prompt.reshape(tp, D_MODEL)
    hs = x_sample.reshape(ts, D_MODEL)
    pp = p_prompt.reshape(2, tp, PLE_DIM)
    ps = p_sample.reshape(2, ts, PLE_DIM)

    w_in = bf(w_ab_in[0])
    w_out = bf(w_ab_out[0])
    g0 = gain(ln_mix_pre, 0)
    sinks = swa_sinks[0].astype(F32)
    lbl = hgrn_lb_logits.astype(F32)
    on = hgrn_out_norm[0][None].astype(F32)

    qa, ka, va, hg = _ab_in(hp, g0, w_in, BF16)
    oa = _swa_prompt(sinks, qa.reshape(BATCH, SEQ, SWA_Q_W), ka.reshape(BATCH, SEQ, SWA_KV_W),
                     va.reshape(BATCH, SEQ, SWA_KV_W))
    ob, hgrn_p = _hgrn_prompt(lbl, on, hg.reshape(BATCH, SEQ, 4 * HGRN_W))
    swa_k_p = ka.reshape(BATCH, SEQ, SWA_KV_HEADS, SWA_HEAD_DIM)[:, -WINDOW:]
    swa_v_p = va.reshape(BATCH, SEQ, SWA_KV_HEADS, SWA_HEAD_DIM)[:, -WINDOW:]
    hp = finish([oa.reshape(tp, SWA_Q_W), ob.reshape(tp, HGRN_W)], hp, pp[0], 0, w_out)

    qa, ka, va, hg = _ab_in(hs, g0, w_in, F32)
    oa, swa_k_s, swa_v_s = _swa_sample(
        sinks, qa, ka.reshape(DEC_BATCH, DEC_SEQ, SWA_KV_W),
        va.reshape(DEC_BATCH, DEC_SEQ, SWA_KV_W), cache_swa_k[0].reshape(DEC_BATCH, WINDOW, SWA_KV_W),
        cache_swa_v[0].reshape(DEC_BATCH, WINDOW, SWA_KV_W))
    ob, hgrn_s = _hgrn_sample(lbl, on, hg, state_hgrn[0])
    hs = finish([oa.reshape(ts, SWA_Q_W), ob], hs, ps[0], 0, w_out)

    mw = _mla_weights(w_mla_uq[0], w_mla_dkv[0], w_mla_uk[0], w_mla_uv[0])
    invf32, sgn32, invf128, sgn128 = _rope_consts()
    g1 = gain(ln_mix_pre, 1)
    wdq = bf(w_mla_dq[0])
    g_q = mla_q_norm[0][None].astype(F32)
    g_kv = mla_kv_norm[0][None].astype(F32)
    w_o = bf(w_mla_o[0])

    q, k, vt, ckv_p, kpe_p = _mla_pre_prompt(hp, g1, wdq, g_q, mw["wq"], mw["wqs"], mw["wkv"], g_kv, mw["wpe"],
                                             mw["wpes"], mw["wuk"], mw["epe"], mw["wuvt"], mw["ones_col"],
                                             invf128, sgn128)
    o = _mla_prompt_attn(q, k, vt)
    hp = finish([o.reshape(tp, MLA_HEADS * MLA_V_DIM)], hp, pp[1], 1, w_o)

    ql, qp, ckv_s, kpe_s = _mla_pre_sample(hs, g1, wdq, g_q, mw["wn"], mw["wr"], mw["wrs"], mw["wukt"], mw["wkv"],
                                           g_kv, mw["wpe"], mw["wpes"], invf32, sgn32)
    o_lat = _mla_sample_attn(page_table.reshape(-1), ql, qp, ckv_s.reshape(DEC_BATCH, DEC_SEQ, MLA_KV_RANK),
                             kpe_s.reshape(DEC_BATCH, DEC_SEQ, MLA_ROPE_DIM), cache_mla_ckv,
                             jnp.swapaxes(cache_mla_kpe, 2, 3))
    o = _mla_sample_out(o_lat, mw["wuv_pairs"])
    hs = finish([o], hs, ps[1], 1, w_o)

    kv5 = lambda a, n: a.reshape(1, n, WINDOW, SWA_KV_HEADS, SWA_HEAD_DIM)
    return (hp.reshape(BATCH, SEQ, D_MODEL), hs.reshape(DEC_BATCH, DEC_SEQ, D_MODEL),
            swa_k_p[None], swa_v_p[None], hgrn_p[None],
            ckv_p.reshape(1, BATCH, SEQ, MLA_KV_RANK), kpe_p.reshape(1, BATCH, SEQ, MLA_ROPE_DIM),
            kv5(swa_k_s, DEC_BATCH), kv5(swa_v_s, DEC_BATCH), hgrn_s[None],
            ckv_s.reshape(1, DEC_BATCH, DEC_SEQ, MLA_KV_RANK), kpe_s.reshape(1, DEC_BATCH, DEC_SEQ, MLA_ROPE_DIM))
```

```python
import functools

import jax
import jax.numpy as jnp
from jax import lax
from jax.experimental import pallas as pl
from jax.experimental.pallas import tpu as pltpu

F32 = jnp.float32
BF16 = jnp.bfloat16

D_MODEL = 1024
BATCH = 2
SEQ = 8192
DEC_BATCH = 128
DEC_SEQ = 8
PAST_LEN = 16384
PAGE_SIZE = 128
N_PAGES = PAST_LEN // PAGE_SIZE

SWA_HEADS = 8
SWA_KV_HEADS = 2
SWA_GROUP = SWA_HEADS // SWA_KV_HEADS
SWA_HEAD_DIM = 64
WINDOW = 128
SWA_SCALE = SWA_HEAD_DIM ** -0.5
SWA_Q_W = SWA_HEADS * SWA_HEAD_DIM
SWA_KV_W = SWA_KV_HEADS * SWA_HEAD_DIM

HGRN_HEADS = 4
HGRN_HEAD_DIM = 128
HGRN_CHUNK = 32
HGRN_W = HGRN_HEADS * HGRN_HEAD_DIM
AB_IN_W = SWA_Q_W + 2 * SWA_KV_W + 4 * HGRN_W

MLA_HEADS = 16
MLA_NOPE_DIM = 64
MLA_ROPE_DIM = 32
MLA_V_DIM = 64
MLA_Q_RANK = 512
MLA_KV_RANK = 256
MLA_SCALE = (MLA_NOPE_DIM + MLA_ROPE_DIM) ** -0.5
ROPE_THETA = 10000.0
MLA_HEAD_PAD = 128

D_FF = 2816
PLE_DIM = 256
NORM_EPS = 1e-6
NEG = -1e30
LOG2_E = 1.4426950408889634

VMEM_LIMIT_BYTES = 56 * 1024 * 1024

TOKEN_TILE = 512
HGRN_TILE = 256
SWA_TQ = 256
SAMPLE_BATCH_TILE = 16
MLA_TQ = 512
MLA_TK = 512
MLA_HEAD_GROUP = 8
MLA_VT_ROWS = 80
PAGES_PER_GROUP = 32
PAGES_PER_CHUNK = 16
PAGE_RING = 3


def _dot(a, b):
    return jnp.dot(a, b, preferred_element_type=F32)


def _dot_nt(a, b):
    return lax.dot_general(a, b, (((1,), (1,)), ((), ())), preferred_element_type=F32)


def _rms(x, g):
    return x * lax.rsqrt(jnp.mean(x * x, axis=-1, keepdims=True) + NORM_EPS) * g


def _sigmoid(x):
    return 1.0 / (1.0 + jnp.exp(-x))


def _silu(x):
    return x * _sigmoid(x)


def _iota(shape, dim):
    return lax.broadcasted_iota(jnp.int32, shape, dim)


def _params(*sem):
    return pltpu.CompilerParams(dimension_semantics=sem, vmem_limit_bytes=VMEM_LIMIT_BYTES)


def _resident(shape):
    nd = len(shape)
    return pl.BlockSpec(shape, lambda *_: (0,) * nd, pipeline_mode=pl.Buffered(1))


def _smem():
    return pl.BlockSpec(memory_space=pltpu.SMEM)


def _ab_in_body(h_ref, g_ref, w_ref, qa_ref, k_ref, v_ref, hg_ref):
    u = _rms(h_ref[...], g_ref[...]).astype(BF16)
    qa_ref[...] = (_dot(u, w_ref[:, :SWA_Q_W]) * SWA_SCALE).astype(qa_ref.dtype)
    k_ref[...] = _dot(u, w_ref[:, SWA_Q_W:SWA_Q_W + SWA_KV_W])
    v_ref[...] = _dot(u, w_ref[:, SWA_Q_W + SWA_KV_W:SWA_Q_W + 2 * SWA_KV_W])
    hg_ref[...] = _dot(u, w_ref[:, SWA_Q_W + 2 * SWA_KV_W:])


def _ab_in(h, g, w, q_dtype):
    t = h.shape[0]
    tm = TOKEN_TILE
    row = lambda n: pl.BlockSpec((tm, n), lambda i: (i, 0))
    return pl.pallas_call(
        _ab_in_body,
        grid=(t // tm,),
        in_specs=[row(D_MODEL), _resident((1, D_MODEL)), _resident((D_MODEL, AB_IN_W))],
        out_specs=[row(SWA_Q_W), row(SWA_KV_W), row(SWA_KV_W), row(4 * HGRN_W)],
        out_shape=[jax.ShapeDtypeStruct((t, SWA_Q_W), q_dtype),
                   jax.ShapeDtypeStruct((t, SWA_KV_W), F32),
                   jax.ShapeDtypeStruct((t, SWA_KV_W), F32),
                   jax.ShapeDtypeStruct((t, 4 * HGRN_W), F32)],
        compiler_params=_params("parallel"),
        name="ab_in",
    )(h, g, w)


def _swa_head(q, keys, vals, dists, valids, slope, sink):
    scores = []
    for k, dist, valid in zip(keys, dists, valids):
        s = _dot_nt(q, k) - slope * dist
        scores.append(jnp.where(valid, s, NEG))
    m = sink
    for s in scores:
        m = jnp.maximum(m, jnp.max(s, axis=-1, keepdims=True))
    den = jnp.exp(sink - m)
    acc = None
    for s, v in zip(scores, vals):
        p = jnp.exp(s - m)
        den = den + jnp.sum(p, axis=-1, keepdims=True)
        pv = _dot(p.astype(BF16), v)
        acc = pv if acc is None else acc + pv
    return acc / den


def _swa_prompt_body(sink_ref, q_ref, kc_ref, kp_ref, vc_ref, vp_ref, o_ref):
    i = pl.program_id(1)
    nk = WINDOW + SWA_TQ
    kk = jnp.concatenate([kp_ref[0], kc_ref[0]], axis=0).astype(BF16)
    vv_t = jnp.concatenate([vp_ref[0], vc_ref[0]], axis=0).T.astype(BF16)
    key = _iota((nk, SWA_TQ), 0)
    qry = _iota((nk, SWA_TQ), 1)
    dist_i = qry + WINDOW - key
    valid = (dist_i >= 0) & (dist_i <= WINDOW) & (key >= jnp.where(i > 0, 0, WINDOW))
    dist = dist_i.astype(F32)
    q = q_ref[0]
    kv_lanes = lambda j: slice(j // SWA_GROUP * SWA_HEAD_DIM, (j // SWA_GROUP + 1) * SWA_HEAD_DIM)

    def scores(j):
        s = _dot_nt(kk[:, kv_lanes(j)], q[:, j * SWA_HEAD_DIM:(j + 1) * SWA_HEAD_DIM]) - 2.0 ** -(j + 1) * dist
        return jnp.where(valid, s, NEG)

    def attend(j, s):
        m = jnp.maximum(jnp.max(s, axis=0, keepdims=True), sink_ref[j])
        p = jnp.exp(s - m)
        den = jnp.sum(p, axis=0, keepdims=True) + jnp.exp(sink_ref[j] - m)
        return _dot(vv_t[kv_lanes(j), :], p.astype(BF16)) / den

    heads = []
    nxt = scores(0)
    for j in range(SWA_HEADS):
        cur = nxt
        if j + 1 < SWA_HEADS:
            nxt = scores(j + 1)
        heads.append(attend(j, cur))
    for pair in range(SWA_HEADS // 2):
        o_t = jnp.concatenate(heads[2 * pair:2 * pair + 2], axis=0)
        o_ref[0, :, pair * 2 * SWA_HEAD_DIM:(pair + 1) * 2 * SWA_HEAD_DIM] = o_t.T.astype(BF16)


def _swa_prompt(sinks, qa, k, v):
    per_tile = SWA_TQ // WINDOW
    cur = lambda w: pl.BlockSpec((1, SWA_TQ, w), lambda b, i: (b, i, 0))
    prev = lambda w: pl.BlockSpec((1, WINDOW, w), lambda b, i: (b, jnp.maximum(i * per_tile - 1, 0), 0))
    return pl.pallas_call(
        _swa_prompt_body,
        grid=(BATCH, SEQ // SWA_TQ),
        in_specs=[_smem(), cur(SWA_Q_W), cur(SWA_KV_W), prev(SWA_KV_W), cur(SWA_KV_W), prev(SWA_KV_W)],
        out_specs=cur(SWA_Q_W),
        out_shape=jax.ShapeDtypeStruct((BATCH, SEQ, SWA_Q_W), BF16),
        compiler_params=_params("parallel", "arbitrary"),
        name="swa_prompt",
    )(sinks, qa, k, k, v, v)


def _swa_sample_body(sink_ref, q_ref, kn_ref, vn_ref, kb_ref, vb_ref, o_ref, ko_ref, vo_ref):
    bb = SAMPLE_BATCH_TILE
    t_new = DEC_SEQ
    r = bb * t_new
    nk = bb * WINDOW
    kb3, vb3, kn3, vn3 = kb_ref[...], vb_ref[...], kn_ref[...], vn_ref[...]
    ko_ref[:, :WINDOW - t_new, :] = kb3[:, t_new:, :]
    ko_ref[:, WINDOW - t_new:, :] = kn3
    vo_ref[:, :WINDOW - t_new, :] = vb3[:, t_new:, :]
    vo_ref[:, WINDOW - t_new:, :] = vn3
    kb = kb3.reshape(nk, SWA_KV_W).astype(BF16)
    vb = vb3.reshape(nk, SWA_KV_W).astype(BF16)
    kn = kn3.reshape(r, SWA_KV_W).astype(BF16)
    vn = vn3.reshape(r, SWA_KV_W).astype(BF16)

    row = _iota((r, nk), 0)
    col = _iota((r, nk), 1)
    t_q = row & (t_new - 1)
    slot = col & (WINDOW - 1)
    dist_b = (t_q - slot + WINDOW).astype(F32)
    valid_b = ((row >> 3) == (col >> 7)) & (slot >= t_q)
    row_n = _iota((r, r), 0)
    col_n = _iota((r, r), 1)
    dist_n = ((row_n & (t_new - 1)) - (col_n & (t_new - 1))).astype(F32)
    valid_n = ((row_n >> 3) == (col_n >> 3)) & (col_n <= row_n)
    q = q_ref[...].astype(BF16)
    for hk in range(SWA_KV_HEADS):
        ks = slice(hk * SWA_HEAD_DIM, (hk + 1) * SWA_HEAD_DIM)
        for g in range(SWA_GROUP):
            j = hk * SWA_GROUP + g
            qs = slice(j * SWA_HEAD_DIM, (j + 1) * SWA_HEAD_DIM)
            o_ref[:, qs] = _swa_head(q[:, qs], (kb[:, ks], kn[:, ks]), (vb[:, ks], vn[:, ks]),
                                     (dist_b, dist_n), (valid_b, valid_n), 2.0 ** -(j + 1), sink_ref[j])


def _swa_sample(sinks, qa, kn, vn, kbuf, vbuf):
    bb = SAMPLE_BATCH_TILE
    assert DEC_SEQ == 8 and WINDOW == 128
    blk = lambda r, w: pl.BlockSpec((bb, r, w), lambda i: (i, 0, 0))
    rows = pl.BlockSpec((bb * DEC_SEQ, SWA_Q_W), lambda i: (i, 0))
    return pl.pallas_call(
        _swa_sample_body,
        grid=(DEC_BATCH // bb,),
        in_specs=[_smem(), rows, blk(DEC_SEQ, SWA_KV_W), blk(DEC_SEQ, SWA_KV_W),
                  blk(WINDOW, SWA_KV_W), blk(WINDOW, SWA_KV_W)],
        out_specs=[rows, blk(WINDOW, SWA_KV_W), blk(WINDOW, SWA_KV_W)],
        out_shape=[jax.ShapeDtypeStruct((DEC_BATCH * DEC_SEQ, SWA_Q_W), F32),
                   jax.ShapeDtypeStruct((DEC_BATCH, WINDOW, SWA_KV_W), F32),
                   jax.ShapeDtypeStruct((DEC_BATCH, WINDOW, SWA_KV_W), F32)],
        compiler_params=_params("parallel"),
        name="swa_sample",
    )(sinks, qa, kn, vn, kbuf, vbuf)


def _hgrn_lower_bound(lbl_ref, layer):
    logits = lbl_ref[...]
    e = jnp.exp(logits - jnp.max(logits, axis=0, keepdims=True))
    sm = e / jnp.sum(e, axis=0, keepdims=True)
    return jnp.sum(sm[:layer + 1], axis=0, keepdims=True)


def _split3(x):
    hi = x.astype(BF16)
    r = x - hi.astype(F32)
    mid = r.astype(BF16)
    lo = (r - mid.astype(F32)).astype(BF16)
    return hi, mid, lo


def _hgrn_chunk_terms(x, lb, chunk):
    r = x.shape[0]
    qr, fr, ir, gr = (x[:, i * HGRN_W:(i + 1) * HGRN_W] for i in range(4))
    f = lb + (1.0 - lb) * _sigmoid(fr)
    logf = jnp.log(f)
    row = _iota((r, r), 0)
    col = _iota((r, r), 1)
    chunk_start = row & ~(chunk - 1)
    same = (col >= chunk_start) & (col < chunk_start + chunk)
    tril = (col >= chunk_start) & (col <= row)
    parts = _split3(logf)
    tri_m = jnp.where(tril, 1.0, 0.0).astype(BF16)
    blk_m = jnp.where(same, 1.0, 0.0).astype(BF16)
    b = sum(_dot(tri_m, p) for p in parts)
    bl = sum(_dot(blk_m, p) for p in parts)
    k = 1.0 - f
    qd = _silu(qr) * jnp.exp(b)
    kd = k * jnp.exp(-b)
    k2 = k * jnp.exp(bl - b)
    return qd, kd, k2, ir, gr, bl, tril


def _hgrn_intra(qd, kd, v, tril):
    att = jnp.where(tril, _dot_nt(qd.astype(BF16), kd.astype(BF16)), 0.0)
    return _dot(att.astype(BF16), v.astype(BF16))


def _hgrn_finish(o, g, on):
    return (_rms(o, on) * _silu(g)).astype(BF16)


def _hgrn_prompt_body(lbl_ref, on_ref, x_ref, o_ref, s_ref, st_ref):
    t = pl.program_id(1)

    @pl.when(t == 0)
    def _():
        st_ref[...] = jnp.zeros_like(st_ref)

    lb = _hgrn_lower_bound(lbl_ref, 0)
    qd, kd, k2, v, g, bl, tril = _hgrn_chunk_terms(x_ref[0], lb, HGRN_CHUNK)
    rows = _iota((HGRN_TILE, HGRN_HEAD_DIM), 0)
    n_chunks = HGRN_TILE // HGRN_CHUNK
    head = [slice(h * HGRN_HEAD_DIM, (h + 1) * HGRN_HEAD_DIM) for h in range(HGRN_HEADS)]
    o_intra = [_hgrn_intra(qd[:, hs], kd[:, hs], v[:, hs], tril) for hs in head]
    qdb = [qd[:, hs].astype(BF16) for hs in head]
    vt = [v[:, hs].T.astype(BF16) for hs in head]
    st = [st_ref[h] for h in range(HGRN_HEADS)]
    o_inter = [[] for _ in head]
    for c in range(n_chunks):
        r0 = c * HGRN_CHUNK
        in_chunk = (rows >= r0) & (rows < r0 + HGRN_CHUNK)
        for h, hs in enumerate(head):
            o_inter[h].append(_dot_nt(qdb[h][r0:r0 + HGRN_CHUNK], st[h].astype(BF16)))
            d_st = _dot(vt[h], jnp.where(in_chunk, k2[:, hs], 0.0).astype(BF16))
            st[h] = st[h] * jnp.exp(bl[r0:r0 + 1, hs]) + d_st
    for h, hs in enumerate(head):
        st_ref[h] = st[h]
        o = o_intra[h] + jnp.concatenate(o_inter[h], axis=0)
        o_ref[0, :, hs] = _hgrn_finish(o, g[:, hs], on_ref[...])

    @pl.when(t == pl.num_programs(1) - 1)
    def _():
        for h in range(HGRN_HEADS):
            s_ref[0, h] = st_ref[h].T


def _hgrn_prompt(lbl, on, hg):
    tt = HGRN_TILE
    return pl.pallas_call(
        _hgrn_prompt_body,
        grid=(BATCH, SEQ // tt),
        in_specs=[_resident(lbl.shape), _resident((1, HGRN_HEAD_DIM)),
                  pl.BlockSpec((1, tt, 4 * HGRN_W), lambda b, t: (b, t, 0))],
        out_specs=[pl.BlockSpec((1, tt, HGRN_W), lambda b, t: (b, t, 0)),
                   pl.BlockSpec((1, HGRN_HEADS, HGRN_HEAD_DIM, HGRN_HEAD_DIM), lambda b, t: (b, 0, 0, 0))],
        out_shape=[jax.ShapeDtypeStruct((BATCH, SEQ, HGRN_W), BF16),
                   jax.ShapeDtypeStruct((BATCH, HGRN_HEADS, HGRN_HEAD_DIM, HGRN_HEAD_DIM), F32)],
        scratch_shapes=[pltpu.VMEM((HGRN_HEADS, HGRN_HEAD_DIM, HGRN_HEAD_DIM), F32)],
        compiler_params=_params("parallel", "arbitrary"),
        name="hgrn_prompt",
    )(lbl, on, hg)


def _hgrn_sample_body(lbl_ref, on_ref, x_ref, s0_ref, o_ref, s_ref):
    bb = SAMPLE_BATCH_TILE
    r = bb * DEC_SEQ
    lb = _hgrn_lower_bound(lbl_ref, 0)
    qd, kd, k2, v, g, bl, tril = _hgrn_chunk_terms(x_ref[...], lb, DEC_SEQ)
    cols = _iota((HGRN_HEAD_DIM, r), 1)
    for h in range(HGRN_HEADS):
        hs = slice(h * HGRN_HEAD_DIM, (h + 1) * HGRN_HEAD_DIM)
        o_intra = _hgrn_intra(qd[:, hs], kd[:, hs], v[:, hs], tril)
        qdb = qd[:, hs].astype(BF16)
        vb = v[:, hs].astype(BF16)
        k2t = k2[:, hs].T
        decay_t = jnp.exp(bl[:, hs].T)
        o_inter = []
        for b in range(bb):
            r0 = b * DEC_SEQ
            s0 = s0_ref[b, h]
            o_inter.append(_dot(qdb[r0:r0 + DEC_SEQ], s0.astype(BF16)))
            in_seq = (cols >= r0) & (cols < r0 + DEC_SEQ)
            d_s = _dot(jnp.where(in_seq, k2t, 0.0).astype(BF16), vb)
            s_ref[b, h] = s0 * decay_t[:, r0:r0 + 1] + d_s
        o = o_intra + jnp.concatenate(o_inter, axis=0)
        o_ref[:, hs] = _hgrn_finish(o, g[:, hs], on_ref[...])


def _hgrn_sample(lbl, on, hg, s0):
    bb = SAMPLE_BATCH_TILE
    r = bb * DEC_SEQ
    st = pl.BlockSpec((bb, HGRN_HEADS, HGRN_HEAD_DIM, HGRN_HEAD_DIM), lambda i: (i, 0, 0, 0))
    return pl.pallas_call(
        _hgrn_sample_body,
        grid=(DEC_BATCH // bb,),
        in_specs=[_resident(lbl.shape), _resident((1, HGRN_HEAD_DIM)),
                  pl.BlockSpec((r, 4 * HGRN_W), lambda i: (i, 0)), st],
        out_specs=[pl.BlockSpec((r, HGRN_W), lambda i: (i, 0)), st],
        out_shape=[jax.ShapeDtypeStruct((DEC_BATCH * DEC_SEQ, HGRN_W), BF16),
                   jax.ShapeDtypeStruct(s0.shape, F32)],
        compiler_params=_params("parallel"),
        name="hgrn_sample",
    )(lbl, on, hg, s0)


FFN_CHUNK = D_FF // 2


def _post_body(n_mix, *refs):
    mix_refs = refs[:n_mix]
    (h_ref, p_ref, wo_ref, g_mix, g_pre, wg_ref, wu_ref, wd_ref, g_post, wpp_ref, wpg_ref, g_ple, o_ref) = refs[n_mix:]
    y = None
    r0 = 0
    for m_ref in mix_refs:
        w = m_ref.shape[-1]
        part = _dot(m_ref[...].astype(BF16), wo_ref[r0:r0 + w, :])
        y = part if y is None else y + part
        r0 += w
    h = h_ref[...] + _rms(y, g_mix[...])
    u = _rms(h, g_pre[...]).astype(BF16)
    ff = None
    for c in range(0, D_FF, FFN_CHUNK):
        gate = _dot(u, wg_ref[:, c:c + FFN_CHUNK])
        up = _dot(u, wu_ref[:, c:c + FFN_CHUNK])
        part = _dot((_silu(gate) * up).astype(BF16), wd_ref[c:c + FFN_CHUNK, :])
        ff = part if ff is None else ff + part
    h = h + _rms(ff, g_post[...])
    e = _dot(p_ref[...].astype(BF16), wpp_ref[...]) * _sigmoid(_dot(h.astype(BF16), wpg_ref[...]))
    o_ref[...] = h + _rms(e, g_ple[...])


def _post(mix, h, p, wo, g_mix, g_pre, wg, wu, wd, g_post, wpp, wpg, g_ple):
    t = h.shape[0]
    tm = TOKEN_TILE
    row = lambda n: pl.BlockSpec((tm, n), lambda i: (i, 0))
    gain = _resident((1, D_MODEL))
    return pl.pallas_call(
        functools.partial(_post_body, len(mix)),
        grid=(t // tm,),
        in_specs=[row(m.shape[-1]) for m in mix] + [
            row(D_MODEL), row(PLE_DIM), _resident(wo.shape), gain, gain,
            _resident(wg.shape), _resident(wu.shape), _resident(wd.shape), gain,
            _resident(wpp.shape), _resident(wpg.shape), gain],
        out_specs=row(D_MODEL),
        out_shape=jax.ShapeDtypeStruct((t, D_MODEL), F32),
        compiler_params=_params("parallel"),
        name="post",
    )(*mix, h, p, wo, g_mix, g_pre, wg, wu, wd, g_post, wpp, wpg, g_ple)


def _rope_tables(pos, invf, sgn):
    ang = pos * invf
    return jnp.cos(ang), jnp.sin(ang) * sgn


def _mla_latents(h_ref, g_pre, wdq_ref, g_q, wkv_ref, g_kv, wpe_ref, wpes_ref, cos32, sin32):
    u = _rms(h_ref[...], g_pre[...]).astype(BF16)
    cqn = _rms(_dot(u, wdq_ref[...]), g_q[...]).astype(BF16)
    ckv = _rms(_dot(u, wkv_ref[...]), g_kv[...])
    kpe = _dot(u, wpe_ref[...]) * cos32 + _dot(u, wpes_ref[...]) * sin32
    return cqn, ckv, kpe


def _mla_pre_prompt_body(h_ref, g_pre, wdq_ref, g_q, wq_ref, wkvp_ref, g_kv,
                         wuk_ref, wuvt_ref, ones_ref, invf_ref, sgn_ref,
                         q_ref, k_ref, vt_ref, ckv_ref, kpe_ref):
    tm = TOKEN_TILE
    i = pl.program_id(0)
    pos = ((i * tm + _iota((tm, 1), 0)) & (SEQ - 1)).astype(F32)
    cos128, sin128 = _rope_tables(pos, invf_ref[...], sgn_ref[...])
    rope = slice(MLA_NOPE_DIM, MLA_NOPE_DIM + MLA_ROPE_DIM)
    half = MLA_ROPE_DIM // 2
    first_half = _iota((tm, MLA_HEAD_PAD), 1) < MLA_NOPE_DIM + half

    def swap_rope_halves(x):
        return jnp.where(first_half, pltpu.roll(x, MLA_HEAD_PAD - half, 1), pltpu.roll(x, half, 1))

    u = _rms(h_ref[...], g_pre[...]).astype(BF16)
    cqn = _rms(_dot(u, wdq_ref[...]), g_q[...]).astype(BF16)
    kvp = _dot(u, wkvp_ref[...])
    ckv = _rms(kvp[:, :MLA_KV_RANK], g_kv[...])
    kpe_raw = kvp[:, MLA_KV_RANK:]
    kpe_pad = kpe_raw * cos128 + swap_rope_halves(kpe_raw) * sin128
    ckv_ref[...] = ckv
    kpe_ref[...] = kpe_pad[:, rope]
    ckv_b = ckv.astype(BF16)
    vt_ref[0, 0] = (_dot_nt(wuvt_ref[...], ckv_b) + ones_ref[...]).astype(BF16)
    for h in range(MLA_HEADS):
        hs = slice(h * MLA_HEAD_PAD, (h + 1) * MLA_HEAD_PAD)
        q = _dot(cqn, wq_ref[:, hs])
        q = q * cos128 + swap_rope_halves(q) * sin128
        q_ref[0, h] = (q * (MLA_SCALE * LOG2_E)).astype(BF16)
        k_ref[0, h] = (_dot(ckv_b, wuk_ref[:, hs]) + kpe_pad).astype(BF16)


def _mla_pre_prompt(h, g_pre, wdq, g_q, wq, wkvp, g_kv, wuk, wuvt, ones_col, invf, sgn):
    tm = TOKEN_TILE
    assert tm == MLA_TK
    per_seq = SEQ // tm
    heads = pl.BlockSpec((1, MLA_HEADS, tm, MLA_HEAD_PAD), lambda i: (i // per_seq, 0, i % per_seq, 0))
    head_shape = jax.ShapeDtypeStruct((BATCH, MLA_HEADS, SEQ, MLA_HEAD_PAD), BF16)
    vt_rows = MLA_HEADS * MLA_VT_ROWS
    row = lambda n: pl.BlockSpec((tm, n), lambda i: (i, 0))
    consts = [g_pre, wdq, g_q, wq, wkvp, g_kv, wuk, wuvt, ones_col, invf, sgn]
    return pl.pallas_call(
        _mla_pre_prompt_body,
        grid=(h.shape[0] // tm,),
        in_specs=[row(D_MODEL)] + [_resident(c.shape) for c in consts],
        out_specs=[heads, heads,
                   pl.BlockSpec((1, 1, vt_rows, tm), lambda i: (i // per_seq, i % per_seq, 0, 0)),
                   row(MLA_KV_RANK), row(MLA_ROPE_DIM)],
        out_shape=[head_shape, head_shape,
                   jax.ShapeDtypeStruct((BATCH, per_seq, vt_rows, tm), BF16),
                   jax.ShapeDtypeStruct((h.shape[0], MLA_KV_RANK), F32),
                   jax.ShapeDtypeStruct((h.shape[0], MLA_ROPE_DIM), F32)],
        compiler_params=_params("parallel"),
        name="mla_pre_prompt",
    )(h, *consts)


def _mla_prompt_attn_body(q_ref, k_ref, vt_ref, o_ref, m_ref, acc_ref):
    i = pl.program_id(2)
    m_ref[...] = jnp.full_like(m_ref, NEG)
    acc_ref[...] = jnp.zeros_like(acc_ref)
    causal = _iota((MLA_TK, MLA_TQ), 0) <= _iota((MLA_TK, MLA_TQ), 1)

    def kv_tile(j, masked):
        k0 = pl.multiple_of(j * MLA_TK, MLA_TK)

        def scores(hh):
            s = _dot_nt(k_ref[0, hh, pl.ds(k0, MLA_TK), :], q_ref[0, hh])
            return jnp.where(causal, s, NEG) if masked else s

        def softmax(hh, s):
            m_prev = m_ref[hh]
            m_new = jnp.maximum(m_prev, jnp.max(s, axis=0, keepdims=True))
            m_ref[hh] = m_new
            return jnp.exp2(s - m_new).astype(BF16), jnp.exp2(m_prev - m_new)

        def values(hh, p, alpha):
            vt = vt_ref[0, j, hh * MLA_VT_ROWS:(hh + 1) * MLA_VT_ROWS, :]
            acc_ref[hh] = alpha * acc_ref[hh] + _dot(vt, p)

        s_of, p_of = {}, {}
        for step in range(MLA_HEAD_GROUP + 2):
            if step < MLA_HEAD_GROUP:
                s_of[step] = scores(step)
            if 0 <= step - 1 < MLA_HEAD_GROUP:
                p_of[step - 1] = softmax(step - 1, s_of.pop(step - 1))
            if 0 <= step - 2 < MLA_HEAD_GROUP:
                values(step - 2, *p_of.pop(step - 2))

    def below_diagonal(j, carry):
        kv_tile(j, False)
        return carry

    lax.fori_loop(0, i, below_diagonal, 0)
    kv_tile(i, True)
    for pair in range(MLA_HEAD_GROUP // 2):
        heads = []
        for hh in (2 * pair, 2 * pair + 1):
            acc = acc_ref[hh]
            heads.append(acc[:MLA_V_DIM] / acc[MLA_V_DIM:MLA_V_DIM + 1])
        o_t = jnp.concatenate(heads, axis=0)
        o_ref[0, :, pair * 2 * MLA_V_DIM:(pair + 1) * 2 * MLA_V_DIM] = o_t.T.astype(BF16)


def _mla_prompt_attn(q, k, vt):
    hg = MLA_HEAD_GROUP
    assert MLA_TQ == MLA_TK and hg % 2 == 0
    return pl.pallas_call(
        _mla_prompt_attn_body,
        grid=(BATCH, MLA_HEADS // hg, SEQ // MLA_TQ),
        in_specs=[pl.BlockSpec((1, hg, MLA_TQ, MLA_HEAD_PAD), lambda b, g, i: (b, g, i, 0)),
                  pl.BlockSpec((1, hg, SEQ, MLA_HEAD_PAD), lambda b, g, i: (b, g, 0, 0),
                               pipeline_mode=pl.Buffered(1)),
                  pl.BlockSpec((1, SEQ // MLA_TK, hg * MLA_VT_ROWS, MLA_TK), lambda b, g, i: (b, 0, g, 0),
                               pipeline_mode=pl.Buffered(1))],
        out_specs=pl.BlockSpec((1, MLA_TQ, hg * MLA_V_DIM), lambda b, g, i: (b, i, g)),
        out_shape=jax.ShapeDtypeStruct((BATCH, SEQ, MLA_HEADS * MLA_V_DIM), BF16),
        scratch_shapes=[pltpu.VMEM((hg, 1, MLA_TQ), F32), pltpu.VMEM((hg, MLA_VT_ROWS, MLA_TQ), F32)],
        compiler_params=_params("parallel", "parallel", "arbitrary"),
        name="mla_prompt_attn",
    )(q, k, vt)


def _mla_pre_sample_body(h_ref, g_pre, wdq_ref, g_q, wn_ref, wr_ref, wrs_ref, wukt_ref, wkv_ref, g_kv,
                         wpe_ref, wpes_ref, invf_ref, sgn_ref,
                         ql_ref, qp_ref, ckv_ref, kpe_ref):
    tm = TOKEN_TILE
    pos = (PAST_LEN + (_iota((tm, 1), 0) & (DEC_SEQ - 1))).astype(F32)
    cos32, sin32 = _rope_tables(pos, invf_ref[...], sgn_ref[...])
    cqn, ckv, kpe = _mla_latents(h_ref, g_pre, wdq_ref, g_q, wkv_ref, g_kv, wpe_ref, wpes_ref, cos32, sin32)
    ckv_ref[...] = ckv
    kpe_ref[...] = kpe
    n_seq = tm // DEC_SEQ
    for h in range(MLA_HEADS):
        rows = slice(h * DEC_SEQ, (h + 1) * DEC_SEQ)
        q_nope = _dot(cqn, wn_ref[h]).astype(BF16)
        q_lat = _dot(q_nope, wukt_ref[h]) * (MLA_SCALE * LOG2_E)
        ql_ref[:, rows, :] = q_lat.reshape(n_seq, DEC_SEQ, MLA_KV_RANK)
        q_pe = (_dot(cqn, wr_ref[h]) * cos32 + _dot(cqn, wrs_ref[h]) * sin32) * (MLA_SCALE * LOG2_E)
        qp_ref[:, rows, :] = q_pe.reshape(n_seq, DEC_SEQ, MLA_ROPE_DIM)


def _mla_pre_sample(h, g_pre, wdq, g_q, wn, wr, wrs, wukt, wkv, g_kv, wpe, wpes, invf, sgn):
    tm = TOKEN_TILE
    n_seq = tm // DEC_SEQ
    rows = MLA_HEADS * DEC_SEQ
    row = lambda n: pl.BlockSpec((tm, n), lambda i: (i, 0))
    consts = [g_pre, wdq, g_q, wn, wr, wrs, wukt, wkv, g_kv, wpe, wpes, invf, sgn]
    return pl.pallas_call(
        _mla_pre_sample_body,
        grid=(h.shape[0] // tm,),
        in_specs=[row(D_MODEL)] + [_resident(c.shape) for c in consts],
        out_specs=[pl.BlockSpec((n_seq, rows, MLA_KV_RANK), lambda i: (i, 0, 0)),
                   pl.BlockSpec((n_seq, rows, MLA_ROPE_DIM), lambda i: (i, 0, 0)),
                   row(MLA_KV_RANK), row(MLA_ROPE_DIM)],
        out_shape=[jax.ShapeDtypeStruct((DEC_BATCH, rows, MLA_KV_RANK), F32),
                   jax.ShapeDtypeStruct((DEC_BATCH, rows, MLA_ROPE_DIM), F32),
                   jax.ShapeDtypeStruct((h.shape[0], MLA_KV_RANK), F32),
                   jax.ShapeDtypeStruct((h.shape[0], MLA_ROPE_DIM), F32)],
        compiler_params=_params("parallel"),
        name="mla_pre_sample",
    )(h, *consts)


def _mla_sample_attn_body(pt_ref, ql_ref, qp_ref, cn_ref, pn_ref, ckv_hbm, kpe_hbm, o_ref,
                          m_ref, l_ref, acc_ref, cnp_ref, pnp_ref, ck_buf, kp_buf, ck_sem, kp_sem):
    npg = PAGES_PER_GROUP
    groups_per_seq = N_PAGES // npg
    total_groups = DEC_BATCH * groups_per_seq
    b = pl.program_id(0)
    rows = MLA_HEADS * DEC_SEQ

    def page_copies(page_of, slot):
        copies = []
        for j in range(npg):
            page = page_of(j)
            copies.append(pltpu.make_async_copy(
                ckv_hbm.at[0, page], ck_buf.at[slot, pl.ds(j * PAGE_SIZE, PAGE_SIZE), :], ck_sem.at[slot]))
            copies.append(pltpu.make_async_copy(kpe_hbm.at[0, page], kp_buf.at[slot, j], kp_sem.at[slot]))
        return copies

    def start_group(t, slot):
        for c in page_copies(lambda j: pt_ref[t * npg + j], slot):
            c.start()

    def wait_group(slot):
        for c in page_copies(lambda j: 0, slot):
            c.wait()

    @pl.when(b == 0)
    def _():
        for t0 in range(PAGE_RING):
            start_group(t0, t0)

    m_ref[...] = jnp.full_like(m_ref, NEG)
    l_ref[...] = jnp.zeros_like(l_ref)
    acc_ref[...] = jnp.zeros_like(acc_ref)
    ql = ql_ref[...].astype(BF16)
    qp = qp_ref[...].astype(BF16)

    def update(s, keys):
        m_prev = m_ref[...]
        m_new = jnp.maximum(m_prev, jnp.max(s, axis=-1, keepdims=True))
        alpha = jnp.exp2(m_prev - m_new)
        p = jnp.exp2(s - m_new)
        l_ref[...] = alpha * l_ref[...] + jnp.sum(p, axis=-1, keepdims=True)
        acc_ref[...] = alpha * acc_ref[...] + _dot(p.astype(BF16), keys)
        m_ref[...] = m_new

    def group(g, carry):
        t = b * groups_per_seq + g
        slot = t % PAGE_RING
        wait_group(slot)
        def scores(c):
            pages = range(c * PAGES_PER_CHUNK, (c + 1) * PAGES_PER_CHUNK)
            kl = ck_buf[slot, pages[0] * PAGE_SIZE:(pages[-1] + 1) * PAGE_SIZE, :].astype(BF16)
            kpt = jnp.concatenate([kp_buf[slot, j].astype(BF16) for j in pages], axis=1)
            return _dot_nt(ql, kl) + _dot(qp, kpt), kl

        n_chunks = npg // PAGES_PER_CHUNK
        nxt = scores(0)
        for c in range(n_chunks):
            cur = nxt
            if c + 1 < n_chunks:
                nxt = scores(c + 1)
            update(*cur)

        @pl.when(t + PAGE_RING < total_groups)
        def _():
            start_group(t + PAGE_RING, slot)

        return carry

    lax.fori_loop(0, groups_per_seq, group, 0)

    cnp_ref[...] = jnp.zeros_like(cnp_ref)
    pnp_ref[...] = jnp.zeros_like(pnp_ref)
    cnp_ref[:DEC_SEQ, :] = cn_ref[...]
    pnp_ref[:DEC_SEQ, :] = pn_ref[...]
    cn = cnp_ref[...].astype(BF16)
    t_q = _iota((rows, PAGE_SIZE), 0) & (DEC_SEQ - 1)
    s_new = jnp.where(_iota((rows, PAGE_SIZE), 1) <= t_q,
                      _dot_nt(ql, cn) + _dot_nt(qp, pnp_ref[...].astype(BF16)), NEG)
    update(s_new, cn)
    o_ref[...] = (acc_ref[...] / l_ref[...]).astype(BF16)


def _mla_sample_attn(page_table, ql, qp, ckv_new, kpe_new, ckv_pool, kpe_pool):
    npg = PAGES_PER_GROUP
    rows = MLA_HEADS * DEC_SEQ
    assert N_PAGES % npg == 0 and npg % PAGES_PER_CHUNK == 0
    seq = lambda r, w: pl.BlockSpec((None, r, w), lambda b, pt: (b, 0, 0))
    hbm = pl.BlockSpec(memory_space=pl.ANY)
    grid_spec = pltpu.PrefetchScalarGridSpec(
        num_scalar_prefetch=1,
        grid=(DEC_BATCH,),
        in_specs=[seq(rows, MLA_KV_RANK), seq(rows, MLA_ROPE_DIM), seq(DEC_SEQ, MLA_KV_RANK),
                  seq(DEC_SEQ, MLA_ROPE_DIM), hbm, hbm],
        out_specs=seq(rows, MLA_KV_RANK),
        scratch_shapes=[pltpu.VMEM((rows, 1), F32), pltpu.VMEM((rows, 1), F32), pltpu.VMEM((rows, MLA_KV_RANK), F32),
                        pltpu.VMEM((PAGE_SIZE, MLA_KV_RANK), F32), pltpu.VMEM((PAGE_SIZE, MLA_ROPE_DIM), F32),
                        pltpu.VMEM((PAGE_RING, npg * PAGE_SIZE, MLA_KV_RANK), F32),
                        pltpu.VMEM((PAGE_RING, npg, MLA_ROPE_DIM, PAGE_SIZE), F32),
                        pltpu.SemaphoreType.DMA((PAGE_RING,)), pltpu.SemaphoreType.DMA((PAGE_RING,))],
    )
    return pl.pallas_call(
        _mla_sample_attn_body,
        grid_spec=grid_spec,
        out_shape=jax.ShapeDtypeStruct((DEC_BATCH, rows, MLA_KV_RANK), BF16),
        compiler_params=_params("arbitrary"),
        name="mla_sample_attn",
    )(page_table, ql, qp, ckv_new, kpe_new, ckv_pool, kpe_pool)


def _mla_sample_out_body(x_ref, w_ref, o_ref):
    x = x_ref[...].reshape(DEC_BATCH * 2 * DEC_SEQ, MLA_KV_RANK)
    y = _dot(x, w_ref[0]).reshape(DEC_BATCH, 2 * DEC_SEQ, 2 * MLA_V_DIM)
    first_head = _iota((DEC_BATCH, DEC_SEQ, 2 * MLA_V_DIM), 2) < MLA_V_DIM
    o = jnp.where(first_head, y[:, :DEC_SEQ], y[:, DEC_SEQ:])
    o_ref[...] = o.reshape(DEC_BATCH * DEC_SEQ, 2 * MLA_V_DIM)


def _mla_sample_out(o_lat, wuv_pairs):
    return pl.pallas_call(
        _mla_sample_out_body,
        grid=(MLA_HEADS // 2,),
        in_specs=[pl.BlockSpec((DEC_BATCH, 2 * DEC_SEQ, MLA_KV_RANK), lambda j: (0, j, 0)),
                  pl.BlockSpec((1, MLA_KV_RANK, 2 * MLA_V_DIM), lambda j: (j, 0, 0))],
        out_specs=pl.BlockSpec((DEC_BATCH * DEC_SEQ, 2 * MLA_V_DIM), lambda j: (0, j)),
        out_shape=jax.ShapeDtypeStruct((DEC_BATCH * DEC_SEQ, MLA_HEADS * MLA_V_DIM), F32),
        compiler_params=_params("parallel"),
        name="mla_sample_out",
    )(o_lat, wuv_pairs)


def _swap_halves(w):
    half = w.shape[-1] // 2
    return jnp.concatenate([w[..., half:], w[..., :half]], axis=-1)


def _rope_consts():
    half = MLA_ROPE_DIM // 2
    inv_freq = ROPE_THETA ** (-jnp.arange(0, half, dtype=F32) * 2.0 / MLA_ROPE_DIM)
    invf32 = jnp.concatenate([inv_freq, inv_freq])[None]
    sgn32 = jnp.concatenate([-jnp.ones((half,), F32), jnp.ones((half,), F32)])[None]
    pad = lambda a: jnp.pad(a, ((0, 0), (MLA_NOPE_DIM, MLA_HEAD_PAD - MLA_NOPE_DIM - MLA_ROPE_DIM)))
    return invf32, sgn32, pad(invf32), pad(sgn32)


def _mla_weights(w_uq, w_dkv, w_uk, w_uv):
    wq = w_uq.reshape(MLA_Q_RANK, MLA_HEADS, MLA_NOPE_DIM + MLA_ROPE_DIM)
    wq_nope, wq_rope = wq[..., :MLA_NOPE_DIM], wq[..., MLA_NOPE_DIM:]
    tail = MLA_HEAD_PAD - MLA_NOPE_DIM - MLA_ROPE_DIM
    pad_last = lambda a, lo, hi: jnp.pad(a, [(0, 0)] * (a.ndim - 1) + [(lo, hi)])
    flat = lambda a: a.reshape(a.shape[0], MLA_HEADS * MLA_HEAD_PAD).astype(BF16)
    w = {}
    w["wq"] = flat(pad_last(wq, 0, tail))
    w["wkvp"] = jnp.concatenate([w_dkv[:, :MLA_KV_RANK], pad_last(w_dkv[:, MLA_KV_RANK:], MLA_NOPE_DIM, tail)],
                                axis=1).astype(BF16)
    w["wuk"] = flat(pad_last(w_uk, 0, MLA_HEAD_PAD - MLA_NOPE_DIM))
    wuvt = jnp.transpose(w_uv, (1, 2, 0))
    w["wuvt"] = jnp.pad(wuvt, ((0, 0), (0, MLA_VT_ROWS - MLA_V_DIM), (0, 0))).reshape(
        MLA_HEADS * MLA_VT_ROWS, MLA_KV_RANK).astype(BF16)
    w["ones_col"] = jnp.tile((jnp.arange(MLA_VT_ROWS) == MLA_V_DIM).astype(F32), MLA_HEADS)[:, None]
    w["wn"] = jnp.moveaxis(wq_nope, 1, 0).astype(BF16)
    w["wr"] = jnp.moveaxis(wq_rope, 1, 0).astype(BF16)
    w["wrs"] = jnp.moveaxis(_swap_halves(wq_rope), 1, 0).astype(BF16)
    w["wukt"] = jnp.transpose(w_uk, (1, 2, 0)).astype(BF16)
    w["wkv"] = w_dkv[:, :MLA_KV_RANK].astype(BF16)
    w["wpe"] = w_dkv[:, MLA_KV_RANK:].astype(BF16)
    w["wpes"] = _swap_halves(w_dkv[:, MLA_KV_RANK:]).astype(BF16)
    w["wuv_pairs"] = jnp.transpose(w_uv.reshape(MLA_KV_RANK, MLA_HEADS // 2, 2 * MLA_V_DIM),
                                   (1, 0, 2)).astype(BF16)
    return w


def kernel(x_prompt, x_sample, cache_swa_k, cache_swa_v, state_hgrn, cache_mla_ckv, cache_mla_kpe, page_table, p_prompt, p_sample, ln_mix_pre, ln_mix_post, ln_ffn_pre, ln_ffn_post, ln_ple, w_ab_in, w_ab_out, swa_sinks, hgrn_lb_logits, hgrn_out_norm, w_mla_dq, mla_q_norm, w_mla_uq, w_mla_dkv, mla_kv_norm, w_mla_uk, w_mla_uv, w_mla_o, w_ffn_gate, w_ffn_up, w_ffn_down, w_ple_proj, w_ple_gate):
    tp = BATCH * SEQ
    ts = DEC_BATCH * DEC_SEQ
    bf = lambda a: a.astype(BF16)
    gain = lambda a, i: a[i][None].astype(F32)

    def finish(mix, h, p, i, wo):
        return _post(mix, h, p, wo, gain(ln_mix_post, i), gain(ln_ffn_pre, i), bf(w_ffn_gate[i]), bf(w_ffn_up[i]),
                     bf(w_ffn_down[i]), gain(ln_ffn_post, i), bf(w_ple_proj[i]), bf(w_ple_gate[i]), gain(ln_ple, i))

    hp = x_prompt.reshape(tp, D_MODEL)
    hs = x_sample.reshape(ts, D_MODEL)
    pp = p_prompt.reshape(2, tp, PLE_DIM)
    ps = p_sample.reshape(2, ts, PLE_DIM)

    w_in = bf(w_ab_in[0])
    w_out = bf(w_ab_out[0])
    g0 = gain(ln_mix_pre, 0)
    sinks = swa_sinks[0].astype(F32)
    lbl = hgrn_lb_logits.astype(F32)
    on = hgrn_out_norm[0][None].astype(F32)

    qa, ka, va, hg = _ab_in(hp, g0, w_in, BF16)
    oa = _swa_prompt(sinks, qa.reshape(BATCH, SEQ, SWA_Q_W), ka.reshape(BATCH, SEQ, SWA_KV_W),
                     va.reshape(BATCH, SEQ, SWA_KV_W))
    ob, hgrn_p = _hgrn_prompt(lbl, on, hg.reshape(BATCH, SEQ, 4 * HGRN_W))
    swa_k_p = ka.reshape(BATCH, SEQ, SWA_KV_HEADS, SWA_HEAD_DIM)[:, -WINDOW:]
    swa_v_p = va.reshape(BATCH, SEQ, SWA_KV_HEADS, SWA_HEAD_DIM)[:, -WINDOW:]
    hp = finish([oa.reshape(tp, SWA_Q_W), ob.reshape(tp, HGRN_W)], hp, pp[0], 0, w_out)

    qa, ka, va, hg = _ab_in(hs, g0, w_in, F32)
    oa, swa_k_s, swa_v_s = _swa_sample(
        sinks, qa, ka.reshape(DEC_BATCH, DEC_SEQ, SWA_KV_W),
        va.reshape(DEC_BATCH, DEC_SEQ, SWA_KV_W), cache_swa_k[0].reshape(DEC_BATCH, WINDOW, SWA_KV_W),
        cache_swa_v[0].reshape(DEC_BATCH, WINDOW, SWA_KV_W))
    ob, hgrn_s = _hgrn_sample(lbl, on, hg, state_hgrn[0])
    hs = finish([oa.reshape(ts, SWA_Q_W), ob], hs, ps[0], 0, w_out)

    mw = _mla_weights(w_mla_uq[0], w_mla_dkv[0], w_mla_uk[0], w_mla_uv[0])
    invf32, sgn32, invf128, sgn128 = _rope_consts()
    g1 = gain(ln_mix_pre, 1)
    wdq = bf(w_mla_dq[0])
    g_q = mla_q_norm[0][None].astype(F32)
    g_kv = mla_kv_norm[0][None].astype(F32)
    w_o = bf(w_mla_o[0])

    q, k, vt, ckv_p, kpe_p = _mla_pre_prompt(hp, g1, wdq, g_q, mw["wq"], mw["wkvp"], g_kv, mw["wuk"], mw["wuvt"],
                                             mw["ones_col"], invf128, sgn128)
    o = _mla_prompt_attn(q, k, vt)
    hp = finish([o.reshape(tp, MLA_HEADS * MLA_V_DIM)], hp, pp[1], 1, w_o)

    ql, qp, ckv_s, kpe_s = _mla_pre_sample(hs, g1, wdq, g_q, mw["wn"], mw["wr"], mw["wrs"], mw["wukt"], mw["wkv"],
                                           g_kv, mw["wpe"], mw["wpes"], invf32, sgn32)
    o_lat = _mla_sample_attn(page_table.reshape(-1), ql, qp, ckv_s.reshape(DEC_BATCH, DEC_SEQ, MLA_KV_RANK),
                             kpe_s.reshape(DEC_BATCH, DEC_SEQ, MLA_ROPE_DIM), cache_mla_ckv,
                             jnp.swapaxes(cache_mla_kpe, 2, 3))
    o = _mla_sample_out(o_lat, mw["wuv_pairs"])
    hs = finish([o], hs, ps[1], 1, w_o)

    kv5 = lambda a, n: a.reshape(1, n, WINDOW, SWA_KV_HEADS, SWA_HEAD_DIM)
    return (hp.reshape(BATCH, SEQ, D_MODEL), hs.reshape(DEC_BATCH, DEC_SEQ, D_MODEL),
            swa_k_p[None], swa_v_p[None], hgrn_p[None],
            ckv_p.reshape(1, BATCH, SEQ, MLA_KV_RANK), kpe_p.reshape(1, BATCH, SEQ, MLA_ROPE_DIM),
            kv5(swa_k_s, DEC_BATCH), kv5(swa_v_s, DEC_BATCH), hgrn_s[None],
            ckv_s.reshape(1, DEC_BATCH, DEC_SEQ, MLA_KV_RANK), kpe_s.reshape(1, DEC_BATCH, DEC_SEQ, MLA_ROPE_DIM))
```

```python
import functools

import jax
import jax.numpy as jnp
from jax import lax
from jax.experimental import pallas as pl
from jax.experimental.pallas import tpu as pltpu

F32 = jnp.float32
BF16 = jnp.bfloat16

D_MODEL = 1024
BATCH = 2
SEQ = 8192
DEC_BATCH = 128
DEC_SEQ = 8
PAST_LEN = 16384
PAGE_SIZE = 128
N_PAGES = PAST_LEN // PAGE_SIZE

SWA_HEADS = 8
SWA_KV_HEADS = 2
SWA_GROUP = SWA_HEADS // SWA_KV_HEADS
SWA_HEAD_DIM = 64
WINDOW = 128
SWA_SCALE = SWA_HEAD_DIM ** -0.5
SWA_Q_W = SWA_HEADS * SWA_HEAD_DIM
SWA_KV_W = SWA_KV_HEADS * SWA_HEAD_DIM

HGRN_HEADS = 4
HGRN_HEAD_DIM = 128
HGRN_CHUNK = 32
HGRN_W = HGRN_HEADS * HGRN_HEAD_DIM
AB_IN_W = SWA_Q_W + 2 * SWA_KV_W + 4 * HGRN_W

MLA_HEADS = 16
MLA_NOPE_DIM = 64
MLA_ROPE_DIM = 32
MLA_V_DIM = 64
MLA_Q_RANK = 512
MLA_KV_RANK = 256
MLA_SCALE = (MLA_NOPE_DIM + MLA_ROPE_DIM) ** -0.5
ROPE_THETA = 10000.0
MLA_HEAD_PAD = 128

D_FF = 2816
PLE_DIM = 256
NORM_EPS = 1e-6
NEG = -1e30
LOG2_E = 1.4426950408889634

VMEM_LIMIT_BYTES = 56 * 1024 * 1024

TOKEN_TILE = 512
HGRN_TILE = 256
SWA_TQ = 256
SAMPLE_BATCH_TILE = 16
MLA_TQ = 1024
MLA_TK = 512
MLA_HEAD_GROUP = 8
MLA_VT_ROWS = 80
PAGES_PER_GROUP = 32
PAGES_PER_CHUNK = 16
PAGE_RING = 3


def _dot(a, b):
    return jnp.dot(a, b, preferred_element_type=F32)


def _dot_nt(a, b):
    return lax.dot_general(a, b, (((1,), (1,)), ((), ())), preferred_element_type=F32)


def _rms(x, g):
    return x * lax.rsqrt(jnp.mean(x * x, axis=-1, keepdims=True) + NORM_EPS) * g


def _sigmoid(x):
    return 1.0 / (1.0 + jnp.exp(-x))


def _silu(x):
    return x * _sigmoid(x)


def _iota(shape, dim):
    return lax.broadcasted_iota(jnp.int32, shape, dim)


def _params(*sem):
    return pltpu.CompilerParams(dimension_semantics=sem, vmem_limit_bytes=VMEM_LIMIT_BYTES)


def _resident(shape):
    nd = len(shape)
    return pl.BlockSpec(shape, lambda *_: (0,) * nd, pipeline_mode=pl.Buffered(1))


def _smem():
    return pl.BlockSpec(memory_space=pltpu.SMEM)


def _ab_in_body(h_ref, g_ref, w_ref, qa_ref, k_ref, v_ref, hg_ref):
    u = _rms(h_ref[...], g_ref[...]).astype(BF16)
    qa_ref[...] = (_dot(u, w_ref[:, :SWA_Q_W]) * SWA_SCALE).astype(qa_ref.dtype)
    k_ref[...] = _dot(u, w_ref[:, SWA_Q_W:SWA_Q_W + SWA_KV_W])
    v_ref[...] = _dot(u, w_ref[:, SWA_Q_W + SWA_KV_W:SWA_Q_W + 2 * SWA_KV_W])
    hg_ref[...] = _dot(u, w_ref[:, SWA_Q_W + 2 * SWA_KV_W:])


def _ab_in(h, g, w, q_dtype):
    t = h.shape[0]
    tm = TOKEN_TILE
    row = lambda n: pl.BlockSpec((tm, n), lambda i: (i, 0))
    return pl.pallas_call(
        _ab_in_body,
        grid=(t // tm,),
        in_specs=[row(D_MODEL), _resident((1, D_MODEL)), _resident((D_MODEL, AB_IN_W))],
        out_specs=[row(SWA_Q_W), row(SWA_KV_W), row(SWA_KV_W), row(4 * HGRN_W)],
        out_shape=[jax.ShapeDtypeStruct((t, SWA_Q_W), q_dtype),
                   jax.ShapeDtypeStruct((t, SWA_KV_W), F32),
                   jax.ShapeDtypeStruct((t, SWA_KV_W), F32),
                   jax.ShapeDtypeStruct((t, 4 * HGRN_W), F32)],
        compiler_params=_params("parallel"),
        name="ab_in",
    )(h, g, w)


def _swa_head(q, keys, vals, dists, valids, slope, sink):
    scores = []
    for k, dist, valid in zip(keys, dists, valids):
        s = _dot_nt(q, k) - slope * dist
        scores.append(jnp.where(valid, s, NEG))
    m = sink
    for s in scores:
        m = jnp.maximum(m, jnp.max(s, axis=-1, keepdims=True))
    den = jnp.exp(sink - m)
    acc = None
    for s, v in zip(scores, vals):
        p = jnp.exp(s - m)
        den = den + jnp.sum(p, axis=-1, keepdims=True)
        pv = _dot(p.astype(BF16), v)
        acc = pv if acc is None else acc + pv
    return acc / den


def _swa_prompt_body(sink_ref, q_ref, kc_ref, kp_ref, vc_ref, vp_ref, o_ref):
    i = pl.program_id(1)
    nk = WINDOW + SWA_TQ
    kk = jnp.concatenate([kp_ref[0], kc_ref[0]], axis=0).astype(BF16)
    vv_t = jnp.concatenate([vp_ref[0], vc_ref[0]], axis=0).T.astype(BF16)
    key = _iota((nk, SWA_TQ), 0)
    qry = _iota((nk, SWA_TQ), 1)
    dist_i = qry + WINDOW - key
    valid = (dist_i >= 0) & (dist_i <= WINDOW) & (key >= jnp.where(i > 0, 0, WINDOW))
    dist = dist_i.astype(F32)
    q = q_ref[0]
    kv_lanes = lambda j: slice(j // SWA_GROUP * SWA_HEAD_DIM, (j // SWA_GROUP + 1) * SWA_HEAD_DIM)

    def scores(j):
        s = _dot_nt(kk[:, kv_lanes(j)], q[:, j * SWA_HEAD_DIM:(j + 1) * SWA_HEAD_DIM]) - 2.0 ** -(j + 1) * dist
        return jnp.where(valid, s, NEG)

    def attend(j, s):
        m = jnp.maximum(jnp.max(s, axis=0, keepdims=True), sink_ref[j])
        p = jnp.exp(s - m)
        den = jnp.sum(p, axis=0, keepdims=True) + jnp.exp(sink_ref[j] - m)
        return _dot(vv_t[kv_lanes(j), :], p.astype(BF16)) / den

    heads = []
    nxt = scores(0)
    for j in range(SWA_HEADS):
        cur = nxt
        if j + 1 < SWA_HEADS:
            nxt = scores(j + 1)
        heads.append(attend(j, cur))
    for pair in range(SWA_HEADS // 2):
        o_t = jnp.concatenate(heads[2 * pair:2 * pair + 2], axis=0)
        o_ref[0, :, pair * 2 * SWA_HEAD_DIM:(pair + 1) * 2 * SWA_HEAD_DIM] = o_t.T.astype(BF16)


def _swa_prompt(sinks, qa, k, v):
    per_tile = SWA_TQ // WINDOW
    cur = lambda w: pl.BlockSpec((1, SWA_TQ, w), lambda b, i: (b, i, 0))
    prev = lambda w: pl.BlockSpec((1, WINDOW, w), lambda b, i: (b, jnp.maximum(i * per_tile - 1, 0), 0))
    return pl.pallas_call(
        _swa_prompt_body,
        grid=(BATCH, SEQ // SWA_TQ),
        in_specs=[_smem(), cur(SWA_Q_W), cur(SWA_KV_W), prev(SWA_KV_W), cur(SWA_KV_W), prev(SWA_KV_W)],
        out_specs=cur(SWA_Q_W),
        out_shape=jax.ShapeDtypeStruct((BATCH, SEQ, SWA_Q_W), BF16),
        compiler_params=_params("parallel", "arbitrary"),
        name="swa_prompt",
    )(sinks, qa, k, k, v, v)


def _swa_sample_body(sink_ref, q_ref, kn_ref, vn_ref, kb_ref, vb_ref, o_ref, ko_ref, vo_ref):
    bb = SAMPLE_BATCH_TILE
    t_new = DEC_SEQ
    r = bb * t_new
    nk = bb * WINDOW
    kb3, vb3, kn3, vn3 = kb_ref[...], vb_ref[...], kn_ref[...], vn_ref[...]
    ko_ref[:, :WINDOW - t_new, :] = kb3[:, t_new:, :]
    ko_ref[:, WINDOW - t_new:, :] = kn3
    vo_ref[:, :WINDOW - t_new, :] = vb3[:, t_new:, :]
    vo_ref[:, WINDOW - t_new:, :] = vn3
    kb = kb3.reshape(nk, SWA_KV_W).astype(BF16)
    vb = vb3.reshape(nk, SWA_KV_W).astype(BF16)
    kn = kn3.reshape(r, SWA_KV_W).astype(BF16)
    vn = vn3.reshape(r, SWA_KV_W).astype(BF16)

    row = _iota((r, nk), 0)
    col = _iota((r, nk), 1)
    t_q = row & (t_new - 1)
    slot = col & (WINDOW - 1)
    dist_b = (t_q - slot + WINDOW).astype(F32)
    valid_b = ((row >> 3) == (col >> 7)) & (slot >= t_q)
    row_n = _iota((r, r), 0)
    col_n = _iota((r, r), 1)
    dist_n = ((row_n & (t_new - 1)) - (col_n & (t_new - 1))).astype(F32)
    valid_n = ((row_n >> 3) == (col_n >> 3)) & (col_n <= row_n)
    q = q_ref[...].astype(BF16)
    for hk in range(SWA_KV_HEADS):
        ks = slice(hk * SWA_HEAD_DIM, (hk + 1) * SWA_HEAD_DIM)
        for g in range(SWA_GROUP):
            j = hk * SWA_GROUP + g
            qs = slice(j * SWA_HEAD_DIM, (j + 1) * SWA_HEAD_DIM)
            o_ref[:, qs] = _swa_head(q[:, qs], (kb[:, ks], kn[:, ks]), (vb[:, ks], vn[:, ks]),
                                     (dist_b, dist_n), (valid_b, valid_n), 2.0 ** -(j + 1), sink_ref[j])


def _swa_sample(sinks, qa, kn, vn, kbuf, vbuf):
    bb = SAMPLE_BATCH_TILE
    assert DEC_SEQ == 8 and WINDOW == 128
    blk = lambda r, w: pl.BlockSpec((bb, r, w), lambda i: (i, 0, 0))
    rows = pl.BlockSpec((bb * DEC_SEQ, SWA_Q_W), lambda i: (i, 0))
    return pl.pallas_call(
        _swa_sample_body,
        grid=(DEC_BATCH // bb,),
        in_specs=[_smem(), rows, blk(DEC_SEQ, SWA_KV_W), blk(DEC_SEQ, SWA_KV_W),
                  blk(WINDOW, SWA_KV_W), blk(WINDOW, SWA_KV_W)],
        out_specs=[rows, blk(WINDOW, SWA_KV_W), blk(WINDOW, SWA_KV_W)],
        out_shape=[jax.ShapeDtypeStruct((DEC_BATCH * DEC_SEQ, SWA_Q_W), F32),
                   jax.ShapeDtypeStruct((DEC_BATCH, WINDOW, SWA_KV_W), F32),
                   jax.ShapeDtypeStruct((DEC_BATCH, WINDOW, SWA_KV_W), F32)],
        compiler_params=_params("parallel"),
        name="swa_sample",
    )(sinks, qa, kn, vn, kbuf, vbuf)


def _hgrn_lower_bound(lbl_ref, layer):
    logits = lbl_ref[...]
    e = jnp.exp(logits - jnp.max(logits, axis=0, keepdims=True))
    sm = e / jnp.sum(e, axis=0, keepdims=True)
    return jnp.sum(sm[:layer + 1], axis=0, keepdims=True)


def _split3(x):
    hi = x.astype(BF16)
    r = x - hi.astype(F32)
    mid = r.astype(BF16)
    lo = (r - mid.astype(F32)).astype(BF16)
    return hi, mid, lo


def _hgrn_chunk_terms(x, lb, chunk):
    r = x.shape[0]
    qr, fr, ir, gr = (x[:, i * HGRN_W:(i + 1) * HGRN_W] for i in range(4))
    f = lb + (1.0 - lb) * _sigmoid(fr)
    logf = jnp.log(f)
    row = _iota((r, r), 0)
    col = _iota((r, r), 1)
    chunk_start = row & ~(chunk - 1)
    same = (col >= chunk_start) & (col < chunk_start + chunk)
    tril = (col >= chunk_start) & (col <= row)
    parts = _split3(logf)
    tri_m = jnp.where(tril, 1.0, 0.0).astype(BF16)
    blk_m = jnp.where(same, 1.0, 0.0).astype(BF16)
    b = sum(_dot(tri_m, p) for p in parts)
    bl = sum(_dot(blk_m, p) for p in parts)
    k = 1.0 - f
    qd = _silu(qr) * jnp.exp(b)
    kd = k * jnp.exp(-b)
    k2 = k * jnp.exp(bl - b)
    return qd, kd, k2, ir, gr, bl, tril


def _hgrn_intra(qd, kd, v, tril):
    att = jnp.where(tril, _dot_nt(qd.astype(BF16), kd.astype(BF16)), 0.0)
    return _dot(att.astype(BF16), v.astype(BF16))


def _hgrn_finish(o, g, on):
    return (_rms(o, on) * _silu(g)).astype(BF16)


def _hgrn_prompt_body(lbl_ref, on_ref, x_ref, o_ref, s_ref, st_ref):
    t = pl.program_id(1)

    @pl.when(t == 0)
    def _():
        st_ref[...] = jnp.zeros_like(st_ref)

    lb = _hgrn_lower_bound(lbl_ref, 0)
    qd, kd, k2, v, g, bl, tril = _hgrn_chunk_terms(x_ref[0], lb, HGRN_CHUNK)
    rows = _iota((HGRN_TILE, HGRN_HEAD_DIM), 0)
    n_chunks = HGRN_TILE // HGRN_CHUNK
    head = [slice(h * HGRN_HEAD_DIM, (h + 1) * HGRN_HEAD_DIM) for h in range(HGRN_HEADS)]
    o_intra = [_hgrn_intra(qd[:, hs], kd[:, hs], v[:, hs], tril) for hs in head]
    qdb = [qd[:, hs].astype(BF16) for hs in head]
    vt = [v[:, hs].T.astype(BF16) for hs in head]
    st = [st_ref[h] for h in range(HGRN_HEADS)]
    o_inter = [[] for _ in head]
    for c in range(n_chunks):
        r0 = c * HGRN_CHUNK
        in_chunk = (rows >= r0) & (rows < r0 + HGRN_CHUNK)
        for h, hs in enumerate(head):
            o_inter[h].append(_dot_nt(qdb[h][r0:r0 + HGRN_CHUNK], st[h].astype(BF16)))
            d_st = _dot(vt[h], jnp.where(in_chunk, k2[:, hs], 0.0).astype(BF16))
            st[h] = st[h] * jnp.exp(bl[r0:r0 + 1, hs]) + d_st
    for h, hs in enumerate(head):
        st_ref[h] = st[h]
        o = o_intra[h] + jnp.concatenate(o_inter[h], axis=0)
        o_ref[0, :, hs] = _hgrn_finish(o, g[:, hs], on_ref[...])

    @pl.when(t == pl.num_programs(1) - 1)
    def _():
        for h in range(HGRN_HEADS):
            s_ref[0, h] = st_ref[h].T


def _hgrn_prompt(lbl, on, hg):
    tt = HGRN_TILE
    return pl.pallas_call(
        _hgrn_prompt_body,
        grid=(BATCH, SEQ // tt),
        in_specs=[_resident(lbl.shape), _resident((1, HGRN_HEAD_DIM)),
                  pl.BlockSpec((1, tt, 4 * HGRN_W), lambda b, t: (b, t, 0))],
        out_specs=[pl.BlockSpec((1, tt, HGRN_W), lambda b, t: (b, t, 0)),
                   pl.BlockSpec((1, HGRN_HEADS, HGRN_HEAD_DIM, HGRN_HEAD_DIM), lambda b, t: (b, 0, 0, 0))],
        out_shape=[jax.ShapeDtypeStruct((BATCH, SEQ, HGRN_W), BF16),
                   jax.ShapeDtypeStruct((BATCH, HGRN_HEADS, HGRN_HEAD_DIM, HGRN_HEAD_DIM), F32)],
        scratch_shapes=[pltpu.VMEM((HGRN_HEADS, HGRN_HEAD_DIM, HGRN_HEAD_DIM), F32)],
        compiler_params=_params("parallel", "arbitrary"),
        name="hgrn_prompt",
    )(lbl, on, hg)


def _hgrn_sample_body(lbl_ref, on_ref, x_ref, s0_ref, o_ref, s_ref):
    bb = SAMPLE_BATCH_TILE
    r = bb * DEC_SEQ
    lb = _hgrn_lower_bound(lbl_ref, 0)
    qd, kd, k2, v, g, bl, tril = _hgrn_chunk_terms(x_ref[...], lb, DEC_SEQ)
    cols = _iota((HGRN_HEAD_DIM, r), 1)
    for h in range(HGRN_HEADS):
        hs = slice(h * HGRN_HEAD_DIM, (h + 1) * HGRN_HEAD_DIM)
        o_intra = _hgrn_intra(qd[:, hs], kd[:, hs], v[:, hs], tril)
        qdb = qd[:, hs].astype(BF16)
        vb = v[:, hs].astype(BF16)
        k2t = k2[:, hs].T
        decay_t = jnp.exp(bl[:, hs].T)
        o_inter = []
        for b in range(bb):
            r0 = b * DEC_SEQ
            s0 = s0_ref[b, h]
            o_inter.append(_dot(qdb[r0:r0 + DEC_SEQ], s0.astype(BF16)))
            in_seq = (cols >= r0) & (cols < r0 + DEC_SEQ)
            d_s = _dot(jnp.where(in_seq, k2t, 0.0).astype(BF16), vb)
            s_ref[b, h] = s0 * decay_t[:, r0:r0 + 1] + d_s
        o = o_intra + jnp.concatenate(o_inter, axis=0)
        o_ref[:, hs] = _hgrn_finish(o, g[:, hs], on_ref[...])


def _hgrn_sample(lbl, on, hg, s0):
    bb = SAMPLE_BATCH_TILE
    r = bb * DEC_SEQ
    st = pl.BlockSpec((bb, HGRN_HEADS, HGRN_HEAD_DIM, HGRN_HEAD_DIM), lambda i: (i, 0, 0, 0))
    return pl.pallas_call(
        _hgrn_sample_body,
        grid=(DEC_BATCH // bb,),
        in_specs=[_resident(lbl.shape), _resident((1, HGRN_HEAD_DIM)),
                  pl.BlockSpec((r, 4 * HGRN_W), lambda i: (i, 0)), st],
        out_specs=[pl.BlockSpec((r, HGRN_W), lambda i: (i, 0)), st],
        out_shape=[jax.ShapeDtypeStruct((DEC_BATCH * DEC_SEQ, HGRN_W), BF16),
                   jax.ShapeDtypeStruct(s0.shape, F32)],
        compiler_params=_params("parallel"),
        name="hgrn_sample",
    )(lbl, on, hg, s0)


FFN_CHUNK = D_FF // 2


def _post_body(n_mix, *refs):
    mix_refs = refs[:n_mix]
    (h_ref, p_ref, wo_ref, g_mix, g_pre, wg_ref, wu_ref, wd_ref, g_post, wpp_ref, wpg_ref, g_ple, o_ref) = refs[n_mix:]
    y = None
    r0 = 0
    for m_ref in mix_refs:
        w = m_ref.shape[-1]
        part = _dot(m_ref[...].astype(BF16), wo_ref[r0:r0 + w, :])
        y = part if y is None else y + part
        r0 += w
    h = h_ref[...] + _rms(y, g_mix[...])
    u = _rms(h, g_pre[...]).astype(BF16)
    ff = None
    for c in range(0, D_FF, FFN_CHUNK):
        gate = _dot(u, wg_ref[:, c:c + FFN_CHUNK])
        up = _dot(u, wu_ref[:, c:c + FFN_CHUNK])
        part = _dot((_silu(gate) * up).astype(BF16), wd_ref[c:c + FFN_CHUNK, :])
        ff = part if ff is None else ff + part
    h = h + _rms(ff, g_post[...])
    e = _dot(p_ref[...].astype(BF16), wpp_ref[...]) * _sigmoid(_dot(h.astype(BF16), wpg_ref[...]))
    o_ref[...] = h + _rms(e, g_ple[...])


def _post(layer, mix, h, p, wo, g_mix, g_pre, wg, wu, wd, g_post, wpp, wpg, g_ple):
    t = h.shape[0]
    tm = TOKEN_TILE
    row = lambda n: pl.BlockSpec((tm, n), lambda i: (i, 0))
    gain = _resident((1, D_MODEL))
    of_layer = lambda w: pl.BlockSpec((None,) + w.shape[1:], lambda i: (layer, 0, 0), pipeline_mode=pl.Buffered(1))
    return pl.pallas_call(
        functools.partial(_post_body, len(mix)),
        grid=(t // tm,),
        in_specs=[row(m.shape[-1]) for m in mix] + [
            row(D_MODEL), pl.BlockSpec((None, tm, PLE_DIM), lambda i: (layer, i, 0)), _resident(wo.shape), gain, gain,
            of_layer(wg), of_layer(wu), of_layer(wd), gain, of_layer(wpp), of_layer(wpg), gain],
        out_specs=row(D_MODEL),
        out_shape=jax.ShapeDtypeStruct((t, D_MODEL), F32),
        compiler_params=_params("parallel"),
        name="post",
    )(*mix, h, p, wo, g_mix, g_pre, wg, wu, wd, g_post, wpp, wpg, g_ple)


def _rope_tables(pos, invf, sgn):
    ang = pos * invf
    return jnp.cos(ang), jnp.sin(ang) * sgn


def _mla_latents(h_ref, g_pre, wdq_ref, g_q, wkv_ref, g_kv, wpe_ref, wpes_ref, cos32, sin32):
    u = _rms(h_ref[...], g_pre[...]).astype(BF16)
    cqn = _rms(_dot(u, wdq_ref[...]), g_q[...]).astype(BF16)
    ckv = _rms(_dot(u, wkv_ref[...]), g_kv[...])
    kpe = _dot(u, wpe_ref[...]) * cos32 + _dot(u, wpes_ref[...]) * sin32
    return cqn, ckv, kpe


def _mla_pre_prompt_body(h_ref, g_pre, wdq_ref, g_q, wq_ref, wkvp_ref, g_kv,
                         wuk_ref, wuvt_ref, ones_ref, invf_ref, sgn_ref,
                         q_ref, k_ref, vt_ref, ckv_ref, kpe_ref):
    tm = TOKEN_TILE
    i = pl.program_id(0)
    pos = ((i * tm + _iota((tm, 1), 0)) & (SEQ - 1)).astype(F32)
    cos128, sin128 = _rope_tables(pos, invf_ref[...], sgn_ref[...])
    rope = slice(MLA_NOPE_DIM, MLA_NOPE_DIM + MLA_ROPE_DIM)
    half = MLA_ROPE_DIM // 2
    first_half = _iota((tm, MLA_HEAD_PAD), 1) < MLA_NOPE_DIM + half

    def swap_rope_halves(x):
        return jnp.where(first_half, pltpu.roll(x, MLA_HEAD_PAD - half, 1), pltpu.roll(x, half, 1))

    u = _rms(h_ref[...], g_pre[...]).astype(BF16)
    cqn = _rms(_dot(u, wdq_ref[...]), g_q[...]).astype(BF16)
    kvp = _dot(u, wkvp_ref[...])
    ckv = _rms(kvp[:, :MLA_KV_RANK], g_kv[...])
    kpe_raw = kvp[:, MLA_KV_RANK:]
    kpe_pad = kpe_raw * cos128 + swap_rope_halves(kpe_raw) * sin128
    ckv_ref[...] = ckv
    kpe_ref[...] = kpe_pad[:, rope]
    ckv_b = ckv.astype(BF16)
    vt_ref[0, 0] = (_dot_nt(wuvt_ref[...], ckv_b) + ones_ref[...]).astype(BF16)
    for h in range(MLA_HEADS):
        hs = slice(h * MLA_HEAD_PAD, (h + 1) * MLA_HEAD_PAD)
        q = _dot(cqn, wq_ref[:, hs])
        q = q * cos128 + swap_rope_halves(q) * sin128
        q_ref[0, h] = (q * (MLA_SCALE * LOG2_E)).astype(BF16)
        k_ref[0, h] = (_dot(ckv_b, wuk_ref[:, hs]) + kpe_pad).astype(BF16)


def _mla_pre_prompt(h, g_pre, wdq, g_q, wq, wkvp, g_kv, wuk, wuvt, ones_col, invf, sgn):
    tm = TOKEN_TILE
    assert tm == MLA_TK
    per_seq = SEQ // tm
    heads = pl.BlockSpec((1, MLA_HEADS, tm, MLA_HEAD_PAD), lambda i: (i // per_seq, 0, i % per_seq, 0))
    head_shape = jax.ShapeDtypeStruct((BATCH, MLA_HEADS, SEQ, MLA_HEAD_PAD), BF16)
    vt_rows = MLA_HEADS * MLA_VT_ROWS
    row = lambda n: pl.BlockSpec((tm, n), lambda i: (i, 0))
    consts = [g_pre, wdq, g_q, wq, wkvp, g_kv, wuk, wuvt, ones_col, invf, sgn]
    return pl.pallas_call(
        _mla_pre_prompt_body,
        grid=(h.shape[0] // tm,),
        in_specs=[row(D_MODEL)] + [_resident(c.shape) for c in consts],
        out_specs=[heads, heads,
                   pl.BlockSpec((1, 1, vt_rows, tm), lambda i: (i // per_seq, i % per_seq, 0, 0)),
                   row(MLA_KV_RANK), row(MLA_ROPE_DIM)],
        out_shape=[head_shape, head_shape,
                   jax.ShapeDtypeStruct((BATCH, per_seq, vt_rows, tm), BF16),
                   jax.ShapeDtypeStruct((h.shape[0], MLA_KV_RANK), F32),
                   jax.ShapeDtypeStruct((h.shape[0], MLA_ROPE_DIM), F32)],
        compiler_params=_params("parallel"),
        name="mla_pre_prompt",
    )(h, *consts)


def _mla_prompt_attn_body(q_ref, k_ref, vt_ref, o_ref, m_ref, acc_ref):
    i = pl.program_id(2)
    m_ref[...] = jnp.full_like(m_ref, NEG)
    acc_ref[...] = jnp.zeros_like(acc_ref)
    causal = _iota((MLA_TK, MLA_TK), 0) <= _iota((MLA_TK, MLA_TK), 1)
    splits = MLA_TQ // MLA_TK

    def kv_tile(j, units):
        k0 = pl.multiple_of(j * MLA_TK, MLA_TK)
        cols = lambda blk: slice(blk * MLA_TK, (blk + 1) * MLA_TK)

        def scores(hh, blk, masked):
            s = _dot_nt(k_ref[0, hh, pl.ds(k0, MLA_TK), :], q_ref[0, hh, cols(blk), :])
            return jnp.where(causal, s, NEG) if masked else s

        def softmax(hh, blk, s):
            m_prev = m_ref[hh, :, cols(blk)]
            m_new = jnp.maximum(m_prev, jnp.max(s, axis=0, keepdims=True))
            m_ref[hh, :, cols(blk)] = m_new
            return jnp.exp2(s - m_new).astype(BF16), jnp.exp2(m_prev - m_new)

        def values(hh, blk, p, alpha):
            vt = vt_ref[0, j, hh * MLA_VT_ROWS:(hh + 1) * MLA_VT_ROWS, :]
            acc_ref[hh, :, cols(blk)] = alpha * acc_ref[hh, :, cols(blk)] + _dot(vt, p)

        n = len(units)
        s_of, p_of = {}, {}
        for step in range(n + 2):
            if step < n:
                s_of[step] = scores(*units[step])
            if 0 <= step - 1 < n:
                hh, blk, _ = units[step - 1]
                p_of[step - 1] = softmax(hh, blk, s_of.pop(step - 1))
            if 0 <= step - 2 < n:
                hh, blk, _ = units[step - 2]
                values(hh, blk, *p_of.pop(step - 2))

    heads = range(MLA_HEAD_GROUP)

    def below_diagonal(j, carry):
        kv_tile(j, [(hh, blk, False) for hh in heads for blk in range(splits)])
        return carry

    lax.fori_loop(0, i * splits, below_diagonal, 0)
    for d in range(splits):
        kv_tile(i * splits + d, [(hh, blk, blk == d) for hh in heads for blk in range(d, splits)])
    for pair in range(MLA_HEAD_GROUP // 2):
        heads = []
        for hh in (2 * pair, 2 * pair + 1):
            acc = acc_ref[hh]
            heads.append(acc[:MLA_V_DIM] / acc[MLA_V_DIM:MLA_V_DIM + 1])
        o_t = jnp.concatenate(heads, axis=0)
        o_ref[0, :, pair * 2 * MLA_V_DIM:(pair + 1) * 2 * MLA_V_DIM] = o_t.T.astype(BF16)


def _mla_prompt_attn(q, k, vt):
    hg = MLA_HEAD_GROUP
    assert MLA_TQ % MLA_TK == 0 and hg % 2 == 0
    return pl.pallas_call(
        _mla_prompt_attn_body,
        grid=(BATCH, MLA_HEADS // hg, SEQ // MLA_TQ),
        in_specs=[pl.BlockSpec((1, hg, MLA_TQ, MLA_HEAD_PAD), lambda b, g, i: (b, g, i, 0)),
                  pl.BlockSpec((1, hg, SEQ, MLA_HEAD_PAD), lambda b, g, i: (b, g, 0, 0),
                               pipeline_mode=pl.Buffered(1)),
                  pl.BlockSpec((1, SEQ // MLA_TK, hg * MLA_VT_ROWS, MLA_TK), lambda b, g, i: (b, 0, g, 0),
                               pipeline_mode=pl.Buffered(1))],
        out_specs=pl.BlockSpec((1, MLA_TQ, hg * MLA_V_DIM), lambda b, g, i: (b, i, g)),
        out_shape=jax.ShapeDtypeStruct((BATCH, SEQ, MLA_HEADS * MLA_V_DIM), BF16),
        scratch_shapes=[pltpu.VMEM((hg, 1, MLA_TQ), F32), pltpu.VMEM((hg, MLA_VT_ROWS, MLA_TQ), F32)],
        compiler_params=_params("parallel", "parallel", "arbitrary"),
        name="mla_prompt_attn",
    )(q, k, vt)


def _mla_pre_sample_body(h_ref, g_pre, wdq_ref, g_q, wn_ref, wr_ref, wrs_ref, wukt_ref, wkv_ref, g_kv,
                         wpe_ref, wpes_ref, invf_ref, sgn_ref,
                         ql_ref, qp_ref, ckv_ref, kpe_ref):
    tm = TOKEN_TILE
    pos = (PAST_LEN + (_iota((tm, 1), 0) & (DEC_SEQ - 1))).astype(F32)
    cos32, sin32 = _rope_tables(pos, invf_ref[...], sgn_ref[...])
    cqn, ckv, kpe = _mla_latents(h_ref, g_pre, wdq_ref, g_q, wkv_ref, g_kv, wpe_ref, wpes_ref, cos32, sin32)
    ckv_ref[...] = ckv
    kpe_ref[...] = kpe
    n_seq = tm // DEC_SEQ
    for h in range(MLA_HEADS):
        rows = slice(h * DEC_SEQ, (h + 1) * DEC_SEQ)
        q_nope = _dot(cqn, wn_ref[h]).astype(BF16)
        q_lat = _dot(q_nope, wukt_ref[h]) * (MLA_SCALE * LOG2_E)
        ql_ref[:, rows, :] = q_lat.reshape(n_seq, DEC_SEQ, MLA_KV_RANK)
        q_pe = (_dot(cqn, wr_ref[h]) * cos32 + _dot(cqn, wrs_ref[h]) * sin32) * (MLA_SCALE * LOG2_E)
        qp_ref[:, rows, :] = q_pe.reshape(n_seq, DEC_SEQ, MLA_ROPE_DIM)


def _mla_pre_sample(h, g_pre, wdq, g_q, wn, wr, wrs, wukt, wkv, g_kv, wpe, wpes, invf, sgn):
    tm = TOKEN_TILE
    n_seq = tm // DEC_SEQ
    rows = MLA_HEADS * DEC_SEQ
    row = lambda n: pl.BlockSpec((tm, n), lambda i: (i, 0))
    consts = [g_pre, wdq, g_q, wn, wr, wrs, wukt, wkv, g_kv, wpe, wpes, invf, sgn]
    return pl.pallas_call(
        _mla_pre_sample_body,
        grid=(h.shape[0] // tm,),
        in_specs=[row(D_MODEL)] + [_resident(c.shape) for c in consts],
        out_specs=[pl.BlockSpec((n_seq, rows, MLA_KV_RANK), lambda i: (i, 0, 0)),
                   pl.BlockSpec((n_seq, rows, MLA_ROPE_DIM), lambda i: (i, 0, 0)),
                   row(MLA_KV_RANK), row(MLA_ROPE_DIM)],
        out_shape=[jax.ShapeDtypeStruct((DEC_BATCH, rows, MLA_KV_RANK), F32),
                   jax.ShapeDtypeStruct((DEC_BATCH, rows, MLA_ROPE_DIM), F32),
                   jax.ShapeDtypeStruct((h.shape[0], MLA_KV_RANK), F32),
                   jax.ShapeDtypeStruct((h.shape[0], MLA_ROPE_DIM), F32)],
        compiler_params=_params("parallel"),
        name="mla_pre_sample",
    )(h, *consts)


def _mla_sample_attn_body(pt_ref, ql_ref, qp_ref, cn_ref, pn_ref, ckv_hbm, kpe_hbm, o_ref,
                          m_ref, l_ref, acc_ref, cnp_ref, pnp_ref, ck_buf, kp_buf, ck_sem, kp_sem):
    npg = PAGES_PER_GROUP
    groups_per_seq = N_PAGES // npg
    total_groups = DEC_BATCH * groups_per_seq
    b = pl.program_id(0)
    rows = MLA_HEADS * DEC_SEQ

    def page_copies(page_of, slot):
        copies = []
        for j in range(npg):
            page = page_of(j)
            copies.append(pltpu.make_async_copy(
                ckv_hbm.at[0, page], ck_buf.at[slot, pl.ds(j * PAGE_SIZE, PAGE_SIZE), :], ck_sem.at[slot]))
            copies.append(pltpu.make_async_copy(kpe_hbm.at[0, page], kp_buf.at[slot, j], kp_sem.at[slot]))
        return copies

    def start_group(t, slot):
        for c in page_copies(lambda j: pt_ref[t * npg + j], slot):
            c.start()

    def wait_group(slot):
        for c in page_copies(lambda j: 0, slot):
            c.wait()

    @pl.when(b == 0)
    def _():
        for t0 in range(PAGE_RING):
            start_group(t0, t0)

    m_ref[...] = jnp.full_like(m_ref, NEG)
    l_ref[...] = jnp.zeros_like(l_ref)
    acc_ref[...] = jnp.zeros_like(acc_ref)
    ql = ql_ref[...].astype(BF16)
    qp = qp_ref[...].astype(BF16)

    def update(s, keys):
        m_prev = m_ref[...]
        m_new = jnp.maximum(m_prev, jnp.max(s, axis=-1, keepdims=True))
        alpha = jnp.exp2(m_prev - m_new)
        p = jnp.exp2(s - m_new)
        l_ref[...] = alpha * l_ref[...] + jnp.sum(p, axis=-1, keepdims=True)
        acc_ref[...] = alpha * acc_ref[...] + _dot(p.astype(BF16), keys)
        m_ref[...] = m_new

    def group(g, carry):
        t = b * groups_per_seq + g
        slot = t % PAGE_RING
        wait_group(slot)
        def scores(c):
            pages = range(c * PAGES_PER_CHUNK, (c + 1) * PAGES_PER_CHUNK)
            kl = ck_buf[slot, pages[0] * PAGE_SIZE:(pages[-1] + 1) * PAGE_SIZE, :].astype(BF16)
            kpt = jnp.concatenate([kp_buf[slot, j].astype(BF16) for j in pages], axis=1)
            return _dot_nt(ql, kl) + _dot(qp, kpt), kl

        n_chunks = npg // PAGES_PER_CHUNK
        nxt = scores(0)
        for c in range(n_chunks):
            cur = nxt
            if c + 1 < n_chunks:
                nxt = scores(c + 1)
            update(*cur)

        @pl.when(t + PAGE_RING < total_groups)
        def _():
            start_group(t + PAGE_RING, slot)

        return carry

    lax.fori_loop(0, groups_per_seq, group, 0)

    cnp_ref[...] = jnp.zeros_like(cnp_ref)
    pnp_ref[...] = jnp.zeros_like(pnp_ref)
    cnp_ref[:DEC_SEQ, :] = cn_ref[...]
    pnp_ref[:DEC_SEQ, :] = pn_ref[...]
    cn = cnp_ref[...].astype(BF16)
    t_q = _iota((rows, PAGE_SIZE), 0) & (DEC_SEQ - 1)
    s_new = jnp.where(_iota((rows, PAGE_SIZE), 1) <= t_q,
                      _dot_nt(ql, cn) + _dot_nt(qp, pnp_ref[...].astype(BF16)), NEG)
    update(s_new, cn)
    o_ref[...] = (acc_ref[...] / l_ref[...]).astype(BF16)


def _mla_sample_attn(page_table, ql, qp, ckv_new, kpe_new, ckv_pool, kpe_pool):
    npg = PAGES_PER_GROUP
    rows = MLA_HEADS * DEC_SEQ
    assert N_PAGES % npg == 0 and npg % PAGES_PER_CHUNK == 0
    seq = lambda r, w: pl.BlockSpec((None, r, w), lambda b, pt: (b, 0, 0))
    hbm = pl.BlockSpec(memory_space=pl.ANY)
    grid_spec = pltpu.PrefetchScalarGridSpec(
        num_scalar_prefetch=1,
        grid=(DEC_BATCH,),
        in_specs=[seq(rows, MLA_KV_RANK), seq(rows, MLA_ROPE_DIM), seq(DEC_SEQ, MLA_KV_RANK),
                  seq(DEC_SEQ, MLA_ROPE_DIM), hbm, hbm],
        out_specs=seq(rows, MLA_KV_RANK),
        scratch_shapes=[pltpu.VMEM((rows, 1), F32), pltpu.VMEM((rows, 1), F32), pltpu.VMEM((rows, MLA_KV_RANK), F32),
                        pltpu.VMEM((PAGE_SIZE, MLA_KV_RANK), F32), pltpu.VMEM((PAGE_SIZE, MLA_ROPE_DIM), F32),
                        pltpu.VMEM((PAGE_RING, npg * PAGE_SIZE, MLA_KV_RANK), F32),
                        pltpu.VMEM((PAGE_RING, npg, MLA_ROPE_DIM, PAGE_SIZE), F32),
                        pltpu.SemaphoreType.DMA((PAGE_RING,)), pltpu.SemaphoreType.DMA((PAGE_RING,))],
    )
    return pl.pallas_call(
        _mla_sample_attn_body,
        grid_spec=grid_spec,
        out_shape=jax.ShapeDtypeStruct((DEC_BATCH, rows, MLA_KV_RANK), BF16),
        compiler_params=_params("arbitrary"),
        name="mla_sample_attn",
    )(page_table, ql, qp, ckv_new, kpe_new, ckv_pool, kpe_pool)


def _mla_sample_out_body(x_ref, w_ref, o_ref):
    x = x_ref[...].reshape(DEC_BATCH * 2 * DEC_SEQ, MLA_KV_RANK)
    y = _dot(x, w_ref[0]).reshape(DEC_BATCH, 2 * DEC_SEQ, 2 * MLA_V_DIM)
    first_head = _iota((DEC_BATCH, DEC_SEQ, 2 * MLA_V_DIM), 2) < MLA_V_DIM
    o = jnp.where(first_head, y[:, :DEC_SEQ], y[:, DEC_SEQ:])
    o_ref[...] = o.reshape(DEC_BATCH * DEC_SEQ, 2 * MLA_V_DIM)


def _mla_sample_out(o_lat, wuv_pairs):
    return pl.pallas_call(
        _mla_sample_out_body,
        grid=(MLA_HEADS // 2,),
        in_specs=[pl.BlockSpec((DEC_BATCH, 2 * DEC_SEQ, MLA_KV_RANK), lambda j: (0, j, 0)),
                  pl.BlockSpec((1, MLA_KV_RANK, 2 * MLA_V_DIM), lambda j: (j, 0, 0))],
        out_specs=pl.BlockSpec((DEC_BATCH * DEC_SEQ, 2 * MLA_V_DIM), lambda j: (0, j)),
        out_shape=jax.ShapeDtypeStruct((DEC_BATCH * DEC_SEQ, MLA_HEADS * MLA_V_DIM), F32),
        compiler_params=_params("parallel"),
        name="mla_sample_out",
    )(o_lat, wuv_pairs)


def _swap_halves(w):
    half = w.shape[-1] // 2
    return jnp.concatenate([w[..., half:], w[..., :half]], axis=-1)


def _rope_consts():
    half = MLA_ROPE_DIM // 2
    inv_freq = ROPE_THETA ** (-jnp.arange(0, half, dtype=F32) * 2.0 / MLA_ROPE_DIM)
    invf32 = jnp.concatenate([inv_freq, inv_freq])[None]
    sgn32 = jnp.concatenate([-jnp.ones((half,), F32), jnp.ones((half,), F32)])[None]
    pad = lambda a: jnp.pad(a, ((0, 0), (MLA_NOPE_DIM, MLA_HEAD_PAD - MLA_NOPE_DIM - MLA_ROPE_DIM)))
    return invf32, sgn32, pad(invf32), pad(sgn32)


def _mla_weights(w_uq, w_dkv, w_uk, w_uv):
    wq = w_uq.reshape(MLA_Q_RANK, MLA_HEADS, MLA_NOPE_DIM + MLA_ROPE_DIM)
    wq_nope, wq_rope = wq[..., :MLA_NOPE_DIM], wq[..., MLA_NOPE_DIM:]
    tail = MLA_HEAD_PAD - MLA_NOPE_DIM - MLA_ROPE_DIM
    pad_last = lambda a, lo, hi: jnp.pad(a, [(0, 0)] * (a.ndim - 1) + [(lo, hi)])
    flat = lambda a: a.reshape(a.shape[0], MLA_HEADS * MLA_HEAD_PAD).astype(BF16)
    w = {}
    w["wq"] = flat(pad_last(wq, 0, tail))
    w["wkvp"] = jnp.concatenate([w_dkv[:, :MLA_KV_RANK], pad_last(w_dkv[:, MLA_KV_RANK:], MLA_NOPE_DIM, tail)],
                                axis=1).astype(BF16)
    w["wuk"] = flat(pad_last(w_uk, 0, MLA_HEAD_PAD - MLA_NOPE_DIM))
    wuvt = jnp.transpose(w_uv, (1, 2, 0))
    w["wuvt"] = jnp.pad(wuvt, ((0, 0), (0, MLA_VT_ROWS - MLA_V_DIM), (0, 0))).reshape(
        MLA_HEADS * MLA_VT_ROWS, MLA_KV_RANK).astype(BF16)
    w["ones_col"] = jnp.tile((jnp.arange(MLA_VT_ROWS) == MLA_V_DIM).astype(F32), MLA_HEADS)[:, None]
    w["wn"] = jnp.moveaxis(wq_nope, 1, 0).astype(BF16)
    w["wr"] = jnp.moveaxis(wq_rope, 1, 0).astype(BF16)
    w["wrs"] = jnp.moveaxis(_swap_halves(wq_rope), 1, 0).astype(BF16)
    w["wukt"] = jnp.transpose(w_uk, (1, 2, 0)).astype(BF16)
    w["wkv"] = w_dkv[:, :MLA_KV_RANK].astype(BF16)
    w["wpe"] = w_dkv[:, MLA_KV_RANK:].astype(BF16)
    w["wpes"] = _swap_halves(w_dkv[:, MLA_KV_RANK:]).astype(BF16)
    w["wuv_pairs"] = jnp.transpose(w_uv.reshape(MLA_KV_RANK, MLA_HEADS // 2, 2 * MLA_V_DIM),
                                   (1, 0, 2)).astype(BF16)
    return w


def kernel(x_prompt, x_sample, cache_swa_k, cache_swa_v, state_hgrn, cache_mla_ckv, cache_mla_kpe, page_table, p_prompt, p_sample, ln_mix_pre, ln_mix_post, ln_ffn_pre, ln_ffn_post, ln_ple, w_ab_in, w_ab_out, swa_sinks, hgrn_lb_logits, hgrn_out_norm, w_mla_dq, mla_q_norm, w_mla_uq, w_mla_dkv, mla_kv_norm, w_mla_uk, w_mla_uv, w_mla_o, w_ffn_gate, w_ffn_up, w_ffn_down, w_ple_proj, w_ple_gate):
    tp = BATCH * SEQ
    ts = DEC_BATCH * DEC_SEQ
    bf = lambda a: a.astype(BF16)
    gain = lambda a, i: a[i][None].astype(F32)

    ffn_w = [bf(w) for w in (w_ffn_gate, w_ffn_up, w_ffn_down)]
    ple_w = [bf(w) for w in (w_ple_proj, w_ple_gate)]

    def finish(mix, h, p, i, wo):
        return _post(i, mix, h, p, wo, gain(ln_mix_post, i), gain(ln_ffn_pre, i), *ffn_w, gain(ln_ffn_post, i),
                     *ple_w, gain(ln_ple, i))

    hp = x_prompt.reshape(tp, D_MODEL)
    hs = x_sample.reshape(ts, D_MODEL)
    pp = p_prompt.reshape(2, tp, PLE_DIM)
    ps = p_sample.reshape(2, ts, PLE_DIM)

    w_in = bf(w_ab_in[0])
    w_out = bf(w_ab_out[0])
    g0 = gain(ln_mix_pre, 0)
    sinks = swa_sinks[0].astype(F32)
    lbl = hgrn_lb_logits.astype(F32)
    on = hgrn_out_norm[0][None].astype(F32)

    qa, ka, va, hg = _ab_in(hp, g0, w_in, BF16)
    oa = _swa_prompt(sinks, qa.reshape(BATCH, SEQ, SWA_Q_W), ka.reshape(BATCH, SEQ, SWA_KV_W),
                     va.reshape(BATCH, SEQ, SWA_KV_W))
    ob, hgrn_p = _hgrn_prompt(lbl, on, hg.reshape(BATCH, SEQ, 4 * HGRN_W))
    swa_k_p = ka.reshape(BATCH, SEQ, SWA_KV_HEADS, SWA_HEAD_DIM)[:, -WINDOW:]
    swa_v_p = va.reshape(BATCH, SEQ, SWA_KV_HEADS, SWA_HEAD_DIM)[:, -WINDOW:]
    hp = finish([oa.reshape(tp, SWA_Q_W), ob.reshape(tp, HGRN_W)], hp, pp, 0, w_out)

    qa, ka, va, hg = _ab_in(hs, g0, w_in, F32)
    oa, swa_k_s, swa_v_s = _swa_sample(
        sinks, qa, ka.reshape(DEC_BATCH, DEC_SEQ, SWA_KV_W),
        va.reshape(DEC_BATCH, DEC_SEQ, SWA_KV_W), cache_swa_k[0].reshape(DEC_BATCH, WINDOW, SWA_KV_W),
        cache_swa_v[0].reshape(DEC_BATCH, WINDOW, SWA_KV_W))
    ob, hgrn_s = _hgrn_sample(lbl, on, hg, state_hgrn[0])
    hs = finish([oa.reshape(ts, SWA_Q_W), ob], hs, ps, 0, w_out)

    mw = _mla_weights(w_mla_uq[0], w_mla_dkv[0], w_mla_uk[0], w_mla_uv[0])
    invf32, sgn32, invf128, sgn128 = _rope_consts()
    g1 = gain(ln_mix_pre, 1)
    wdq = bf(w_mla_dq[0])
    g_q = mla_q_norm[0][None].astype(F32)
    g_kv = mla_kv_norm[0][None].astype(F32)
    w_o = bf(w_mla_o[0])

    q, k, vt, ckv_p, kpe_p = _mla_pre_prompt(hp, g1, wdq, g_q, mw["wq"], mw["wkvp"], g_kv, mw["wuk"], mw["wuvt"],
                                             mw["ones_col"], invf128, sgn128)
    o = _mla_prompt_attn(q, k, vt)
    hp = finish([o.reshape(tp, MLA_HEADS * MLA_V_DIM)], hp, pp, 1, w_o)

    ql, qp, ckv_s, kpe_s = _mla_pre_sample(hs, g1, wdq, g_q, mw["wn"], mw["wr"], mw["wrs"], mw["wukt"], mw["wkv"],
                                           g_kv, mw["wpe"], mw["wpes"], invf32, sgn32)
    o_lat = _mla_sample_attn(page_table.reshape(-1), ql, qp, ckv_s.reshape(DEC_BATCH, DEC_SEQ, MLA_KV_RANK),
                             kpe_s.reshape(DEC_BATCH, DEC_SEQ, MLA_ROPE_DIM), cache_mla_ckv,
                             jnp.swapaxes(cache_mla_kpe, 2, 3))
    o = _mla_sample_out(o_lat, mw["wuv_pairs"])
    hs = finish([o], hs, ps, 1, w_o)

    kv5 = lambda a, n: a.reshape(1, n, WINDOW, SWA_KV_HEADS, SWA_HEAD_DIM)
    return (hp.reshape(BATCH, SEQ, D_MODEL), hs.reshape(DEC_BATCH, DEC_SEQ, D_MODEL),
            swa_k_p[None], swa_v_p[None], hgrn_p[None],
            ckv_p.reshape(1, BATCH, SEQ, MLA_KV_RANK), kpe_p.reshape(1, BATCH, SEQ, MLA_ROPE_DIM),
            kv5(swa_k_s, DEC_BATCH), kv5(swa_v_s, DEC_BATCH), hgrn_s[None],
            ckv_s.reshape(1, DEC_BATCH, DEC_SEQ, MLA_KV_RANK), kpe_s.reshape(1, DEC_BATCH, DEC_SEQ, MLA_ROPE_DIM))
```

```python
import functools

import jax
import jax.numpy as jnp
from jax import lax
from jax.experimental import pallas as pl
from jax.experimental.pallas import tpu as pltpu

F32 = jnp.float32
BF16 = jnp.bfloat16

D_MODEL = 1024
BATCH = 2
SEQ = 8192
DEC_BATCH = 128
DEC_SEQ = 8
PAST_LEN = 16384
PAGE_SIZE = 128
N_PAGES = PAST_LEN // PAGE_SIZE

SWA_HEADS = 8
SWA_KV_HEADS = 2
SWA_GROUP = SWA_HEADS // SWA_KV_HEADS
SWA_HEAD_DIM = 64
WINDOW = 128
SWA_SCALE = SWA_HEAD_DIM ** -0.5
SWA_Q_W = SWA_HEADS * SWA_HEAD_DIM
SWA_KV_W = SWA_KV_HEADS * SWA_HEAD_DIM

HGRN_HEADS = 4
HGRN_HEAD_DIM = 128
HGRN_CHUNK = 32
HGRN_W = HGRN_HEADS * HGRN_HEAD_DIM
AB_IN_W = SWA_Q_W + 2 * SWA_KV_W + 4 * HGRN_W

MLA_HEADS = 16
MLA_NOPE_DIM = 64
MLA_ROPE_DIM = 32
MLA_V_DIM = 64
MLA_Q_RANK = 512
MLA_KV_RANK = 256
MLA_SCALE = (MLA_NOPE_DIM + MLA_ROPE_DIM) ** -0.5
ROPE_THETA = 10000.0
MLA_HEAD_PAD = 128

D_FF = 2816
PLE_DIM = 256
NORM_EPS = 1e-6
NEG = -1e30
LOG2_E = 1.4426950408889634

VMEM_LIMIT_BYTES = 56 * 1024 * 1024

TOKEN_TILE = 512
HGRN_TILE = 256
SWA_TQ = 256
SAMPLE_BATCH_TILE = 16
MLA_TQ = 1024
MLA_TK = 512
MLA_HEAD_GROUP = 8
MLA_VT_ROWS = 80
PAGES_PER_GROUP = 32
PAGES_PER_CHUNK = 16
SAMPLE_STREAMS = 2
PAGE_RING = 3


def _dot(a, b):
    return jnp.dot(a, b, preferred_element_type=F32)


def _dot_nt(a, b):
    return lax.dot_general(a, b, (((1,), (1,)), ((), ())), preferred_element_type=F32)


def _rms(x, g):
    return x * lax.rsqrt(jnp.mean(x * x, axis=-1, keepdims=True) + NORM_EPS) * g


def _sigmoid(x):
    return 1.0 / (1.0 + jnp.exp(-x))


def _silu(x):
    return x * _sigmoid(x)


def _iota(shape, dim):
    return lax.broadcasted_iota(jnp.int32, shape, dim)


def _params(*sem):
    return pltpu.CompilerParams(dimension_semantics=sem, vmem_limit_bytes=VMEM_LIMIT_BYTES)


def _resident(shape):
    nd = len(shape)
    return pl.BlockSpec(shape, lambda *_: (0,) * nd, pipeline_mode=pl.Buffered(1))


def _smem():
    return pl.BlockSpec(memory_space=pltpu.SMEM)


def _ab_in_body(h_ref, g_ref, w_ref, qa_ref, k_ref, v_ref, hg_ref):
    u = _rms(h_ref[...], g_ref[...]).astype(BF16)
    qa_ref[...] = (_dot(u, w_ref[:, :SWA_Q_W]) * SWA_SCALE).astype(qa_ref.dtype)
    k_ref[...] = _dot(u, w_ref[:, SWA_Q_W:SWA_Q_W + SWA_KV_W])
    v_ref[...] = _dot(u, w_ref[:, SWA_Q_W + SWA_KV_W:SWA_Q_W + 2 * SWA_KV_W])
    hg_ref[...] = _dot(u, w_ref[:, SWA_Q_W + 2 * SWA_KV_W:])


def _ab_in(h, g, w, q_dtype):
    t = h.shape[0]
    tm = TOKEN_TILE
    row = lambda n: pl.BlockSpec((tm, n), lambda i: (i, 0))
    return pl.pallas_call(
        _ab_in_body,
        grid=(t // tm,),
        in_specs=[row(D_MODEL), _resident((1, D_MODEL)), _resident((D_MODEL, AB_IN_W))],
        out_specs=[row(SWA_Q_W), row(SWA_KV_W), row(SWA_KV_W), row(4 * HGRN_W)],
        out_shape=[jax.ShapeDtypeStruct((t, SWA_Q_W), q_dtype),
                   jax.ShapeDtypeStruct((t, SWA_KV_W), F32),
                   jax.ShapeDtypeStruct((t, SWA_KV_W), F32),
                   jax.ShapeDtypeStruct((t, 4 * HGRN_W), F32)],
        compiler_params=_params("parallel"),
        name="ab_in",
    )(h, g, w)


def _swa_head(q, keys, vals, dists, valids, slope, sink):
    scores = []
    for k, dist, valid in zip(keys, dists, valids):
        s = _dot_nt(q, k) - slope * dist
        scores.append(jnp.where(valid, s, NEG))
    m = sink
    for s in scores:
        m = jnp.maximum(m, jnp.max(s, axis=-1, keepdims=True))
    den = jnp.exp(sink - m)
    acc = None
    for s, v in zip(scores, vals):
        p = jnp.exp(s - m)
        den = den + jnp.sum(p, axis=-1, keepdims=True)
        pv = _dot(p.astype(BF16), v)
        acc = pv if acc is None else acc + pv
    return acc / den


def _swa_prompt_body(sink_ref, q_ref, kc_ref, kp_ref, vc_ref, vp_ref, o_ref):
    i = pl.program_id(1)
    nk = WINDOW + SWA_TQ
    kk = jnp.concatenate([kp_ref[0], kc_ref[0]], axis=0).astype(BF16)
    vv_t = jnp.concatenate([vp_ref[0], vc_ref[0]], axis=0).T.astype(BF16)
    key = _iota((nk, SWA_TQ), 0)
    qry = _iota((nk, SWA_TQ), 1)
    dist_i = qry + WINDOW - key
    valid = (dist_i >= 0) & (dist_i <= WINDOW) & (key >= jnp.where(i > 0, 0, WINDOW))
    dist = dist_i.astype(F32)
    q = q_ref[0]
    kv_lanes = lambda j: slice(j // SWA_GROUP * SWA_HEAD_DIM, (j // SWA_GROUP + 1) * SWA_HEAD_DIM)

    def scores(j):
        s = _dot_nt(kk[:, kv_lanes(j)], q[:, j * SWA_HEAD_DIM:(j + 1) * SWA_HEAD_DIM]) - 2.0 ** -(j + 1) * dist
        return jnp.where(valid, s, NEG)

    def attend(j, s):
        m = jnp.maximum(jnp.max(s, axis=0, keepdims=True), sink_ref[j])
        p = jnp.exp(s - m)
        den = jnp.sum(p, axis=0, keepdims=True) + jnp.exp(sink_ref[j] - m)
        return _dot(vv_t[kv_lanes(j), :], p.astype(BF16)) / den

    heads = []
    nxt = scores(0)
    for j in range(SWA_HEADS):
        cur = nxt
        if j + 1 < SWA_HEADS:
            nxt = scores(j + 1)
        heads.append(attend(j, cur))
    for pair in range(SWA_HEADS // 2):
        o_t = jnp.concatenate(heads[2 * pair:2 * pair + 2], axis=0)
        o_ref[0, :, pair * 2 * SWA_HEAD_DIM:(pair + 1) * 2 * SWA_HEAD_DIM] = o_t.T.astype(BF16)


def _swa_prompt(sinks, qa, k, v):
    per_tile = SWA_TQ // WINDOW
    cur = lambda w: pl.BlockSpec((1, SWA_TQ, w), lambda b, i: (b, i, 0))
    prev = lambda w: pl.BlockSpec((1, WINDOW, w), lambda b, i: (b, jnp.maximum(i * per_tile - 1, 0), 0))
    return pl.pallas_call(
        _swa_prompt_body,
        grid=(BATCH, SEQ // SWA_TQ),
        in_specs=[_smem(), cur(SWA_Q_W), cur(SWA_KV_W), prev(SWA_KV_W), cur(SWA_KV_W), prev(SWA_KV_W)],
        out_specs=cur(SWA_Q_W),
        out_shape=jax.ShapeDtypeStruct((BATCH, SEQ, SWA_Q_W), BF16),
        compiler_params=_params("parallel", "arbitrary"),
        name="swa_prompt",
    )(sinks, qa, k, k, v, v)


def _swa_sample_body(sink_ref, q_ref, kn_ref, vn_ref, kb_ref, vb_ref, o_ref, ko_ref, vo_ref):
    bb = SAMPLE_BATCH_TILE
    t_new = DEC_SEQ
    r = bb * t_new
    nk = bb * WINDOW
    kb3, vb3, kn3, vn3 = kb_ref[...], vb_ref[...], kn_ref[...], vn_ref[...]
    ko_ref[:, :WINDOW - t_new, :] = kb3[:, t_new:, :]
    ko_ref[:, WINDOW - t_new:, :] = kn3
    vo_ref[:, :WINDOW - t_new, :] = vb3[:, t_new:, :]
    vo_ref[:, WINDOW - t_new:, :] = vn3
    kb = kb3.reshape(nk, SWA_KV_W).astype(BF16)
    vb = vb3.reshape(nk, SWA_KV_W).astype(BF16)
    kn = kn3.reshape(r, SWA_KV_W).astype(BF16)
    vn = vn3.reshape(r, SWA_KV_W).astype(BF16)

    row = _iota((r, nk), 0)
    col = _iota((r, nk), 1)
    t_q = row & (t_new - 1)
    slot = col & (WINDOW - 1)
    dist_b = (t_q - slot + WINDOW).astype(F32)
    valid_b = ((row >> 3) == (col >> 7)) & (slot >= t_q)
    row_n = _iota((r, r), 0)
    col_n = _iota((r, r), 1)
    dist_n = ((row_n & (t_new - 1)) - (col_n & (t_new - 1))).astype(F32)
    valid_n = ((row_n >> 3) == (col_n >> 3)) & (col_n <= row_n)
    q = q_ref[...].astype(BF16)
    for hk in range(SWA_KV_HEADS):
        ks = slice(hk * SWA_HEAD_DIM, (hk + 1) * SWA_HEAD_DIM)
        for g in range(SWA_GROUP):
            j = hk * SWA_GROUP + g
            qs = slice(j * SWA_HEAD_DIM, (j + 1) * SWA_HEAD_DIM)
            o_ref[:, qs] = _swa_head(q[:, qs], (kb[:, ks], kn[:, ks]), (vb[:, ks], vn[:, ks]),
                                     (dist_b, dist_n), (valid_b, valid_n), 2.0 ** -(j + 1), sink_ref[j])


def _swa_sample(sinks, qa, kn, vn, kbuf, vbuf):
    bb = SAMPLE_BATCH_TILE
    assert DEC_SEQ == 8 and WINDOW == 128
    blk = lambda r, w: pl.BlockSpec((bb, r, w), lambda i: (i, 0, 0))
    rows = pl.BlockSpec((bb * DEC_SEQ, SWA_Q_W), lambda i: (i, 0))
    return pl.pallas_call(
        _swa_sample_body,
        grid=(DEC_BATCH // bb,),
        in_specs=[_smem(), rows, blk(DEC_SEQ, SWA_KV_W), blk(DEC_SEQ, SWA_KV_W),
                  blk(WINDOW, SWA_KV_W), blk(WINDOW, SWA_KV_W)],
        out_specs=[rows, blk(WINDOW, SWA_KV_W), blk(WINDOW, SWA_KV_W)],
        out_shape=[jax.ShapeDtypeStruct((DEC_BATCH * DEC_SEQ, SWA_Q_W), F32),
                   jax.ShapeDtypeStruct((DEC_BATCH, WINDOW, SWA_KV_W), F32),
                   jax.ShapeDtypeStruct((DEC_BATCH, WINDOW, SWA_KV_W), F32)],
        compiler_params=_params("parallel"),
        name="swa_sample",
    )(sinks, qa, kn, vn, kbuf, vbuf)


def _hgrn_lower_bound(lbl_ref, layer):
    logits = lbl_ref[...]
    e = jnp.exp(logits - jnp.max(logits, axis=0, keepdims=True))
    sm = e / jnp.sum(e, axis=0, keepdims=True)
    return jnp.sum(sm[:layer + 1], axis=0, keepdims=True)


def _split3(x):
    hi = x.astype(BF16)
    r = x - hi.astype(F32)
    mid = r.astype(BF16)
    lo = (r - mid.astype(F32)).astype(BF16)
    return hi, mid, lo


def _hgrn_chunk_terms(x, lb, chunk):
    r = x.shape[0]
    qr, fr, ir, gr = (x[:, i * HGRN_W:(i + 1) * HGRN_W] for i in range(4))
    f = lb + (1.0 - lb) * _sigmoid(fr)
    logf = jnp.log(f)
    row = _iota((r, r), 0)
    col = _iota((r, r), 1)
    chunk_start = row & ~(chunk - 1)
    same = (col >= chunk_start) & (col < chunk_start + chunk)
    tril = (col >= chunk_start) & (col <= row)
    parts = _split3(logf)
    tri_m = jnp.where(tril, 1.0, 0.0).astype(BF16)
    blk_m = jnp.where(same, 1.0, 0.0).astype(BF16)
    b = sum(_dot(tri_m, p) for p in parts)
    bl = sum(_dot(blk_m, p) for p in parts)
    k = 1.0 - f
    qd = _silu(qr) * jnp.exp(b)
    kd = k * jnp.exp(-b)
    k2 = k * jnp.exp(bl - b)
    return qd, kd, k2, ir, gr, bl, tril


def _hgrn_intra(qd, kd, v, tril):
    att = jnp.where(tril, _dot_nt(qd.astype(BF16), kd.astype(BF16)), 0.0)
    return _dot(att.astype(BF16), v.astype(BF16))


def _hgrn_finish(o, g, on):
    return (_rms(o, on) * _silu(g)).astype(BF16)


def _hgrn_prompt_body(lbl_ref, on_ref, x_ref, o_ref, s_ref, st_ref):
    t = pl.program_id(1)

    @pl.when(t == 0)
    def _():
        st_ref[...] = jnp.zeros_like(st_ref)

    lb = _hgrn_lower_bound(lbl_ref, 0)
    qd, kd, k2, v, g, bl, tril = _hgrn_chunk_terms(x_ref[0], lb, HGRN_CHUNK)
    rows = _iota((HGRN_TILE, HGRN_HEAD_DIM), 0)
    n_chunks = HGRN_TILE // HGRN_CHUNK
    head = [slice(h * HGRN_HEAD_DIM, (h + 1) * HGRN_HEAD_DIM) for h in range(HGRN_HEADS)]
    o_intra = [_hgrn_intra(qd[:, hs], kd[:, hs], v[:, hs], tril) for hs in head]
    qdb = [qd[:, hs].astype(BF16) for hs in head]
    vt = [v[:, hs].T.astype(BF16) for hs in head]
    st = [st_ref[h] for h in range(HGRN_HEADS)]
    o_inter = [[] for _ in head]
    for c in range(n_chunks):
        r0 = c * HGRN_CHUNK
        in_chunk = (rows >= r0) & (rows < r0 + HGRN_CHUNK)
        for h, hs in enumerate(head):
            o_inter[h].append(_dot_nt(qdb[h][r0:r0 + HGRN_CHUNK], st[h].astype(BF16)))
            d_st = _dot(vt[h], jnp.where(in_chunk, k2[:, hs], 0.0).astype(BF16))
            st[h] = st[h] * jnp.exp(bl[r0:r0 + 1, hs]) + d_st
    for h, hs in enumerate(head):
        st_ref[h] = st[h]
        o = o_intra[h] + jnp.concatenate(o_inter[h], axis=0)
        o_ref[0, :, hs] = _hgrn_finish(o, g[:, hs], on_ref[...])

    @pl.when(t == pl.num_programs(1) - 1)
    def _():
        for h in range(HGRN_HEADS):
            s_ref[0, h] = st_ref[h].T


def _hgrn_prompt(lbl, on, hg):
    tt = HGRN_TILE
    return pl.pallas_call(
        _hgrn_prompt_body,
        grid=(BATCH, SEQ // tt),
        in_specs=[_resident(lbl.shape), _resident((1, HGRN_HEAD_DIM)),
                  pl.BlockSpec((1, tt, 4 * HGRN_W), lambda b, t: (b, t, 0))],
        out_specs=[pl.BlockSpec((1, tt, HGRN_W), lambda b, t: (b, t, 0)),
                   pl.BlockSpec((1, HGRN_HEADS, HGRN_HEAD_DIM, HGRN_HEAD_DIM), lambda b, t: (b, 0, 0, 0))],
        out_shape=[jax.ShapeDtypeStruct((BATCH, SEQ, HGRN_W), BF16),
                   jax.ShapeDtypeStruct((BATCH, HGRN_HEADS, HGRN_HEAD_DIM, HGRN_HEAD_DIM), F32)],
        scratch_shapes=[pltpu.VMEM((HGRN_HEADS, HGRN_HEAD_DIM, HGRN_HEAD_DIM), F32)],
        compiler_params=_params("parallel", "arbitrary"),
        name="hgrn_prompt",
    )(lbl, on, hg)


def _hgrn_sample_body(lbl_ref, on_ref, x_ref, s0_ref, o_ref, s_ref):
    bb = SAMPLE_BATCH_TILE
    r = bb * DEC_SEQ
    lb = _hgrn_lower_bound(lbl_ref, 0)
    qd, kd, k2, v, g, bl, tril = _hgrn_chunk_terms(x_ref[...], lb, DEC_SEQ)
    cols = _iota((HGRN_HEAD_DIM, r), 1)
    for h in range(HGRN_HEADS):
        hs = slice(h * HGRN_HEAD_DIM, (h + 1) * HGRN_HEAD_DIM)
        o_intra = _hgrn_intra(qd[:, hs], kd[:, hs], v[:, hs], tril)
        qdb = qd[:, hs].astype(BF16)
        vb = v[:, hs].astype(BF16)
        k2t = k2[:, hs].T
        decay_t = jnp.exp(bl[:, hs].T)
        o_inter = []
        for b in range(bb):
            r0 = b * DEC_SEQ
            s0 = s0_ref[b, h]
            o_inter.append(_dot(qdb[r0:r0 + DEC_SEQ], s0.astype(BF16)))
            in_seq = (cols >= r0) & (cols < r0 + DEC_SEQ)
            d_s = _dot(jnp.where(in_seq, k2t, 0.0).astype(BF16), vb)
            s_ref[b, h] = s0 * decay_t[:, r0:r0 + 1] + d_s
        o = o_intra + jnp.concatenate(o_inter, axis=0)
        o_ref[:, hs] = _hgrn_finish(o, g[:, hs], on_ref[...])


def _hgrn_sample(lbl, on, hg, s0):
    bb = SAMPLE_BATCH_TILE
    r = bb * DEC_SEQ
    st = pl.BlockSpec((bb, HGRN_HEADS, HGRN_HEAD_DIM, HGRN_HEAD_DIM), lambda i: (i, 0, 0, 0))
    return pl.pallas_call(
        _hgrn_sample_body,
        grid=(DEC_BATCH // bb,),
        in_specs=[_resident(lbl.shape), _resident((1, HGRN_HEAD_DIM)),
                  pl.BlockSpec((r, 4 * HGRN_W), lambda i: (i, 0)), st],
        out_specs=[pl.BlockSpec((r, HGRN_W), lambda i: (i, 0)), st],
        out_shape=[jax.ShapeDtypeStruct((DEC_BATCH * DEC_SEQ, HGRN_W), BF16),
                   jax.ShapeDtypeStruct(s0.shape, F32)],
        compiler_params=_params("parallel"),
        name="hgrn_sample",
    )(lbl, on, hg, s0)


FFN_CHUNK = D_FF // 2


def _post_body(n_mix, *refs):
    mix_refs = refs[:n_mix]
    (h_ref, p_ref, wo_ref, g_mix, g_pre, wg_ref, wu_ref, wd_ref, g_post, wpp_ref, wpg_ref, g_ple, o_ref) = refs[n_mix:]
    y = None
    r0 = 0
    for m_ref in mix_refs:
        w = m_ref.shape[-1]
        part = _dot(m_ref[...].astype(BF16), wo_ref[r0:r0 + w, :])
        y = part if y is None else y + part
        r0 += w
    h = h_ref[...] + _rms(y, g_mix[...])
    u = _rms(h, g_pre[...]).astype(BF16)
    ff = None
    for c in range(0, D_FF, FFN_CHUNK):
        gate = _dot(u, wg_ref[:, c:c + FFN_CHUNK])
        up = _dot(u, wu_ref[:, c:c + FFN_CHUNK])
        part = _dot((_silu(gate) * up).astype(BF16), wd_ref[c:c + FFN_CHUNK, :])
        ff = part if ff is None else ff + part
    h = h + _rms(ff, g_post[...])
    e = _dot(p_ref[...].astype(BF16), wpp_ref[...]) * _sigmoid(_dot(h.astype(BF16), wpg_ref[...]))
    o_ref[...] = h + _rms(e, g_ple[...])


def _post(layer, mix, h, p, wo, g_mix, g_pre, wg, wu, wd, g_post, wpp, wpg, g_ple):
    t = h.shape[0]
    tm = TOKEN_TILE
    row = lambda n: pl.BlockSpec((tm, n), lambda i: (i, 0))
    gain = _resident((1, D_MODEL))
    of_layer = lambda w: pl.BlockSpec((None,) + w.shape[1:], lambda i: (layer, 0, 0), pipeline_mode=pl.Buffered(1))
    return pl.pallas_call(
        functools.partial(_post_body, len(mix)),
        grid=(t // tm,),
        in_specs=[row(m.shape[-1]) for m in mix] + [
            row(D_MODEL), pl.BlockSpec((None, tm, PLE_DIM), lambda i: (layer, i, 0)), _resident(wo.shape), gain, gain,
            of_layer(wg), of_layer(wu), of_layer(wd), gain, of_layer(wpp), of_layer(wpg), gain],
        out_specs=row(D_MODEL),
        out_shape=jax.ShapeDtypeStruct((t, D_MODEL), F32),
        compiler_params=_params("parallel"),
        name="post",
    )(*mix, h, p, wo, g_mix, g_pre, wg, wu, wd, g_post, wpp, wpg, g_ple)


def _rope_tables(pos, invf, sgn):
    ang = pos * invf
    return jnp.cos(ang), jnp.sin(ang) * sgn


def _mla_latents(h_ref, g_pre, wdq_ref, g_q, wkv_ref, g_kv, wpe_ref, wpes_ref, cos32, sin32):
    u = _rms(h_ref[...], g_pre[...]).astype(BF16)
    cqn = _rms(_dot(u, wdq_ref[...]), g_q[...]).astype(BF16)
    ckv = _rms(_dot(u, wkv_ref[...]), g_kv[...])
    kpe = _dot(u, wpe_ref[...]) * cos32 + _dot(u, wpes_ref[...]) * sin32
    return cqn, ckv, kpe


def _mla_pre_prompt_body(h_ref, g_pre, wdq_ref, g_q, wq_ref, wkvp_ref, g_kv,
                         wuk_ref, wuvt_ref, ones_ref, invf_ref, sgn_ref,
                         q_ref, k_ref, vt_ref, ckv_ref, kpe_ref):
    tm = TOKEN_TILE
    i = pl.program_id(0)
    pos = ((i * tm + _iota((tm, 1), 0)) & (SEQ - 1)).astype(F32)
    cos128, sin128 = _rope_tables(pos, invf_ref[...], sgn_ref[...])
    rope = slice(MLA_NOPE_DIM, MLA_NOPE_DIM + MLA_ROPE_DIM)
    half = MLA_ROPE_DIM // 2
    first_half = _iota((tm, MLA_HEAD_PAD), 1) < MLA_NOPE_DIM + half

    def swap_rope_halves(x):
        return jnp.where(first_half, pltpu.roll(x, MLA_HEAD_PAD - half, 1), pltpu.roll(x, half, 1))

    u = _rms(h_ref[...], g_pre[...]).astype(BF16)
    cqn = _rms(_dot(u, wdq_ref[...]), g_q[...]).astype(BF16)
    kvp = _dot(u, wkvp_ref[...])
    ckv = _rms(kvp[:, :MLA_KV_RANK], g_kv[...])
    kpe_raw = kvp[:, MLA_KV_RANK:]
    kpe_pad = kpe_raw * cos128 + swap_rope_halves(kpe_raw) * sin128
    ckv_ref[...] = ckv
    kpe_ref[...] = kpe_pad[:, rope]
    ckv_b = ckv.astype(BF16)
    vt_ref[0, 0] = (_dot_nt(wuvt_ref[...], ckv_b) + ones_ref[...]).astype(BF16)
    for h in range(MLA_HEADS):
        hs = slice(h * MLA_HEAD_PAD, (h + 1) * MLA_HEAD_PAD)
        q = _dot(cqn, wq_ref[:, hs])
        q = q * cos128 + swap_rope_halves(q) * sin128
        q_ref[0, h] = (q * (MLA_SCALE * LOG2_E)).astype(BF16)
        k_ref[0, h] = (_dot(ckv_b, wuk_ref[:, hs]) + kpe_pad).astype(BF16)


def _mla_pre_prompt(h, g_pre, wdq, g_q, wq, wkvp, g_kv, wuk, wuvt, ones_col, invf, sgn):
    tm = TOKEN_TILE
    assert tm == MLA_TK
    per_seq = SEQ // tm
    heads = pl.BlockSpec((1, MLA_HEADS, tm, MLA_HEAD_PAD), lambda i: (i // per_seq, 0, i % per_seq, 0))
    head_shape = jax.ShapeDtypeStruct((BATCH, MLA_HEADS, SEQ, MLA_HEAD_PAD), BF16)
    vt_rows = MLA_HEADS * MLA_VT_ROWS
    row = lambda n: pl.BlockSpec((tm, n), lambda i: (i, 0))
    consts = [g_pre, wdq, g_q, wq, wkvp, g_kv, wuk, wuvt, ones_col, invf, sgn]
    return pl.pallas_call(
        _mla_pre_prompt_body,
        grid=(h.shape[0] // tm,),
        in_specs=[row(D_MODEL)] + [_resident(c.shape) for c in consts],
        out_specs=[heads, heads,
                   pl.BlockSpec((1, 1, vt_rows, tm), lambda i: (i // per_seq, i % per_seq, 0, 0)),
                   row(MLA_KV_RANK), row(MLA_ROPE_DIM)],
        out_shape=[head_shape, head_shape,
                   jax.ShapeDtypeStruct((BATCH, per_seq, vt_rows, tm), BF16),
                   jax.ShapeDtypeStruct((h.shape[0], MLA_KV_RANK), F32),
                   jax.ShapeDtypeStruct((h.shape[0], MLA_ROPE_DIM), F32)],
        compiler_params=_params("parallel"),
        name="mla_pre_prompt",
    )(h, *consts)


def _mla_prompt_attn_body(q_ref, k_ref, vt_ref, o_ref, m_ref, acc_ref):
    i = pl.program_id(2)
    m_ref[...] = jnp.full_like(m_ref, NEG)
    acc_ref[...] = jnp.zeros_like(acc_ref)
    causal = _iota((MLA_TK, MLA_TK), 0) <= _iota((MLA_TK, MLA_TK), 1)
    splits = MLA_TQ // MLA_TK

    def kv_tile(j, units):
        k0 = pl.multiple_of(j * MLA_TK, MLA_TK)
        cols = lambda blk: slice(blk * MLA_TK, (blk + 1) * MLA_TK)

        def scores(hh, blk, masked):
            s = _dot_nt(k_ref[0, hh, pl.ds(k0, MLA_TK), :], q_ref[0, hh, cols(blk), :])
            return jnp.where(causal, s, NEG) if masked else s

        def softmax(hh, blk, s):
            m_prev = m_ref[hh, :, cols(blk)]
            m_new = jnp.maximum(m_prev, jnp.max(s, axis=0, keepdims=True))
            m_ref[hh, :, cols(blk)] = m_new
            return jnp.exp2(s - m_new).astype(BF16), jnp.exp2(m_prev - m_new)

        def values(hh, blk, p, alpha):
            vt = vt_ref[0, j, hh * MLA_VT_ROWS:(hh + 1) * MLA_VT_ROWS, :]
            acc_ref[hh, :, cols(blk)] = alpha * acc_ref[hh, :, cols(blk)] + _dot(vt, p)

        n = len(units)
        s_of, p_of = {}, {}
        for step in range(n + 2):
            if step < n:
                s_of[step] = scores(*units[step])
            if 0 <= step - 1 < n:
                hh, blk, _ = units[step - 1]
                p_of[step - 1] = softmax(hh, blk, s_of.pop(step - 1))
            if 0 <= step - 2 < n:
                hh, blk, _ = units[step - 2]
                values(hh, blk, *p_of.pop(step - 2))

    heads = range(MLA_HEAD_GROUP)

    def below_diagonal(j, carry):
        kv_tile(j, [(hh, blk, False) for hh in heads for blk in range(splits)])
        return carry

    lax.fori_loop(0, i * splits, below_diagonal, 0)
    for d in range(splits):
        kv_tile(i * splits + d, [(hh, blk, blk == d) for hh in heads for blk in range(d, splits)])
    for pair in range(MLA_HEAD_GROUP // 2):
        heads = []
        for hh in (2 * pair, 2 * pair + 1):
            acc = acc_ref[hh]
            heads.append(acc[:MLA_V_DIM] / acc[MLA_V_DIM:MLA_V_DIM + 1])
        o_t = jnp.concatenate(heads, axis=0)
        o_ref[0, :, pair * 2 * MLA_V_DIM:(pair + 1) * 2 * MLA_V_DIM] = o_t.T.astype(BF16)


def _mla_prompt_attn(q, k, vt):
    hg = MLA_HEAD_GROUP
    assert MLA_TQ % MLA_TK == 0 and hg % 2 == 0
    return pl.pallas_call(
        _mla_prompt_attn_body,
        grid=(BATCH, MLA_HEADS // hg, SEQ // MLA_TQ),
        in_specs=[pl.BlockSpec((1, hg, MLA_TQ, MLA_HEAD_PAD), lambda b, g, i: (b, g, i, 0)),
                  pl.BlockSpec((1, hg, SEQ, MLA_HEAD_PAD), lambda b, g, i: (b, g, 0, 0),
                               pipeline_mode=pl.Buffered(1)),
                  pl.BlockSpec((1, SEQ // MLA_TK, hg * MLA_VT_ROWS, MLA_TK), lambda b, g, i: (b, 0, g, 0),
                               pipeline_mode=pl.Buffered(1))],
        out_specs=pl.BlockSpec((1, MLA_TQ, hg * MLA_V_DIM), lambda b, g, i: (b, i, g)),
        out_shape=jax.ShapeDtypeStruct((BATCH, SEQ, MLA_HEADS * MLA_V_DIM), BF16),
        scratch_shapes=[pltpu.VMEM((hg, 1, MLA_TQ), F32), pltpu.VMEM((hg, MLA_VT_ROWS, MLA_TQ), F32)],
        compiler_params=_params("parallel", "parallel", "arbitrary"),
        name="mla_prompt_attn",
    )(q, k, vt)


def _mla_pre_sample_body(h_ref, g_pre, wdq_ref, g_q, wn_ref, wr_ref, wrs_ref, wukt_ref, wkv_ref, g_kv,
                         wpe_ref, wpes_ref, invf_ref, sgn_ref,
                         ql_ref, qp_ref, ckv_ref, kpe_ref):
    tm = TOKEN_TILE
    pos = (PAST_LEN + (_iota((tm, 1), 0) & (DEC_SEQ - 1))).astype(F32)
    cos32, sin32 = _rope_tables(pos, invf_ref[...], sgn_ref[...])
    cqn, ckv, kpe = _mla_latents(h_ref, g_pre, wdq_ref, g_q, wkv_ref, g_kv, wpe_ref, wpes_ref, cos32, sin32)
    ckv_ref[...] = ckv
    kpe_ref[...] = kpe
    n_seq = tm // DEC_SEQ
    for h in range(MLA_HEADS):
        rows = slice(h * DEC_SEQ, (h + 1) * DEC_SEQ)
        q_nope = _dot(cqn, wn_ref[h]).astype(BF16)
        q_lat = _dot(q_nope, wukt_ref[h]) * (MLA_SCALE * LOG2_E)
        ql_ref[:, rows, :] = q_lat.reshape(n_seq, DEC_SEQ, MLA_KV_RANK)
        q_pe = (_dot(cqn, wr_ref[h]) * cos32 + _dot(cqn, wrs_ref[h]) * sin32) * (MLA_SCALE * LOG2_E)
        qp_ref[:, rows, :] = q_pe.reshape(n_seq, DEC_SEQ, MLA_ROPE_DIM)


def _mla_pre_sample(h, g_pre, wdq, g_q, wn, wr, wrs, wukt, wkv, g_kv, wpe, wpes, invf, sgn):
    tm = TOKEN_TILE
    n_seq = tm // DEC_SEQ
    rows = MLA_HEADS * DEC_SEQ
    row = lambda n: pl.BlockSpec((tm, n), lambda i: (i, 0))
    consts = [g_pre, wdq, g_q, wn, wr, wrs, wukt, wkv, g_kv, wpe, wpes, invf, sgn]
    return pl.pallas_call(
        _mla_pre_sample_body,
        grid=(h.shape[0] // tm,),
        in_specs=[row(D_MODEL)] + [_resident(c.shape) for c in consts],
        out_specs=[pl.BlockSpec((n_seq, rows, MLA_KV_RANK), lambda i: (i, 0, 0)),
                   pl.BlockSpec((n_seq, rows, MLA_ROPE_DIM), lambda i: (i, 0, 0)),
                   row(MLA_KV_RANK), row(MLA_ROPE_DIM)],
        out_shape=[jax.ShapeDtypeStruct((DEC_BATCH, rows, MLA_KV_RANK), F32),
                   jax.ShapeDtypeStruct((DEC_BATCH, rows, MLA_ROPE_DIM), F32),
                   jax.ShapeDtypeStruct((h.shape[0], MLA_KV_RANK), F32),
                   jax.ShapeDtypeStruct((h.shape[0], MLA_ROPE_DIM), F32)],
        compiler_params=_params("parallel"),
        name="mla_pre_sample",
    )(h, *consts)


def _mla_sample_attn_body(pt_ref, ql_ref, qp_ref, cn_ref, pn_ref, ckv_hbm, kpe_hbm, o_ref,
                          m_ref, l_ref, acc_ref, cnp_ref, pnp_ref, ck_buf, kp_buf, ck_sem, kp_sem):
    npg = PAGES_PER_GROUP
    groups_per_seq = N_PAGES // npg
    total_groups = (DEC_BATCH // SAMPLE_STREAMS) * groups_per_seq
    b = pl.program_id(0)
    rows = MLA_HEADS * DEC_SEQ
    streams = range(SAMPLE_STREAMS)

    def page_copies(page_of, st, slot):
        copies = []
        for j in range(npg):
            page = page_of(j)
            copies.append(pltpu.make_async_copy(
                ckv_hbm.at[0, page], ck_buf.at[st, slot, pl.ds(j * PAGE_SIZE, PAGE_SIZE), :], ck_sem.at[st, slot]))
            copies.append(pltpu.make_async_copy(kpe_hbm.at[0, page], kp_buf.at[st, slot, j], kp_sem.at[st, slot]))
        return copies

    def start_group(t, slot):
        step, g = t // groups_per_seq, t % groups_per_seq
        for st in streams:
            base = (step * SAMPLE_STREAMS + st) * N_PAGES + g * npg
            for c in page_copies(lambda j: pt_ref[base + j], st, slot):
                c.start()

    def wait_group(slot):
        for st in streams:
            for c in page_copies(lambda j: 0, st, slot):
                c.wait()

    @pl.when(b == 0)
    def _():
        for t0 in range(PAGE_RING):
            start_group(t0, t0)

    ql = [ql_ref[st].astype(BF16) for st in streams]
    qp = [qp_ref[st].astype(BF16) for st in streams]

    cnp_ref[...] = jnp.zeros_like(cnp_ref)
    pnp_ref[...] = jnp.zeros_like(pnp_ref)
    cnp_ref[:, :DEC_SEQ, :] = cn_ref[...]
    pnp_ref[:, :DEC_SEQ, :] = pn_ref[...]
    t_q = _iota((rows, PAGE_SIZE), 0) & (DEC_SEQ - 1)
    new_valid = _iota((rows, PAGE_SIZE), 1) <= t_q
    cn = [cnp_ref[st].astype(BF16) for st in streams]
    s_new = [jnp.where(new_valid, _dot_nt(ql[st], cn[st]) + _dot_nt(qp[st], pnp_ref[st].astype(BF16)), NEG)
             for st in streams]

    def update(st, s, keys, first):
        m_cur = jnp.max(s, axis=-1, keepdims=True)
        if first:
            p = jnp.exp2(s - m_cur)
            m_ref[st] = m_cur
            l_ref[st] = jnp.sum(p, axis=-1, keepdims=True)
            acc_ref[st] = _dot(p.astype(BF16), keys)
            return
        m_prev = m_ref[st]
        m_new = jnp.maximum(m_prev, m_cur)
        alpha = jnp.exp2(m_prev - m_new)
        p = jnp.exp2(s - m_new)
        l_ref[st] = alpha * l_ref[st] + jnp.sum(p, axis=-1, keepdims=True)
        acc_ref[st] = alpha * acc_ref[st] + _dot(p.astype(BF16), keys)
        m_ref[st] = m_new

    def group(g, first):
        t = b * groups_per_seq + g
        slot = t % PAGE_RING
        wait_group(slot)

        def scores(st, c):
            pages = range(c * PAGES_PER_CHUNK, (c + 1) * PAGES_PER_CHUNK)
            kl = ck_buf[st, slot, pages[0] * PAGE_SIZE:(pages[-1] + 1) * PAGE_SIZE, :].astype(BF16)
            kpt = jnp.concatenate([kp_buf[st, slot, j].astype(BF16) for j in pages], axis=1)
            s = _dot_nt(ql[st], kl) + _dot(qp[st], kpt)
            if first and c == 0:
                return jnp.concatenate([s, s_new[st]], axis=1), jnp.concatenate([kl, cn[st]], axis=0)
            return s, kl

        units = [(st, c) for c in range(npg // PAGES_PER_CHUNK) for st in streams]
        nxt = scores(*units[0])
        for k, (st, c) in enumerate(units):
            cur = nxt
            if k + 1 < len(units):
                nxt = scores(*units[k + 1])
            update(st, *cur, first and c == 0)

        start_group(jnp.minimum(t + PAGE_RING, total_groups - 1), slot)

    group(0, True)

    def later_group(g, carry):
        group(g, False)
        return carry

    lax.fori_loop(1, groups_per_seq, later_group, 0)
    o_ref[...] = (acc_ref[...] / l_ref[...]).astype(BF16)

    @pl.when(b == pl.num_programs(0) - 1)
    def _():
        for slot in range(PAGE_RING):
            wait_group(slot)


def _mla_sample_attn(page_table, ql, qp, ckv_new, kpe_new, ckv_pool, kpe_pool):
    npg = PAGES_PER_GROUP
    ns = SAMPLE_STREAMS
    rows = MLA_HEADS * DEC_SEQ
    assert N_PAGES % npg == 0 and npg % PAGES_PER_CHUNK == 0 and DEC_BATCH % ns == 0
    seq = lambda r, w: pl.BlockSpec((ns, r, w), lambda b, pt: (b, 0, 0))
    hbm = pl.BlockSpec(memory_space=pl.ANY)
    grid_spec = pltpu.PrefetchScalarGridSpec(
        num_scalar_prefetch=1,
        grid=(DEC_BATCH // ns,),
        in_specs=[seq(rows, MLA_KV_RANK), seq(rows, MLA_ROPE_DIM), seq(DEC_SEQ, MLA_KV_RANK),
                  seq(DEC_SEQ, MLA_ROPE_DIM), hbm, hbm],
        out_specs=seq(rows, MLA_KV_RANK),
        scratch_shapes=[pltpu.VMEM((ns, rows, 1), F32), pltpu.VMEM((ns, rows, 1), F32),
                        pltpu.VMEM((ns, rows, MLA_KV_RANK), F32),
                        pltpu.VMEM((ns, PAGE_SIZE, MLA_KV_RANK), F32), pltpu.VMEM((ns, PAGE_SIZE, MLA_ROPE_DIM), F32),
                        pltpu.VMEM((ns, PAGE_RING, npg * PAGE_SIZE, MLA_KV_RANK), F32),
                        pltpu.VMEM((ns, PAGE_RING, npg, MLA_ROPE_DIM, PAGE_SIZE), F32),
                        pltpu.SemaphoreType.DMA((ns, PAGE_RING)), pltpu.SemaphoreType.DMA((ns, PAGE_RING))],
    )
    return pl.pallas_call(
        _mla_sample_attn_body,
        grid_spec=grid_spec,
        out_shape=jax.ShapeDtypeStruct((DEC_BATCH, rows, MLA_KV_RANK), BF16),
        compiler_params=_params("arbitrary"),
        name="mla_sample_attn",
    )(page_table, ql, qp, ckv_new, kpe_new, ckv_pool, kpe_pool)


def _mla_sample_out_body(x_ref, w_ref, o_ref):
    x = x_ref[...].reshape(DEC_BATCH * 2 * DEC_SEQ, MLA_KV_RANK)
    y = _dot(x, w_ref[0]).reshape(DEC_BATCH, 2 * DEC_SEQ, 2 * MLA_V_DIM)
    first_head = _iota((DEC_BATCH, DEC_SEQ, 2 * MLA_V_DIM), 2) < MLA_V_DIM
    o = jnp.where(first_head, y[:, :DEC_SEQ], y[:, DEC_SEQ:])
    o_ref[...] = o.reshape(DEC_BATCH * DEC_SEQ, 2 * MLA_V_DIM)


def _mla_sample_out(o_lat, wuv_pairs):
    return pl.pallas_call(
        _mla_sample_out_body,
        grid=(MLA_HEADS // 2,),
        in_specs=[pl.BlockSpec((DEC_BATCH, 2 * DEC_SEQ, MLA_KV_RANK), lambda j: (0, j, 0)),
                  pl.BlockSpec((1, MLA_KV_RANK, 2 * MLA_V_DIM), lambda j: (j, 0, 0))],
        out_specs=pl.BlockSpec((DEC_BATCH * DEC_SEQ, 2 * MLA_V_DIM), lambda j: (0, j)),
        out_shape=jax.ShapeDtypeStruct((DEC_BATCH * DEC_SEQ, MLA_HEADS * MLA_V_DIM), F32),
        compiler_params=_params("parallel"),
        name="mla_sample_out",
    )(o_lat, wuv_pairs)


def _swap_halves(w):
    half = w.shape[-1] // 2
    return jnp.concatenate([w[..., half:], w[..., :half]], axis=-1)


def _rope_consts():
    half = MLA_ROPE_DIM // 2
    inv_freq = ROPE_THETA ** (-jnp.arange(0, half, dtype=F32) * 2.0 / MLA_ROPE_DIM)
    invf32 = jnp.concatenate([inv_freq, inv_freq])[None]
    sgn32 = jnp.concatenate([-jnp.ones((half,), F32), jnp.ones((half,), F32)])[None]
    pad = lambda a: jnp.pad(a, ((0, 0), (MLA_NOPE_DIM, MLA_HEAD_PAD - MLA_NOPE_DIM - MLA_ROPE_DIM)))
    return invf32, sgn32, pad(invf32), pad(sgn32)


def _mla_weights(w_uq, w_dkv, w_uk, w_uv):
    wq = w_uq.reshape(MLA_Q_RANK, MLA_HEADS, MLA_NOPE_DIM + MLA_ROPE_DIM)
    wq_nope, wq_rope = wq[..., :MLA_NOPE_DIM], wq[..., MLA_NOPE_DIM:]
    tail = MLA_HEAD_PAD - MLA_NOPE_DIM - MLA_ROPE_DIM
    pad_last = lambda a, lo, hi: jnp.pad(a, [(0, 0)] * (a.ndim - 1) + [(lo, hi)])
    flat = lambda a: a.reshape(a.shape[0], MLA_HEADS * MLA_HEAD_PAD).astype(BF16)
    w = {}
    w["wq"] = flat(pad_last(wq, 0, tail))
    w["wkvp"] = jnp.concatenate([w_dkv[:, :MLA_KV_RANK], pad_last(w_dkv[:, MLA_KV_RANK:], MLA_NOPE_DIM, tail)],
                                axis=1).astype(BF16)
    w["wuk"] = flat(pad_last(w_uk, 0, MLA_HEAD_PAD - MLA_NOPE_DIM))
    wuvt = jnp.transpose(w_uv, (1, 2, 0))
    w["wuvt"] = jnp.pad(wuvt, ((0, 0), (0, MLA_VT_ROWS - MLA_V_DIM), (0, 0))).reshape(
        MLA_HEADS * MLA_VT_ROWS, MLA_KV_RANK).astype(BF16)
    w["ones_col"] = jnp.tile((jnp.arange(MLA_VT_ROWS) == MLA_V_DIM).astype(F32), MLA_HEADS)[:, None]
    w["wn"] = jnp.moveaxis(wq_nope, 1, 0).astype(BF16)
    w["wr"] = jnp.moveaxis(wq_rope, 1, 0).astype(BF16)
    w["wrs"] = jnp.moveaxis(_swap_halves(wq_rope), 1, 0).astype(BF16)
    w["wukt"] = jnp.transpose(w_uk, (1, 2, 0)).astype(BF16)
    w["wkv"] = w_dkv[:, :MLA_KV_RANK].astype(BF16)
    w["wpe"] = w_dkv[:, MLA_KV_RANK:].astype(BF16)
    w["wpes"] = _swap_halves(w_dkv[:, MLA_KV_RANK:]).astype(BF16)
    w["wuv_pairs"] = jnp.transpose(w_uv.reshape(MLA_KV_RANK, MLA_HEADS // 2, 2 * MLA_V_DIM),
                                   (1, 0, 2)).astype(BF16)
    return w


def kernel(x_prompt, x_sample, cache_swa_k, cache_swa_v, state_hgrn, cache_mla_ckv, cache_mla_kpe, page_table, p_prompt, p_sample, ln_mix_pre, ln_mix_post, ln_ffn_pre, ln_ffn_post, ln_ple, w_ab_in, w_ab_out, swa_sinks, hgrn_lb_logits, hgrn_out_norm, w_mla_dq, mla_q_norm, w_mla_uq, w_mla_dkv, mla_kv_norm, w_mla_uk, w_mla_uv, w_mla_o, w_ffn_gate, w_ffn_up, w_ffn_down, w_ple_proj, w_ple_gate):
    tp = BATCH * SEQ
    ts = DEC_BATCH * DEC_SEQ
    bf = lambda a: a.astype(BF16)
    gain = lambda a, i: a[i][None].astype(F32)

    ffn_w = [bf(w) for w in (w_ffn_gate, w_ffn_up, w_ffn_down)]
    ple_w = [bf(w) for w in (w_ple_proj, w_ple_gate)]

    def finish(mix, h, p, i, wo):
        return _post(i, mix, h, p, wo, gain(ln_mix_post, i), gain(ln_ffn_pre, i), *ffn_w, gain(ln_ffn_post, i),
                     *ple_w, gain(ln_ple, i))

    hp = x_prompt.reshape(tp, D_MODEL)
    hs = x_sample.reshape(ts, D_MODEL)
    pp = p_prompt.reshape(2, tp, PLE_DIM)
    ps = p_sample.reshape(2, ts, PLE_DIM)

    w_in = bf(w_ab_in[0])
    w_out = bf(w_ab_out[0])
    g0 = gain(ln_mix_pre, 0)
    sinks = swa_sinks[0].astype(F32)
    lbl = hgrn_lb_logits.astype(F32)
    on = hgrn_out_norm[0][None].astype(F32)

    qa, ka, va, hg = _ab_in(hp, g0, w_in, BF16)
    oa = _swa_prompt(sinks, qa.reshape(BATCH, SEQ, SWA_Q_W), ka.reshape(BATCH, SEQ, SWA_KV_W),
                     va.reshape(BATCH, SEQ, SWA_KV_W))
    ob, hgrn_p = _hgrn_prompt(lbl, on, hg.reshape(BATCH, SEQ, 4 * HGRN_W))
    swa_k_p = ka.reshape(BATCH, SEQ, SWA_KV_HEADS, SWA_HEAD_DIM)[:, -WINDOW:]
    swa_v_p = va.reshape(BATCH, SEQ, SWA_KV_HEADS, SWA_HEAD_DIM)[:, -WINDOW:]
    hp = finish([oa.reshape(tp, SWA_Q_W), ob.reshape(tp, HGRN_W)], hp, pp, 0, w_out)

    qa, ka, va, hg = _ab_in(hs, g0, w_in, F32)
    oa, swa_k_s, swa_v_s = _swa_sample(
        sinks, qa, ka.reshape(DEC_BATCH, DEC_SEQ, SWA_KV_W),
        va.reshape(DEC_BATCH, DEC_SEQ, SWA_KV_W), cache_swa_k[0].reshape(DEC_BATCH, WINDOW, SWA_KV_W),
        cache_swa_v[0].reshape(DEC_BATCH, WINDOW, SWA_KV_W))
    ob, hgrn_s = _hgrn_sample(lbl, on, hg, state_hgrn[0])
    hs = finish([oa.reshape(ts, SWA_Q_W), ob], hs, ps, 0, w_out)

    mw = _mla_weights(w_mla_uq[0], w_mla_dkv[0], w_mla_uk[0], w_mla_uv[0])
    invf32, sgn32, invf128, sgn128 = _rope_consts()
    g1 = gain(ln_mix_pre, 1)
    wdq = bf(w_mla_dq[0])
    g_q = mla_q_norm[0][None].astype(F32)
    g_kv = mla_kv_norm[0][None].astype(F32)
    w_o = bf(w_mla_o[0])

    q, k, vt, ckv_p, kpe_p = _mla_pre_prompt(hp, g1, wdq, g_q, mw["wq"], mw["wkvp"], g_kv, mw["wuk"], mw["wuvt"],
                                             mw["ones_col"], invf128, sgn128)
    o = _mla_prompt_attn(q, k, vt)
    hp = finish([o.reshape(tp, MLA_HEADS * MLA_V_DIM)], hp, pp, 1, w_o)

    ql, qp, ckv_s, kpe_s = _mla_pre_sample(hs, g1, wdq, g_q, mw["wn"], mw["wr"], mw["wrs"], mw["wukt"], mw["wkv"],
                                           g_kv, mw["wpe"], mw["wpes"], invf32, sgn32)
    o_lat = _mla_sample_attn(page_table.reshape(-1), ql, qp, ckv_s.reshape(DEC_BATCH, DEC_SEQ, MLA_KV_RANK),
                             kpe_s.reshape(DEC_BATCH, DEC_SEQ, MLA_ROPE_DIM), cache_mla_ckv,
                             jnp.swapaxes(cache_mla_kpe, 2, 3))
    o = _mla_sample_out(o_lat, mw["wuv_pairs"])
    hs = finish([o], hs, ps, 1, w_o)

    kv5 = lambda a, n: a.reshape(1, n, WINDOW, SWA_KV_HEADS, SWA_HEAD_DIM)
    return (hp.reshape(BATCH, SEQ, D_MODEL), hs.reshape(DEC_BATCH, DEC_SEQ, D_MODEL),
            swa_k_p[None], swa_v_p[None], hgrn_p[None],
            ckv_p.reshape(1, BATCH, SEQ, MLA_KV_RANK), kpe_p.reshape(1, BATCH, SEQ, MLA_ROPE_DIM),
            kv5(swa_k_s, DEC_BATCH), kv5(swa_v_s, DEC_BATCH), hgrn_s[None],
            ckv_s.reshape(1, DEC_BATCH, DEC_SEQ, MLA_KV_RANK), kpe_s.reshape(1, DEC_BATCH, DEC_SEQ, MLA_ROPE_DIM))
```

```python
import functools

import jax
import jax.numpy as jnp
from jax import lax
from jax.experimental import pallas as pl
from jax.experimental.pallas import tpu as pltpu

F32 = jnp.float32
BF16 = jnp.bfloat16

D_MODEL = 1024
BATCH = 2
SEQ = 8192
DEC_BATCH = 128
DEC_SEQ = 8
PAST_LEN = 16384
PAGE_SIZE = 128
N_PAGES = PAST_LEN // PAGE_SIZE

SWA_HEADS = 8
SWA_KV_HEADS = 2
SWA_GROUP = SWA_HEADS // SWA_KV_HEADS
SWA_HEAD_DIM = 64
WINDOW = 128
SWA_SCALE = SWA_HEAD_DIM ** -0.5
SWA_Q_W = SWA_HEADS * SWA_HEAD_DIM
SWA_KV_W = SWA_KV_HEADS * SWA_HEAD_DIM

HGRN_HEADS = 4
HGRN_HEAD_DIM = 128
HGRN_CHUNK = 32
HGRN_W = HGRN_HEADS * HGRN_HEAD_DIM
AB_IN_W = SWA_Q_W + 2 * SWA_KV_W + 4 * HGRN_W

MLA_HEADS = 16
MLA_NOPE_DIM = 64
MLA_ROPE_DIM = 32
MLA_V_DIM = 64
MLA_Q_RANK = 512
MLA_KV_RANK = 256
MLA_SCALE = (MLA_NOPE_DIM + MLA_ROPE_DIM) ** -0.5
ROPE_THETA = 10000.0
MLA_HEAD_PAD = 128

D_FF = 2816
PLE_DIM = 256
NORM_EPS = 1e-6
NEG = -1e30
LOG2_E = 1.4426950408889634

VMEM_LIMIT_BYTES = 56 * 1024 * 1024

TOKEN_TILE = 512
HGRN_TILE = 256
SWA_TQ = 256
SAMPLE_BATCH_TILE = 16
MLA_TQ = 1024
MLA_TK = 512
MLA_HEAD_GROUP = 8
MLA_VT_ROWS = 80
PAGES_PER_GROUP = 16
PAGES_PER_CHUNK = 16
SAMPLE_STREAMS = 4
PAGE_RING = 3


def _dot(a, b):
    return jnp.dot(a, b, preferred_element_type=F32)


def _dot_nt(a, b):
    return lax.dot_general(a, b, (((1,), (1,)), ((), ())), preferred_element_type=F32)


def _rms(x, g):
    return x * lax.rsqrt(jnp.mean(x * x, axis=-1, keepdims=True) + NORM_EPS) * g


def _sigmoid(x):
    return 1.0 / (1.0 + jnp.exp(-x))


def _silu(x):
    return x * _sigmoid(x)


def _iota(shape, dim):
    return lax.broadcasted_iota(jnp.int32, shape, dim)


def _params(*sem):
    return pltpu.CompilerParams(dimension_semantics=sem, vmem_limit_bytes=VMEM_LIMIT_BYTES)


def _resident(shape):
    nd = len(shape)
    return pl.BlockSpec(shape, lambda *_: (0,) * nd, pipeline_mode=pl.Buffered(1))


def _smem():
    return pl.BlockSpec(memory_space=pltpu.SMEM)


def _ab_in_body(h_ref, g_ref, w_ref, qa_ref, k_ref, v_ref, hg_ref):
    u = _rms(h_ref[...], g_ref[...]).astype(BF16)
    qa_ref[...] = (_dot(u, w_ref[:, :SWA_Q_W]) * SWA_SCALE).astype(qa_ref.dtype)
    k_ref[...] = _dot(u, w_ref[:, SWA_Q_W:SWA_Q_W + SWA_KV_W])
    v_ref[...] = _dot(u, w_ref[:, SWA_Q_W + SWA_KV_W:SWA_Q_W + 2 * SWA_KV_W])
    hg_ref[...] = _dot(u, w_ref[:, SWA_Q_W + 2 * SWA_KV_W:])


def _ab_in(h, g, w, q_dtype):
    t = h.shape[0]
    tm = TOKEN_TILE
    row = lambda n: pl.BlockSpec((tm, n), lambda i: (i, 0))
    return pl.pallas_call(
        _ab_in_body,
        grid=(t // tm,),
        in_specs=[row(D_MODEL), _resident((1, D_MODEL)), _resident((D_MODEL, AB_IN_W))],
        out_specs=[row(SWA_Q_W), row(SWA_KV_W), row(SWA_KV_W), row(4 * HGRN_W)],
        out_shape=[jax.ShapeDtypeStruct((t, SWA_Q_W), q_dtype),
                   jax.ShapeDtypeStruct((t, SWA_KV_W), F32),
                   jax.ShapeDtypeStruct((t, SWA_KV_W), F32),
                   jax.ShapeDtypeStruct((t, 4 * HGRN_W), F32)],
        compiler_params=_params("parallel"),
        name="ab_in",
    )(h, g, w)


def _swa_head(q, keys, vals, dists, valids, slope, sink):
    scores = []
    for k, dist, valid in zip(keys, dists, valids):
        s = _dot_nt(q, k) - slope * dist
        scores.append(jnp.where(valid, s, NEG))
    m = sink
    for s in scores:
        m = jnp.maximum(m, jnp.max(s, axis=-1, keepdims=True))
    den = jnp.exp(sink - m)
    acc = None
    for s, v in zip(scores, vals):
        p = jnp.exp(s - m)
        den = den + jnp.sum(p, axis=-1, keepdims=True)
        pv = _dot(p.astype(BF16), v)
        acc = pv if acc is None else acc + pv
    return acc / den


def _swa_prompt_body(sink_ref, q_ref, kc_ref, kp_ref, vc_ref, vp_ref, o_ref):
    i = pl.program_id(1)
    nk = WINDOW + SWA_TQ
    kk = jnp.concatenate([kp_ref[0], kc_ref[0]], axis=0).astype(BF16)
    vv_t = jnp.concatenate([vp_ref[0], vc_ref[0]], axis=0).T.astype(BF16)
    key = _iota((nk, SWA_TQ), 0)
    qry = _iota((nk, SWA_TQ), 1)
    dist_i = qry + WINDOW - key
    valid = (dist_i >= 0) & (dist_i <= WINDOW) & (key >= jnp.where(i > 0, 0, WINDOW))
    dist = dist_i.astype(F32)
    q = q_ref[0]
    kv_lanes = lambda j: slice(j // SWA_GROUP * SWA_HEAD_DIM, (j // SWA_GROUP + 1) * SWA_HEAD_DIM)

    def scores(j):
        s = _dot_nt(kk[:, kv_lanes(j)], q[:, j * SWA_HEAD_DIM:(j + 1) * SWA_HEAD_DIM]) - 2.0 ** -(j + 1) * dist
        return jnp.where(valid, s, NEG)

    def attend(j, s):
        m = jnp.maximum(jnp.max(s, axis=0, keepdims=True), sink_ref[j])
        p = jnp.exp(s - m)
        den = jnp.sum(p, axis=0, keepdims=True) + jnp.exp(sink_ref[j] - m)
        return _dot(vv_t[kv_lanes(j), :], p.astype(BF16)) / den

    heads = []
    nxt = scores(0)
    for j in range(SWA_HEADS):
        cur = nxt
        if j + 1 < SWA_HEADS:
            nxt = scores(j + 1)
        heads.append(attend(j, cur))
    for pair in range(SWA_HEADS // 2):
        o_t = jnp.concatenate(heads[2 * pair:2 * pair + 2], axis=0)
        o_ref[0, :, pair * 2 * SWA_HEAD_DIM:(pair + 1) * 2 * SWA_HEAD_DIM] = o_t.T.astype(BF16)


def _swa_prompt(sinks, qa, k, v):
    per_tile = SWA_TQ // WINDOW
    cur = lambda w: pl.BlockSpec((1, SWA_TQ, w), lambda b, i: (b, i, 0))
    prev = lambda w: pl.BlockSpec((1, WINDOW, w), lambda b, i: (b, jnp.maximum(i * per_tile - 1, 0), 0))
    return pl.pallas_call(
        _swa_prompt_body,
        grid=(BATCH, SEQ // SWA_TQ),
        in_specs=[_smem(), cur(SWA_Q_W), cur(SWA_KV_W), prev(SWA_KV_W), cur(SWA_KV_W), prev(SWA_KV_W)],
        out_specs=cur(SWA_Q_W),
        out_shape=jax.ShapeDtypeStruct((BATCH, SEQ, SWA_Q_W), BF16),
        compiler_params=_params("parallel", "arbitrary"),
        name="swa_prompt",
    )(sinks, qa, k, k, v, v)


def _swa_sample_body(sink_ref, q_ref, kn_ref, vn_ref, kb_ref, vb_ref, o_ref, ko_ref, vo_ref):
    bb = SAMPLE_BATCH_TILE
    t_new = DEC_SEQ
    r = bb * t_new
    nk = bb * WINDOW
    kb3, vb3, kn3, vn3 = kb_ref[...], vb_ref[...], kn_ref[...], vn_ref[...]
    ko_ref[:, :WINDOW - t_new, :] = kb3[:, t_new:, :]
    ko_ref[:, WINDOW - t_new:, :] = kn3
    vo_ref[:, :WINDOW - t_new, :] = vb3[:, t_new:, :]
    vo_ref[:, WINDOW - t_new:, :] = vn3
    kb = kb3.reshape(nk, SWA_KV_W).astype(BF16)
    vb = vb3.reshape(nk, SWA_KV_W).astype(BF16)
    kn = kn3.reshape(r, SWA_KV_W).astype(BF16)
    vn = vn3.reshape(r, SWA_KV_W).astype(BF16)

    row = _iota((r, nk), 0)
    col = _iota((r, nk), 1)
    t_q = row & (t_new - 1)
    slot = col & (WINDOW - 1)
    dist_b = (t_q - slot + WINDOW).astype(F32)
    valid_b = ((row >> 3) == (col >> 7)) & (slot >= t_q)
    row_n = _iota((r, r), 0)
    col_n = _iota((r, r), 1)
    dist_n = ((row_n & (t_new - 1)) - (col_n & (t_new - 1))).astype(F32)
    valid_n = ((row_n >> 3) == (col_n >> 3)) & (col_n <= row_n)
    q = q_ref[...].astype(BF16)
    for hk in range(SWA_KV_HEADS):
        ks = slice(hk * SWA_HEAD_DIM, (hk + 1) * SWA_HEAD_DIM)
        for g in range(SWA_GROUP):
            j = hk * SWA_GROUP + g
            qs = slice(j * SWA_HEAD_DIM, (j + 1) * SWA_HEAD_DIM)
            o_ref[:, qs] = _swa_head(q[:, qs], (kb[:, ks], kn[:, ks]), (vb[:, ks], vn[:, ks]),
                                     (dist_b, dist_n), (valid_b, valid_n), 2.0 ** -(j + 1), sink_ref[j])


def _swa_sample(sinks, qa, kn, vn, kbuf, vbuf):
    bb = SAMPLE_BATCH_TILE
    assert DEC_SEQ == 8 and WINDOW == 128
    blk = lambda r, w: pl.BlockSpec((bb, r, w), lambda i: (i, 0, 0))
    rows = pl.BlockSpec((bb * DEC_SEQ, SWA_Q_W), lambda i: (i, 0))
    return pl.pallas_call(
        _swa_sample_body,
        grid=(DEC_BATCH // bb,),
        in_specs=[_smem(), rows, blk(DEC_SEQ, SWA_KV_W), blk(DEC_SEQ, SWA_KV_W),
                  blk(WINDOW, SWA_KV_W), blk(WINDOW, SWA_KV_W)],
        out_specs=[rows, blk(WINDOW, SWA_KV_W), blk(WINDOW, SWA_KV_W)],
        out_shape=[jax.ShapeDtypeStruct((DEC_BATCH * DEC_SEQ, SWA_Q_W), F32),
                   jax.ShapeDtypeStruct((DEC_BATCH, WINDOW, SWA_KV_W), F32),
                   jax.ShapeDtypeStruct((DEC_BATCH, WINDOW, SWA_KV_W), F32)],
        compiler_params=_params("parallel"),
        name="swa_sample",
    )(sinks, qa, kn, vn, kbuf, vbuf)


def _hgrn_lower_bound(lbl_ref, layer):
    logits = lbl_ref[...]
    e = jnp.exp(logits - jnp.max(logits, axis=0, keepdims=True))
    sm = e / jnp.sum(e, axis=0, keepdims=True)
    return jnp.sum(sm[:layer + 1], axis=0, keepdims=True)


def _split3(x):
    hi = x.astype(BF16)
    r = x - hi.astype(F32)
    mid = r.astype(BF16)
    lo = (r - mid.astype(F32)).astype(BF16)
    return hi, mid, lo


def _hgrn_chunk_terms(x, lb, chunk):
    r = x.shape[0]
    qr, fr, ir, gr = (x[:, i * HGRN_W:(i + 1) * HGRN_W] for i in range(4))
    f = lb + (1.0 - lb) * _sigmoid(fr)
    logf = jnp.log(f)
    row = _iota((r, r), 0)
    col = _iota((r, r), 1)
    chunk_start = row & ~(chunk - 1)
    same = (col >= chunk_start) & (col < chunk_start + chunk)
    tril = (col >= chunk_start) & (col <= row)
    parts = _split3(logf)
    tri_m = jnp.where(tril, 1.0, 0.0).astype(BF16)
    blk_m = jnp.where(same, 1.0, 0.0).astype(BF16)
    b = sum(_dot(tri_m, p) for p in parts)
    bl = sum(_dot(blk_m, p) for p in parts)
    k = 1.0 - f
    qd = _silu(qr) * jnp.exp(b)
    kd = k * jnp.exp(-b)
    k2 = k * jnp.exp(bl - b)
    return qd, kd, k2, ir, gr, bl, tril


def _hgrn_intra(qd, kd, v, tril):
    att = jnp.where(tril, _dot_nt(qd.astype(BF16), kd.astype(BF16)), 0.0)
    return _dot(att.astype(BF16), v.astype(BF16))


def _hgrn_finish(o, g, on):
    return (_rms(o, on) * _silu(g)).astype(BF16)


def _hgrn_prompt_body(lbl_ref, on_ref, x_ref, o_ref, s_ref, st_ref):
    t = pl.program_id(1)

    @pl.when(t == 0)
    def _():
        st_ref[...] = jnp.zeros_like(st_ref)

    lb = _hgrn_lower_bound(lbl_ref, 0)
    qd, kd, k2, v, g, bl, tril = _hgrn_chunk_terms(x_ref[0], lb, HGRN_CHUNK)
    rows = _iota((HGRN_TILE, HGRN_HEAD_DIM), 0)
    n_chunks = HGRN_TILE // HGRN_CHUNK
    head = [slice(h * HGRN_HEAD_DIM, (h + 1) * HGRN_HEAD_DIM) for h in range(HGRN_HEADS)]
    o_intra = [_hgrn_intra(qd[:, hs], kd[:, hs], v[:, hs], tril) for hs in head]
    qdb = [qd[:, hs].astype(BF16) for hs in head]
    vt = [v[:, hs].T.astype(BF16) for hs in head]
    st = [st_ref[h] for h in range(HGRN_HEADS)]
    o_inter = [[] for _ in head]
    for c in range(n_chunks):
        r0 = c * HGRN_CHUNK
        in_chunk = (rows >= r0) & (rows < r0 + HGRN_CHUNK)
        for h, hs in enumerate(head):
            o_inter[h].append(_dot_nt(qdb[h][r0:r0 + HGRN_CHUNK], st[h].astype(BF16)))
            d_st = _dot(vt[h], jnp.where(in_chunk, k2[:, hs], 0.0).astype(BF16))
            st[h] = st[h] * jnp.exp(bl[r0:r0 + 1, hs]) + d_st
    for h, hs in enumerate(head):
        st_ref[h] = st[h]
        o = o_intra[h] + jnp.concatenate(o_inter[h], axis=0)
        o_ref[0, :, hs] = _hgrn_finish(o, g[:, hs], on_ref[...])

    @pl.when(t == pl.num_programs(1) - 1)
    def _():
        for h in range(HGRN_HEADS):
            s_ref[0, h] = st_ref[h].T


def _hgrn_prompt(lbl, on, hg):
    tt = HGRN_TILE
    return pl.pallas_call(
        _hgrn_prompt_body,
        grid=(BATCH, SEQ // tt),
        in_specs=[_resident(lbl.shape), _resident((1, HGRN_HEAD_DIM)),
                  pl.BlockSpec((1, tt, 4 * HGRN_W), lambda b, t: (b, t, 0))],
        out_specs=[pl.BlockSpec((1, tt, HGRN_W), lambda b, t: (b, t, 0)),
                   pl.BlockSpec((1, HGRN_HEADS, HGRN_HEAD_DIM, HGRN_HEAD_DIM), lambda b, t: (b, 0, 0, 0))],
        out_shape=[jax.ShapeDtypeStruct((BATCH, SEQ, HGRN_W), BF16),
                   jax.ShapeDtypeStruct((BATCH, HGRN_HEADS, HGRN_HEAD_DIM, HGRN_HEAD_DIM), F32)],
        scratch_shapes=[pltpu.VMEM((HGRN_HEADS, HGRN_HEAD_DIM, HGRN_HEAD_DIM), F32)],
        compiler_params=_params("parallel", "arbitrary"),
        name="hgrn_prompt",
    )(lbl, on, hg)


def _hgrn_sample_body(lbl_ref, on_ref, x_ref, s0_ref, o_ref, s_ref):
    bb = SAMPLE_BATCH_TILE
    r = bb * DEC_SEQ
    lb = _hgrn_lower_bound(lbl_ref, 0)
    qd, kd, k2, v, g, bl, tril = _hgrn_chunk_terms(x_ref[...], lb, DEC_SEQ)
    cols = _iota((HGRN_HEAD_DIM, r), 1)
    for h in range(HGRN_HEADS):
        hs = slice(h * HGRN_HEAD_DIM, (h + 1) * HGRN_HEAD_DIM)
        o_intra = _hgrn_intra(qd[:, hs], kd[:, hs], v[:, hs], tril)
        qdb = qd[:, hs].astype(BF16)
        vb = v[:, hs].astype(BF16)
        k2t = k2[:, hs].T
        decay_t = jnp.exp(bl[:, hs].T)
        o_inter = []
        for b in range(bb):
            r0 = b * DEC_SEQ
            s0 = s0_ref[b, h]
            o_inter.append(_dot(qdb[r0:r0 + DEC_SEQ], s0.astype(BF16)))
            in_seq = (cols >= r0) & (cols < r0 + DEC_SEQ)
            d_s = _dot(jnp.where(in_seq, k2t, 0.0).astype(BF16), vb)
            s_ref[b, h] = s0 * decay_t[:, r0:r0 + 1] + d_s
        o = o_intra + jnp.concatenate(o_inter, axis=0)
        o_ref[:, hs] = _hgrn_finish(o, g[:, hs], on_ref[...])


def _hgrn_sample(lbl, on, hg, s0):
    bb = SAMPLE_BATCH_TILE
    r = bb * DEC_SEQ
    st = pl.BlockSpec((bb, HGRN_HEADS, HGRN_HEAD_DIM, HGRN_HEAD_DIM), lambda i: (i, 0, 0, 0))
    return pl.pallas_call(
        _hgrn_sample_body,
        grid=(DEC_BATCH // bb,),
        in_specs=[_resident(lbl.shape), _resident((1, HGRN_HEAD_DIM)),
                  pl.BlockSpec((r, 4 * HGRN_W), lambda i: (i, 0)), st],
        out_specs=[pl.BlockSpec((r, HGRN_W), lambda i: (i, 0)), st],
        out_shape=[jax.ShapeDtypeStruct((DEC_BATCH * DEC_SEQ, HGRN_W), BF16),
                   jax.ShapeDtypeStruct(s0.shape, F32)],
        compiler_params=_params("parallel"),
        name="hgrn_sample",
    )(lbl, on, hg, s0)


FFN_CHUNK = D_FF // 2


def _post_body(n_mix, *refs):
    mix_refs = refs[:n_mix]
    (h_ref, p_ref, wo_ref, g_mix, g_pre, wg_ref, wu_ref, wd_ref, g_post, wpp_ref, wpg_ref, g_ple, o_ref) = refs[n_mix:]
    y = None
    r0 = 0
    for m_ref in mix_refs:
        w = m_ref.shape[-1]
        part = _dot(m_ref[...].astype(BF16), wo_ref[r0:r0 + w, :])
        y = part if y is None else y + part
        r0 += w
    h = h_ref[...] + _rms(y, g_mix[...])
    u = _rms(h, g_pre[...]).astype(BF16)
    ff = None
    for c in range(0, D_FF, FFN_CHUNK):
        gate = _dot(u, wg_ref[:, c:c + FFN_CHUNK])
        up = _dot(u, wu_ref[:, c:c + FFN_CHUNK])
        part = _dot((_silu(gate) * up).astype(BF16), wd_ref[c:c + FFN_CHUNK, :])
        ff = part if ff is None else ff + part
    h = h + _rms(ff, g_post[...])
    e = _dot(p_ref[...].astype(BF16), wpp_ref[...]) * _sigmoid(_dot(h.astype(BF16), wpg_ref[...]))
    o_ref[...] = h + _rms(e, g_ple[...])


def _post(layer, mix, h, p, wo, g_mix, g_pre, wg, wu, wd, g_post, wpp, wpg, g_ple):
    t = h.shape[0]
    tm = TOKEN_TILE
    row = lambda n: pl.BlockSpec((tm, n), lambda i: (i, 0))
    gain = _resident((1, D_MODEL))
    of_layer = lambda w: pl.BlockSpec((None,) + w.shape[1:], lambda i: (layer, 0, 0), pipeline_mode=pl.Buffered(1))
    return pl.pallas_call(
        functools.partial(_post_body, len(mix)),
        grid=(t // tm,),
        in_specs=[row(m.shape[-1]) for m in mix] + [
            row(D_MODEL), pl.BlockSpec((None, tm, PLE_DIM), lambda i: (layer, i, 0)), _resident(wo.shape), gain, gain,
            of_layer(wg), of_layer(wu), of_layer(wd), gain, of_layer(wpp), of_layer(wpg), gain],
        out_specs=row(D_MODEL),
        out_shape=jax.ShapeDtypeStruct((t, D_MODEL), F32),
        compiler_params=_params("parallel"),
        name="post",
    )(*mix, h, p, wo, g_mix, g_pre, wg, wu, wd, g_post, wpp, wpg, g_ple)


def _rope_tables(pos, invf, sgn):
    ang = pos * invf
    return jnp.cos(ang), jnp.sin(ang) * sgn


def _mla_latents(h_ref, g_pre, wdq_ref, g_q, wkv_ref, g_kv, wpe_ref, wpes_ref, cos32, sin32):
    u = _rms(h_ref[...], g_pre[...]).astype(BF16)
    cqn = _rms(_dot(u, wdq_ref[...]), g_q[...]).astype(BF16)
    ckv = _rms(_dot(u, wkv_ref[...]), g_kv[...])
    kpe = _dot(u, wpe_ref[...]) * cos32 + _dot(u, wpes_ref[...]) * sin32
    return cqn, ckv, kpe


def _mla_pre_prompt_body(h_ref, g_pre, wdq_ref, g_q, wq_ref, wkvp_ref, g_kv,
                         wuk_ref, wuvt_ref, ones_ref, invf_ref, sgn_ref,
                         q_ref, k_ref, vt_ref, ckv_ref, kpe_ref):
    tm = TOKEN_TILE
    i = pl.program_id(0)
    pos = ((i * tm + _iota((tm, 1), 0)) & (SEQ - 1)).astype(F32)
    cos128, sin128 = _rope_tables(pos, invf_ref[...], sgn_ref[...])
    rope = slice(MLA_NOPE_DIM, MLA_NOPE_DIM + MLA_ROPE_DIM)
    half = MLA_ROPE_DIM // 2
    first_half = _iota((tm, MLA_HEAD_PAD), 1) < MLA_NOPE_DIM + half

    def swap_rope_halves(x):
        return jnp.where(first_half, pltpu.roll(x, MLA_HEAD_PAD - half, 1), pltpu.roll(x, half, 1))

    u = _rms(h_ref[...], g_pre[...]).astype(BF16)
    cqn = _rms(_dot(u, wdq_ref[...]), g_q[...]).astype(BF16)
    kvp = _dot(u, wkvp_ref[...])
    ckv = _rms(kvp[:, :MLA_KV_RANK], g_kv[...])
    kpe_raw = kvp[:, MLA_KV_RANK:]
    kpe_pad = kpe_raw * cos128 + swap_rope_halves(kpe_raw) * sin128
    ckv_ref[...] = ckv
    kpe_ref[...] = kpe_pad[:, rope]
    ckv_b = ckv.astype(BF16)
    vt_ref[0, 0] = (_dot_nt(wuvt_ref[...], ckv_b) + ones_ref[...]).astype(BF16)
    for h in range(MLA_HEADS):
        hs = slice(h * MLA_HEAD_PAD, (h + 1) * MLA_HEAD_PAD)
        q = _dot(cqn, wq_ref[:, hs])
        q = q * cos128 + swap_rope_halves(q) * sin128
        q_ref[0, h] = (q * (MLA_SCALE * LOG2_E)).astype(BF16)
        k_ref[0, h] = (_dot(ckv_b, wuk_ref[:, hs]) + kpe_pad).astype(BF16)


def _mla_pre_prompt(h, g_pre, wdq, g_q, wq, wkvp, g_kv, wuk, wuvt, ones_col, invf, sgn):
    tm = TOKEN_TILE
    assert tm == MLA_TK
    per_seq = SEQ // tm
    heads = pl.BlockSpec((1, MLA_HEADS, tm, MLA_HEAD_PAD), lambda i: (i // per_seq, 0, i % per_seq, 0))
    head_shape = jax.ShapeDtypeStruct((BATCH, MLA_HEADS, SEQ, MLA_HEAD_PAD), BF16)
    vt_rows = MLA_HEADS * MLA_VT_ROWS
    row = lambda n: pl.BlockSpec((tm, n), lambda i: (i, 0))
    consts = [g_pre, wdq, g_q, wq, wkvp, g_kv, wuk, wuvt, ones_col, invf, sgn]
    return pl.pallas_call(
        _mla_pre_prompt_body,
        grid=(h.shape[0] // tm,),
        in_specs=[row(D_MODEL)] + [_resident(c.shape) for c in consts],
        out_specs=[heads, heads,
                   pl.BlockSpec((1, 1, vt_rows, tm), lambda i: (i // per_seq, i % per_seq, 0, 0)),
                   row(MLA_KV_RANK), row(MLA_ROPE_DIM)],
        out_shape=[head_shape, head_shape,
                   jax.ShapeDtypeStruct((BATCH, per_seq, vt_rows, tm), BF16),
                   jax.ShapeDtypeStruct((h.shape[0], MLA_KV_RANK), F32),
                   jax.ShapeDtypeStruct((h.shape[0], MLA_ROPE_DIM), F32)],
        compiler_params=_params("parallel"),
        name="mla_pre_prompt",
    )(h, *consts)


def _mla_prompt_attn_body(q_ref, k_ref, vt_ref, o_ref, m_ref, acc_ref):
    i = pl.program_id(2)
    m_ref[...] = jnp.full_like(m_ref, NEG)
    acc_ref[...] = jnp.zeros_like(acc_ref)
    causal = _iota((MLA_TK, MLA_TK), 0) <= _iota((MLA_TK, MLA_TK), 1)
    splits = MLA_TQ // MLA_TK

    def kv_tile(j, units):
        k0 = pl.multiple_of(j * MLA_TK, MLA_TK)
        cols = lambda blk: slice(blk * MLA_TK, (blk + 1) * MLA_TK)

        def scores(hh, blk, masked):
            s = _dot_nt(k_ref[0, hh, pl.ds(k0, MLA_TK), :], q_ref[0, hh, cols(blk), :])
            return jnp.where(causal, s, NEG) if masked else s

        def softmax(hh, blk, s):
            m_prev = m_ref[hh, :, cols(blk)]
            m_new = jnp.maximum(m_prev, jnp.max(s, axis=0, keepdims=True))
            m_ref[hh, :, cols(blk)] = m_new
            return jnp.exp2(s - m_new).astype(BF16), jnp.exp2(m_prev - m_new)

        def values(hh, blk, p, alpha):
            vt = vt_ref[0, j, hh * MLA_VT_ROWS:(hh + 1) * MLA_VT_ROWS, :]
            acc_ref[hh, :, cols(blk)] = alpha * acc_ref[hh, :, cols(blk)] + _dot(vt, p)

        n = len(units)
        s_of, p_of = {}, {}
        for step in range(n + 2):
            if step < n:
                s_of[step] = scores(*units[step])
            if 0 <= step - 1 < n:
                hh, blk, _ = units[step - 1]
                p_of[step - 1] = softmax(hh, blk, s_of.pop(step - 1))
            if 0 <= step - 2 < n:
                hh, blk, _ = units[step - 2]
                values(hh, blk, *p_of.pop(step - 2))

    heads = range(MLA_HEAD_GROUP)

    def below_diagonal(j, carry):
        kv_tile(j, [(hh, blk, False) for hh in heads for blk in range(splits)])
        return carry

    lax.fori_loop(0, i * splits, below_diagonal, 0)
    for d in range(splits):
        kv_tile(i * splits + d, [(hh, blk, blk == d) for hh in heads for blk in range(d, splits)])
    for pair in range(MLA_HEAD_GROUP // 2):
        heads = []
        for hh in (2 * pair, 2 * pair + 1):
            acc = acc_ref[hh]
            heads.append(acc[:MLA_V_DIM] / acc[MLA_V_DIM:MLA_V_DIM + 1])
        o_t = jnp.concatenate(heads, axis=0)
        o_ref[0, :, pair * 2 * MLA_V_DIM:(pair + 1) * 2 * MLA_V_DIM] = o_t.T.astype(BF16)


def _mla_prompt_attn(q, k, vt):
    hg = MLA_HEAD_GROUP
    assert MLA_TQ % MLA_TK == 0 and hg % 2 == 0
    return pl.pallas_call(
        _mla_prompt_attn_body,
        grid=(BATCH, MLA_HEADS // hg, SEQ // MLA_TQ),
        in_specs=[pl.BlockSpec((1, hg, MLA_TQ, MLA_HEAD_PAD), lambda b, g, i: (b, g, i, 0)),
                  pl.BlockSpec((1, hg, SEQ, MLA_HEAD_PAD), lambda b, g, i: (b, g, 0, 0),
                               pipeline_mode=pl.Buffered(1)),
                  pl.BlockSpec((1, SEQ // MLA_TK, hg * MLA_VT_ROWS, MLA_TK), lambda b, g, i: (b, 0, g, 0),
                               pipeline_mode=pl.Buffered(1))],
        out_specs=pl.BlockSpec((1, MLA_TQ, hg * MLA_V_DIM), lambda b, g, i: (b, i, g)),
        out_shape=jax.ShapeDtypeStruct((BATCH, SEQ, MLA_HEADS * MLA_V_DIM), BF16),
        scratch_shapes=[pltpu.VMEM((hg, 1, MLA_TQ), F32), pltpu.VMEM((hg, MLA_VT_ROWS, MLA_TQ), F32)],
        compiler_params=_params("parallel", "parallel", "arbitrary"),
        name="mla_prompt_attn",
    )(q, k, vt)


def _mla_pre_sample_body(h_ref, g_pre, wdq_ref, g_q, wn_ref, wr_ref, wrs_ref, wukt_ref, wkv_ref, g_kv,
                         wpe_ref, wpes_ref, invf_ref, sgn_ref,
                         ql_ref, qp_ref, ckv_ref, kpe_ref):
    tm = TOKEN_TILE
    pos = (PAST_LEN + (_iota((tm, 1), 0) & (DEC_SEQ - 1))).astype(F32)
    cos32, sin32 = _rope_tables(pos, invf_ref[...], sgn_ref[...])
    cqn, ckv, kpe = _mla_latents(h_ref, g_pre, wdq_ref, g_q, wkv_ref, g_kv, wpe_ref, wpes_ref, cos32, sin32)
    ckv_ref[...] = ckv
    kpe_ref[...] = kpe
    n_seq = tm // DEC_SEQ
    for h in range(MLA_HEADS):
        rows = slice(h * DEC_SEQ, (h + 1) * DEC_SEQ)
        q_nope = _dot(cqn, wn_ref[h]).astype(BF16)
        q_lat = _dot(q_nope, wukt_ref[h]) * (MLA_SCALE * LOG2_E)
        ql_ref[:, rows, :] = q_lat.reshape(n_seq, DEC_SEQ, MLA_KV_RANK)
        q_pe = (_dot(cqn, wr_ref[h]) * cos32 + _dot(cqn, wrs_ref[h]) * sin32) * (MLA_SCALE * LOG2_E)
        qp_ref[:, rows, :] = q_pe.reshape(n_seq, DEC_SEQ, MLA_ROPE_DIM)


def _mla_pre_sample(h, g_pre, wdq, g_q, wn, wr, wrs, wukt, wkv, g_kv, wpe, wpes, invf, sgn):
    tm = TOKEN_TILE
    n_seq = tm // DEC_SEQ
    rows = MLA_HEADS * DEC_SEQ
    row = lambda n: pl.BlockSpec((tm, n), lambda i: (i, 0))
    consts = [g_pre, wdq, g_q, wn, wr, wrs, wukt, wkv, g_kv, wpe, wpes, invf, sgn]
    return pl.pallas_call(
        _mla_pre_sample_body,
        grid=(h.shape[0] // tm,),
        in_specs=[row(D_MODEL)] + [_resident(c.shape) for c in consts],
        out_specs=[pl.BlockSpec((n_seq, rows, MLA_KV_RANK), lambda i: (i, 0, 0)),
                   pl.BlockSpec((n_seq, rows, MLA_ROPE_DIM), lambda i: (i, 0, 0)),
                   row(MLA_KV_RANK), row(MLA_ROPE_DIM)],
        out_shape=[jax.ShapeDtypeStruct((DEC_BATCH, rows, MLA_KV_RANK), F32),
                   jax.ShapeDtypeStruct((DEC_BATCH, rows, MLA_ROPE_DIM), F32),
                   jax.ShapeDtypeStruct((h.shape[0], MLA_KV_RANK), F32),
                   jax.ShapeDtypeStruct((h.shape[0], MLA_ROPE_DIM), F32)],
        compiler_params=_params("parallel"),
        name="mla_pre_sample",
    )(h, *consts)


def _mla_sample_attn_body(pt_ref, ql_ref, qp_ref, cn_ref, pn_ref, ckv_hbm, kpe_hbm, o_ref,
                          m_ref, l_ref, acc_ref, cnp_ref, pnp_ref, ck_buf, kp_buf, ck_sem, kp_sem):
    npg = PAGES_PER_GROUP
    groups_per_seq = N_PAGES // npg
    total_groups = (DEC_BATCH // SAMPLE_STREAMS) * groups_per_seq
    b = pl.program_id(0)
    rows = MLA_HEADS * DEC_SEQ
    streams = range(SAMPLE_STREAMS)

    def page_copies(page_of, st, slot):
        copies = []
        for j in range(npg):
            page = page_of(j)
            copies.append(pltpu.make_async_copy(
                ckv_hbm.at[0, page], ck_buf.at[st, slot, pl.ds(j * PAGE_SIZE, PAGE_SIZE), :], ck_sem.at[st, slot]))
            copies.append(pltpu.make_async_copy(kpe_hbm.at[0, page], kp_buf.at[st, slot, j], kp_sem.at[st, slot]))
        return copies

    def start_group(t, slot):
        step, g = t // groups_per_seq, t % groups_per_seq
        for st in streams:
            base = (step * SAMPLE_STREAMS + st) * N_PAGES + g * npg
            for c in page_copies(lambda j: pt_ref[base + j], st, slot):
                c.start()

    def wait_group(slot):
        for st in streams:
            for c in page_copies(lambda j: 0, st, slot):
                c.wait()

    @pl.when(b == 0)
    def _():
        for t0 in range(PAGE_RING):
            start_group(t0, t0)

    ql = [ql_ref[st].astype(BF16) for st in streams]
    qp = [qp_ref[st].astype(BF16) for st in streams]

    cnp_ref[...] = jnp.zeros_like(cnp_ref)
    pnp_ref[...] = jnp.zeros_like(pnp_ref)
    cnp_ref[:, :DEC_SEQ, :] = cn_ref[...]
    pnp_ref[:, :DEC_SEQ, :] = pn_ref[...]
    t_q = _iota((rows, PAGE_SIZE), 0) & (DEC_SEQ - 1)
    new_valid = _iota((rows, PAGE_SIZE), 1) <= t_q
    cn = [cnp_ref[st].astype(BF16) for st in streams]
    s_new = [jnp.where(new_valid, _dot_nt(ql[st], cn[st]) + _dot_nt(qp[st], pnp_ref[st].astype(BF16)), NEG)
             for st in streams]

    def update(st, s, keys, first):
        m_cur = jnp.max(s, axis=-1, keepdims=True)
        if first:
            p = jnp.exp2(s - m_cur)
            m_ref[st] = m_cur
            l_ref[st] = jnp.sum(p, axis=-1, keepdims=True)
            acc_ref[st] = _dot(p.astype(BF16), keys)
            return
        m_prev = m_ref[st]
        m_new = jnp.maximum(m_prev, m_cur)
        alpha = jnp.exp2(m_prev - m_new)
        p = jnp.exp2(s - m_new)
        l_ref[st] = alpha * l_ref[st] + jnp.sum(p, axis=-1, keepdims=True)
        acc_ref[st] = alpha * acc_ref[st] + _dot(p.astype(BF16), keys)
        m_ref[st] = m_new

    def group(g, first):
        t = b * groups_per_seq + g
        slot = t % PAGE_RING
        wait_group(slot)

        def scores(st, c):
            pages = range(c * PAGES_PER_CHUNK, (c + 1) * PAGES_PER_CHUNK)
            kl = ck_buf[st, slot, pages[0] * PAGE_SIZE:(pages[-1] + 1) * PAGE_SIZE, :].astype(BF16)
            kpt = jnp.concatenate([kp_buf[st, slot, j].astype(BF16) for j in pages], axis=1)
            s = _dot_nt(ql[st], kl) + _dot(qp[st], kpt)
            if first and c == 0:
                return jnp.concatenate([s, s_new[st]], axis=1), jnp.concatenate([kl, cn[st]], axis=0)
            return s, kl

        units = [(st, c) for c in range(npg // PAGES_PER_CHUNK) for st in streams]
        nxt = scores(*units[0])
        for k, (st, c) in enumerate(units):
            cur = nxt
            if k + 1 < len(units):
                nxt = scores(*units[k + 1])
            update(st, *cur, first and c == 0)

        start_group(jnp.minimum(t + PAGE_RING, total_groups - 1), slot)

    group(0, True)

    def later_group(g, carry):
        group(g, False)
        return carry

    lax.fori_loop(1, groups_per_seq, later_group, 0)
    o_ref[...] = (acc_ref[...] / l_ref[...]).astype(BF16)

    @pl.when(b == pl.num_programs(0) - 1)
    def _():
        for slot in range(PAGE_RING):
            wait_group(slot)


def _mla_sample_attn(page_table, ql, qp, ckv_new, kpe_new, ckv_pool, kpe_pool):
    npg = PAGES_PER_GROUP
    ns = SAMPLE_STREAMS
    rows = MLA_HEADS * DEC_SEQ
    assert N_PAGES % npg == 0 and npg % PAGES_PER_CHUNK == 0 and DEC_BATCH % ns == 0
    seq = lambda r, w: pl.BlockSpec((ns, r, w), lambda b, pt: (b, 0, 0))
    hbm = pl.BlockSpec(memory_space=pl.ANY)
    grid_spec = pltpu.PrefetchScalarGridSpec(
        num_scalar_prefetch=1,
        grid=(DEC_BATCH // ns,),
        in_specs=[seq(rows, MLA_KV_RANK), seq(rows, MLA_ROPE_DIM), seq(DEC_SEQ, MLA_KV_RANK),
                  seq(DEC_SEQ, MLA_ROPE_DIM), hbm, hbm],
        out_specs=seq(rows, MLA_KV_RANK),
        scratch_shapes=[pltpu.VMEM((ns, rows, 1), F32), pltpu.VMEM((ns, rows, 1), F32),
                        pltpu.VMEM((ns, rows, MLA_KV_RANK), F32),
                        pltpu.VMEM((ns, PAGE_SIZE, MLA_KV_RANK), F32), pltpu.VMEM((ns, PAGE_SIZE, MLA_ROPE_DIM), F32),
                        pltpu.VMEM((ns, PAGE_RING, npg * PAGE_SIZE, MLA_KV_RANK), F32),
                        pltpu.VMEM((ns, PAGE_RING, npg, MLA_ROPE_DIM, PAGE_SIZE), F32),
                        pltpu.SemaphoreType.DMA((ns, PAGE_RING)), pltpu.SemaphoreType.DMA((ns, PAGE_RING))],
    )
    return pl.pallas_call(
        _mla_sample_attn_body,
        grid_spec=grid_spec,
        out_shape=jax.ShapeDtypeStruct((DEC_BATCH, rows, MLA_KV_RANK), BF16),
        compiler_params=_params("arbitrary"),
        name="mla_sample_attn",
    )(page_table, ql, qp, ckv_new, kpe_new, ckv_pool, kpe_pool)


def _mla_sample_out_body(x_ref, w_ref, o_ref):
    x = x_ref[...].reshape(DEC_BATCH * 2 * DEC_SEQ, MLA_KV_RANK)
    y = _dot(x, w_ref[0]).reshape(DEC_BATCH, 2 * DEC_SEQ, 2 * MLA_V_DIM)
    first_head = _iota((DEC_BATCH, DEC_SEQ, 2 * MLA_V_DIM), 2) < MLA_V_DIM
    o = jnp.where(first_head, y[:, :DEC_SEQ], y[:, DEC_SEQ:])
    o_ref[...] = o.reshape(DEC_BATCH * DEC_SEQ, 2 * MLA_V_DIM)


def _mla_sample_out(o_lat, wuv_pairs):
    return pl.pallas_call(
        _mla_sample_out_body,
        grid=(MLA_HEADS // 2,),
        in_specs=[pl.BlockSpec((DEC_BATCH, 2 * DEC_SEQ, MLA_KV_RANK), lambda j: (0, j, 0)),
                  pl.BlockSpec((1, MLA_KV_RANK, 2 * MLA_V_DIM), lambda j: (j, 0, 0))],
        out_specs=pl.BlockSpec((DEC_BATCH * DEC_SEQ, 2 * MLA_V_DIM), lambda j: (0, j)),
        out_shape=jax.ShapeDtypeStruct((DEC_BATCH * DEC_SEQ, MLA_HEADS * MLA_V_DIM), F32),
        compiler_params=_params("parallel"),
        name="mla_sample_out",
    )(o_lat, wuv_pairs)


def _swap_halves(w):
    half = w.shape[-1] // 2
    return jnp.concatenate([w[..., half:], w[..., :half]], axis=-1)


def _rope_consts():
    half = MLA_ROPE_DIM // 2
    inv_freq = ROPE_THETA ** (-jnp.arange(0, half, dtype=F32) * 2.0 / MLA_ROPE_DIM)
    invf32 = jnp.concatenate([inv_freq, inv_freq])[None]
    sgn32 = jnp.concatenate([-jnp.ones((half,), F32), jnp.ones((half,), F32)])[None]
    pad = lambda a: jnp.pad(a, ((0, 0), (MLA_NOPE_DIM, MLA_HEAD_PAD - MLA_NOPE_DIM - MLA_ROPE_DIM)))
    return invf32, sgn32, pad(invf32), pad(sgn32)


def _mla_weights(w_uq, w_dkv, w_uk, w_uv):
    wq = w_uq.reshape(MLA_Q_RANK, MLA_HEADS, MLA_NOPE_DIM + MLA_ROPE_DIM)
    wq_nope, wq_rope = wq[..., :MLA_NOPE_DIM], wq[..., MLA_NOPE_DIM:]
    tail = MLA_HEAD_PAD - MLA_NOPE_DIM - MLA_ROPE_DIM
    pad_last = lambda a, lo, hi: jnp.pad(a, [(0, 0)] * (a.ndim - 1) + [(lo, hi)])
    flat = lambda a: a.reshape(a.shape[0], MLA_HEADS * MLA_HEAD_PAD).astype(BF16)
    w = {}
    w["wq"] = flat(pad_last(wq, 0, tail))
    w["wkvp"] = jnp.concatenate([w_dkv[:, :MLA_KV_RANK], pad_last(w_dkv[:, MLA_KV_RANK:], MLA_NOPE_DIM, tail)],
                                axis=1).astype(BF16)
    w["wuk"] = flat(pad_last(w_uk, 0, MLA_HEAD_PAD - MLA_NOPE_DIM))
    wuvt = jnp.transpose(w_uv, (1, 2, 0))
    w["wuvt"] = jnp.pad(wuvt, ((0, 0), (0, MLA_VT_ROWS - MLA_V_DIM), (0, 0))).reshape(
        MLA_HEADS * MLA_VT_ROWS, MLA_KV_RANK).astype(BF16)
    w["ones_col"] = jnp.tile((jnp.arange(MLA_VT_ROWS) == MLA_V_DIM).astype(F32), MLA_HEADS)[:, None]
    w["wn"] = jnp.moveaxis(wq_nope, 1, 0).astype(BF16)
    w["wr"] = jnp.moveaxis(wq_rope, 1, 0).astype(BF16)
    w["wrs"] = jnp.moveaxis(_swap_halves(wq_rope), 1, 0).astype(BF16)
    w["wukt"] = jnp.transpose(w_uk, (1, 2, 0)).astype(BF16)
    w["wkv"] = w_dkv[:, :MLA_KV_RANK].astype(BF16)
    w["wpe"] = w_dkv[:, MLA_KV_RANK:].astype(BF16)
    w["wpes"] = _swap_halves(w_dkv[:, MLA_KV_RANK:]).astype(BF16)
    w["wuv_pairs"] = jnp.transpose(w_uv.reshape(MLA_KV_RANK, MLA_HEADS // 2, 2 * MLA_V_DIM),
                                   (1, 0, 2)).astype(BF16)
    return w


def kernel(x_prompt, x_sample, cache_swa_k, cache_swa_v, state_hgrn, cache_mla_ckv, cache_mla_kpe, page_table, p_prompt, p_sample, ln_mix_pre, ln_mix_post, ln_ffn_pre, ln_ffn_post, ln_ple, w_ab_in, w_ab_out, swa_sinks, hgrn_lb_logits, hgrn_out_norm, w_mla_dq, mla_q_norm, w_mla_uq, w_mla_dkv, mla_kv_norm, w_mla_uk, w_mla_uv, w_mla_o, w_ffn_gate, w_ffn_up, w_ffn_down, w_ple_proj, w_ple_gate):
    tp = BATCH * SEQ
    ts = DEC_BATCH * DEC_SEQ
    bf = lambda a: a.astype(BF16)
    gain = lambda a, i: a[i][None].astype(F32)

    ffn_w = [bf(w) for w in (w_ffn_gate, w_ffn_up, w_ffn_down)]
    ple_w = [bf(w) for w in (w_ple_proj, w_ple_gate)]

    def finish(mix, h, p, i, wo):
        return _post(i, mix, h, p, wo, gain(ln_mix_post, i), gain(ln_ffn_pre, i), *ffn_w, gain(ln_ffn_post, i),
                     *ple_w, gain(ln_ple, i))

    hp = x_prompt.reshape(tp, D_MODEL)
    hs = x_sample.reshape(ts, D_MODEL)
    pp = p_prompt.reshape(2, tp, PLE_DIM)
    ps = p_sample.reshape(2, ts, PLE_DIM)

    w_in = bf(w_ab_in[0])
    w_out = bf(w_ab_out[0])
    g0 = gain(ln_mix_pre, 0)
    sinks = swa_sinks[0].astype(F32)
    lbl = hgrn_lb_logits.astype(F32)
    on = hgrn_out_norm[0][None].astype(F32)

    qa, ka, va, hg = _ab_in(hp, g0, w_in, BF16)
    oa = _swa_prompt(sinks, qa.reshape(BATCH, SEQ, SWA_Q_W), ka.reshape(BATCH, SEQ, SWA_KV_W),
                     va.reshape(BATCH, SEQ, SWA_KV_W))
    ob, hgrn_p = _hgrn_prompt(lbl, on, hg.reshape(BATCH, SEQ, 4 * HGRN_W))
    swa_k_p = ka.reshape(BATCH, SEQ, SWA_KV_HEADS, SWA_HEAD_DIM)[:, -WINDOW:]
    swa_v_p = va.reshape(BATCH, SEQ, SWA_KV_HEADS, SWA_HEAD_DIM)[:, -WINDOW:]
    hp = finish([oa.reshape(tp, SWA_Q_W), ob.reshape(tp, HGRN_W)], hp, pp, 0, w_out)

    qa, ka, va, hg = _ab_in(hs, g0, w_in, F32)
    oa, swa_k_s, swa_v_s = _swa_sample(
        sinks, qa, ka.reshape(DEC_BATCH, DEC_SEQ, SWA_KV_W),
        va.reshape(DEC_BATCH, DEC_SEQ, SWA_KV_W), cache_swa_k[0].reshape(DEC_BATCH, WINDOW, SWA_KV_W),
        cache_swa_v[0].reshape(DEC_BATCH, WINDOW, SWA_KV_W))
    ob, hgrn_s = _hgrn_sample(lbl, on, hg, state_hgrn[0])
    hs = finish([oa.reshape(ts, SWA_Q_W), ob], hs, ps, 0, w_out)

    mw = _mla_weights(w_mla_uq[0], w_mla_dkv[0], w_mla_uk[0], w_mla_uv[0])
    invf32, sgn32, invf128, sgn128 = _rope_consts()
    g1 = gain(ln_mix_pre, 1)
    wdq = bf(w_mla_dq[0])
    g_q = mla_q_norm[0][None].astype(F32)
    g_kv = mla_kv_norm[0][None].astype(F32)
    w_o = bf(w_mla_o[0])

    q, k, vt, ckv_p, kpe_p = _mla_pre_prompt(hp, g1, wdq, g_q, mw["wq"], mw["wkvp"], g_kv, mw["wuk"], mw["wuvt"],
                                             mw["ones_col"], invf128, sgn128)
    o = _mla_prompt_attn(q, k, vt)
    hp = finish([o.reshape(tp, MLA_HEADS * MLA_V_DIM)], hp, pp, 1, w_o)

    ql, qp, ckv_s, kpe_s = _mla_pre_sample(hs, g1, wdq, g_q, mw["wn"], mw["wr"], mw["wrs"], mw["wukt"], mw["wkv"],
                                           g_kv, mw["wpe"], mw["wpes"], invf32, sgn32)
    o_lat = _mla_sample_attn(page_table.reshape(-1), ql, qp, ckv_s.reshape(DEC_BATCH, DEC_SEQ, MLA_KV_RANK),
                             kpe_s.reshape(DEC_BATCH, DEC_SEQ, MLA_ROPE_DIM), cache_mla_ckv,
                             jnp.swapaxes(cache_mla_kpe, 2, 3))
    o = _mla_sample_out(o_lat, mw["wuv_pairs"])
    hs = finish([o], hs, ps, 1, w_o)

    kv5 = lambda a, n: a.reshape(1, n, WINDOW, SWA_KV_HEADS, SWA_HEAD_DIM)
    return (hp.reshape(BATCH, SEQ, D_MODEL), hs.reshape(DEC_BATCH, DEC_SEQ, D_MODEL),
            swa_k_p[None], swa_v_p[None], hgrn_p[None],
            ckv_p.reshape(1, BATCH, SEQ, MLA_KV_RANK), kpe_p.reshape(1, BATCH, SEQ, MLA_ROPE_DIM),
            kv5(swa_k_s, DEC_BATCH), kv5(swa_v_s, DEC_BATCH), hgrn_s[None],
            ckv_s.reshape(1, DEC_BATCH, DEC_SEQ, MLA_KV_RANK), kpe_s.reshape(1, DEC_BATCH, DEC_SEQ, MLA_ROPE_DIM))
```

```python
import functools

import jax
import jax.numpy as jnp
from jax import lax
from jax.experimental import pallas as pl
from jax.experimental.pallas import tpu as pltpu

F32 = jnp.float32
BF16 = jnp.bfloat16

D_MODEL = 1024
BATCH = 2
SEQ = 8192
DEC_BATCH = 128
DEC_SEQ = 8
PAST_LEN = 16384
PAGE_SIZE = 128
N_PAGES = PAST_LEN // PAGE_SIZE

SWA_HEADS = 8
SWA_KV_HEADS = 2
SWA_GROUP = SWA_HEADS // SWA_KV_HEADS
SWA_HEAD_DIM = 64
WINDOW = 128
SWA_SCALE = SWA_HEAD_DIM ** -0.5
SWA_Q_W = SWA_HEADS * SWA_HEAD_DIM
SWA_KV_W = SWA_KV_HEADS * SWA_HEAD_DIM

HGRN_HEADS = 4
HGRN_HEAD_DIM = 128
HGRN_CHUNK = 32
HGRN_W = HGRN_HEADS * HGRN_HEAD_DIM
AB_IN_W = SWA_Q_W + 2 * SWA_KV_W + 4 * HGRN_W

MLA_HEADS = 16
MLA_NOPE_DIM = 64
MLA_ROPE_DIM = 32
MLA_V_DIM = 64
MLA_Q_RANK = 512
MLA_KV_RANK = 256
MLA_SCALE = (MLA_NOPE_DIM + MLA_ROPE_DIM) ** -0.5
ROPE_THETA = 10000.0
MLA_HEAD_PAD = 128

D_FF = 2816
PLE_DIM = 256
NORM_EPS = 1e-6
NEG = -1e30
LOG2_E = 1.4426950408889634

VMEM_LIMIT_BYTES = 56 * 1024 * 1024

TOKEN_TILE = 512
HGRN_TILE = 256
SWA_TQ = 256
SAMPLE_BATCH_TILE = 16
MLA_TQ = 1024
MLA_TK = 512
MLA_HEAD_GROUP = 8
MLA_VT_ROWS = 80
PAGES_PER_GROUP = 16
PAGES_PER_CHUNK = 16
SAMPLE_STREAMS = 4
PAGE_RING = 4


def _dot(a, b):
    return jnp.dot(a, b, preferred_element_type=F32)


def _dot_nt(a, b):
    return lax.dot_general(a, b, (((1,), (1,)), ((), ())), preferred_element_type=F32)


def _rms(x, g):
    return x * lax.rsqrt(jnp.mean(x * x, axis=-1, keepdims=True) + NORM_EPS) * g


def _sigmoid(x):
    return 1.0 / (1.0 + jnp.exp(-x))


def _silu(x):
    return x * _sigmoid(x)


def _iota(shape, dim):
    return lax.broadcasted_iota(jnp.int32, shape, dim)


def _params(*sem):
    return pltpu.CompilerParams(dimension_semantics=sem, vmem_limit_bytes=VMEM_LIMIT_BYTES)


def _resident(shape):
    nd = len(shape)
    return pl.BlockSpec(shape, lambda *_: (0,) * nd, pipeline_mode=pl.Buffered(1))


def _smem():
    return pl.BlockSpec(memory_space=pltpu.SMEM)


def _ab_in_body(h_ref, g_ref, w_ref, qa_ref, k_ref, v_ref, hg_ref):
    u = _rms(h_ref[...], g_ref[...]).astype(BF16)
    qa_ref[...] = (_dot(u, w_ref[:, :SWA_Q_W]) * SWA_SCALE).astype(qa_ref.dtype)
    k_ref[...] = _dot(u, w_ref[:, SWA_Q_W:SWA_Q_W + SWA_KV_W])
    v_ref[...] = _dot(u, w_ref[:, SWA_Q_W + SWA_KV_W:SWA_Q_W + 2 * SWA_KV_W])
    hg_ref[...] = _dot(u, w_ref[:, SWA_Q_W + 2 * SWA_KV_W:])


def _ab_in(h, g, w, q_dtype):
    t = h.shape[0]
    tm = TOKEN_TILE
    row = lambda n: pl.BlockSpec((tm, n), lambda i: (i, 0))
    return pl.pallas_call(
        _ab_in_body,
        grid=(t // tm,),
        in_specs=[row(D_MODEL), _resident((1, D_MODEL)), _resident((D_MODEL, AB_IN_W))],
        out_specs=[row(SWA_Q_W), row(SWA_KV_W), row(SWA_KV_W), row(4 * HGRN_W)],
        out_shape=[jax.ShapeDtypeStruct((t, SWA_Q_W), q_dtype),
                   jax.ShapeDtypeStruct((t, SWA_KV_W), F32),
                   jax.ShapeDtypeStruct((t, SWA_KV_W), F32),
                   jax.ShapeDtypeStruct((t, 4 * HGRN_W), F32)],
        compiler_params=_params("parallel"),
        name="ab_in",
    )(h, g, w)


def _swa_head(q, keys, vals, dists, valids, slope, sink):
    scores = []
    for k, dist, valid in zip(keys, dists, valids):
        s = _dot_nt(q, k) - slope * dist
        scores.append(jnp.where(valid, s, NEG))
    m = sink
    for s in scores:
        m = jnp.maximum(m, jnp.max(s, axis=-1, keepdims=True))
    den = jnp.exp(sink - m)
    acc = None
    for s, v in zip(scores, vals):
        p = jnp.exp(s - m)
        den = den + jnp.sum(p, axis=-1, keepdims=True)
        pv = _dot(p.astype(BF16), v)
        acc = pv if acc is None else acc + pv
    return acc / den


def _swa_prompt_body(sink_ref, q_ref, kc_ref, kp_ref, vc_ref, vp_ref, o_ref):
    i = pl.program_id(1)
    nk = WINDOW + SWA_TQ
    kk = jnp.concatenate([kp_ref[0], kc_ref[0]], axis=0).astype(BF16)
    vv_t = jnp.concatenate([vp_ref[0], vc_ref[0]], axis=0).T.astype(BF16)
    key = _iota((nk, SWA_TQ), 0)
    qry = _iota((nk, SWA_TQ), 1)
    dist_i = qry + WINDOW - key
    valid = (dist_i >= 0) & (dist_i <= WINDOW) & (key >= jnp.where(i > 0, 0, WINDOW))
    dist = dist_i.astype(F32)
    q = q_ref[0]
    kv_lanes = lambda j: slice(j // SWA_GROUP * SWA_HEAD_DIM, (j // SWA_GROUP + 1) * SWA_HEAD_DIM)

    def scores(j):
        s = _dot_nt(kk[:, kv_lanes(j)], q[:, j * SWA_HEAD_DIM:(j + 1) * SWA_HEAD_DIM]) - 2.0 ** -(j + 1) * dist
        return jnp.where(valid, s, NEG)

    def attend(j, s):
        m = jnp.maximum(jnp.max(s, axis=0, keepdims=True), sink_ref[j])
        p = jnp.exp(s - m)
        den = jnp.sum(p, axis=0, keepdims=True) + jnp.exp(sink_ref[j] - m)
        return _dot(vv_t[kv_lanes(j), :], p.astype(BF16)) / den

    heads = []
    nxt = scores(0)
    for j in range(SWA_HEADS):
        cur = nxt
        if j + 1 < SWA_HEADS:
            nxt = scores(j + 1)
        heads.append(attend(j, cur))
    for pair in range(SWA_HEADS // 2):
        o_t = jnp.concatenate(heads[2 * pair:2 * pair + 2], axis=0)
        o_ref[0, :, pair * 2 * SWA_HEAD_DIM:(pair + 1) * 2 * SWA_HEAD_DIM] = o_t.T.astype(BF16)


def _swa_prompt(sinks, qa, k, v):
    per_tile = SWA_TQ // WINDOW
    cur = lambda w: pl.BlockSpec((1, SWA_TQ, w), lambda b, i: (b, i, 0))
    prev = lambda w: pl.BlockSpec((1, WINDOW, w), lambda b, i: (b, jnp.maximum(i * per_tile - 1, 0), 0))
    return pl.pallas_call(
        _swa_prompt_body,
        grid=(BATCH, SEQ // SWA_TQ),
        in_specs=[_smem(), cur(SWA_Q_W), cur(SWA_KV_W), prev(SWA_KV_W), cur(SWA_KV_W), prev(SWA_KV_W)],
        out_specs=cur(SWA_Q_W),
        out_shape=jax.ShapeDtypeStruct((BATCH, SEQ, SWA_Q_W), BF16),
        compiler_params=_params("parallel", "arbitrary"),
        name="swa_prompt",
    )(sinks, qa, k, k, v, v)


def _swa_sample_body(sink_ref, q_ref, kn_ref, vn_ref, kb_ref, vb_ref, o_ref, ko_ref, vo_ref):
    bb = SAMPLE_BATCH_TILE
    t_new = DEC_SEQ
    r = bb * t_new
    nk = bb * WINDOW
    kb3, vb3, kn3, vn3 = kb_ref[...], vb_ref[...], kn_ref[...], vn_ref[...]
    ko_ref[:, :WINDOW - t_new, :] = kb3[:, t_new:, :]
    ko_ref[:, WINDOW - t_new:, :] = kn3
    vo_ref[:, :WINDOW - t_new, :] = vb3[:, t_new:, :]
    vo_ref[:, WINDOW - t_new:, :] = vn3
    kb = kb3.reshape(nk, SWA_KV_W).astype(BF16)
    vb = vb3.reshape(nk, SWA_KV_W).astype(BF16)
    kn = kn3.reshape(r, SWA_KV_W).astype(BF16)
    vn = vn3.reshape(r, SWA_KV_W).astype(BF16)

    row = _iota((r, nk), 0)
    col = _iota((r, nk), 1)
    t_q = row & (t_new - 1)
    slot = col & (WINDOW - 1)
    dist_b = (t_q - slot + WINDOW).astype(F32)
    valid_b = ((row >> 3) == (col >> 7)) & (slot >= t_q)
    row_n = _iota((r, r), 0)
    col_n = _iota((r, r), 1)
    dist_n = ((row_n & (t_new - 1)) - (col_n & (t_new - 1))).astype(F32)
    valid_n = ((row_n >> 3) == (col_n >> 3)) & (col_n <= row_n)
    q = q_ref[...].astype(BF16)
    for hk in range(SWA_KV_HEADS):
        ks = slice(hk * SWA_HEAD_DIM, (hk + 1) * SWA_HEAD_DIM)
        for g in range(SWA_GROUP):
            j = hk * SWA_GROUP + g
            qs = slice(j * SWA_HEAD_DIM, (j + 1) * SWA_HEAD_DIM)
            o_ref[:, qs] = _swa_head(q[:, qs], (kb[:, ks], kn[:, ks]), (vb[:, ks], vn[:, ks]),
                                     (dist_b, dist_n), (valid_b, valid_n), 2.0 ** -(j + 1), sink_ref[j])


def _swa_sample(sinks, qa, kn, vn, kbuf, vbuf):
    bb = SAMPLE_BATCH_TILE
    assert DEC_SEQ == 8 and WINDOW == 128
    blk = lambda r, w: pl.BlockSpec((bb, r, w), lambda i: (i, 0, 0))
    rows = pl.BlockSpec((bb * DEC_SEQ, SWA_Q_W), lambda i: (i, 0))
    return pl.pallas_call(
        _swa_sample_body,
        grid=(DEC_BATCH // bb,),
        in_specs=[_smem(), rows, blk(DEC_SEQ, SWA_KV_W), blk(DEC_SEQ, SWA_KV_W),
                  blk(WINDOW, SWA_KV_W), blk(WINDOW, SWA_KV_W)],
        out_specs=[rows, blk(WINDOW, SWA_KV_W), blk(WINDOW, SWA_KV_W)],
        out_shape=[jax.ShapeDtypeStruct((DEC_BATCH * DEC_SEQ, SWA_Q_W), F32),
                   jax.ShapeDtypeStruct((DEC_BATCH, WINDOW, SWA_KV_W), F32),
                   jax.ShapeDtypeStruct((DEC_BATCH, WINDOW, SWA_KV_W), F32)],
        compiler_params=_params("parallel"),
        name="swa_sample",
    )(sinks, qa, kn, vn, kbuf, vbuf)


def _hgrn_lower_bound(lbl_ref, layer):
    logits = lbl_ref[...]
    e = jnp.exp(logits - jnp.max(logits, axis=0, keepdims=True))
    sm = e / jnp.sum(e, axis=0, keepdims=True)
    return jnp.sum(sm[:layer + 1], axis=0, keepdims=True)


def _split3(x):
    hi = x.astype(BF16)
    r = x - hi.astype(F32)
    mid = r.astype(BF16)
    lo = (r - mid.astype(F32)).astype(BF16)
    return hi, mid, lo


def _hgrn_chunk_terms(x, lb, chunk):
    r = x.shape[0]
    qr, fr, ir, gr = (x[:, i * HGRN_W:(i + 1) * HGRN_W] for i in range(4))
    f = lb + (1.0 - lb) * _sigmoid(fr)
    logf = jnp.log(f)
    row = _iota((r, r), 0)
    col = _iota((r, r), 1)
    chunk_start = row & ~(chunk - 1)
    same = (col >= chunk_start) & (col < chunk_start + chunk)
    tril = (col >= chunk_start) & (col <= row)
    parts = _split3(logf)
    tri_m = jnp.where(tril, 1.0, 0.0).astype(BF16)
    blk_m = jnp.where(same, 1.0, 0.0).astype(BF16)
    b = sum(_dot(tri_m, p) for p in parts)
    bl = sum(_dot(blk_m, p) for p in parts)
    k = 1.0 - f
    qd = _silu(qr) * jnp.exp(b)
    kd = k * jnp.exp(-b)
    k2 = k * jnp.exp(bl - b)
    return qd, kd, k2, ir, gr, bl, tril


def _hgrn_intra(qd, kd, v, tril):
    att = jnp.where(tril, _dot_nt(qd.astype(BF16), kd.astype(BF16)), 0.0)
    return _dot(att.astype(BF16), v.astype(BF16))


def _hgrn_finish(o, g, on):
    return (_rms(o, on) * _silu(g)).astype(BF16)


def _hgrn_prompt_body(lbl_ref, on_ref, x_ref, o_ref, s_ref, st_ref):
    t = pl.program_id(1)

    @pl.when(t == 0)
    def _():
        st_ref[...] = jnp.zeros_like(st_ref)

    lb = _hgrn_lower_bound(lbl_ref, 0)
    qd, kd, k2, v, g, bl, tril = _hgrn_chunk_terms(x_ref[0], lb, HGRN_CHUNK)
    rows = _iota((HGRN_TILE, HGRN_HEAD_DIM), 0)
    n_chunks = HGRN_TILE // HGRN_CHUNK
    head = [slice(h * HGRN_HEAD_DIM, (h + 1) * HGRN_HEAD_DIM) for h in range(HGRN_HEADS)]
    o_intra = [_hgrn_intra(qd[:, hs], kd[:, hs], v[:, hs], tril) for hs in head]
    qdb = [qd[:, hs].astype(BF16) for hs in head]
    vt = [v[:, hs].T.astype(BF16) for hs in head]
    st = [st_ref[h] for h in range(HGRN_HEADS)]
    o_inter = [[] for _ in head]
    for c in range(n_chunks):
        r0 = c * HGRN_CHUNK
        in_chunk = (rows >= r0) & (rows < r0 + HGRN_CHUNK)
        for h, hs in enumerate(head):
            o_inter[h].append(_dot_nt(qdb[h][r0:r0 + HGRN_CHUNK], st[h].astype(BF16)))
            d_st = _dot(vt[h], jnp.where(in_chunk, k2[:, hs], 0.0).astype(BF16))
            st[h] = st[h] * jnp.exp(bl[r0:r0 + 1, hs]) + d_st
    for h, hs in enumerate(head):
        st_ref[h] = st[h]
        o = o_intra[h] + jnp.concatenate(o_inter[h], axis=0)
        o_ref[0, :, hs] = _hgrn_finish(o, g[:, hs], on_ref[...])

    @pl.when(t == pl.num_programs(1) - 1)
    def _():
        for h in range(HGRN_HEADS):
            s_ref[0, h] = st_ref[h].T


def _hgrn_prompt(lbl, on, hg):
    tt = HGRN_TILE
    return pl.pallas_call(
        _hgrn_prompt_body,
        grid=(BATCH, SEQ // tt),
        in_specs=[_resident(lbl.shape), _resident((1, HGRN_HEAD_DIM)),
                  pl.BlockSpec((1, tt, 4 * HGRN_W), lambda b, t: (b, t, 0))],
        out_specs=[pl.BlockSpec((1, tt, HGRN_W), lambda b, t: (b, t, 0)),
                   pl.BlockSpec((1, HGRN_HEADS, HGRN_HEAD_DIM, HGRN_HEAD_DIM), lambda b, t: (b, 0, 0, 0))],
        out_shape=[jax.ShapeDtypeStruct((BATCH, SEQ, HGRN_W), BF16),
                   jax.ShapeDtypeStruct((BATCH, HGRN_HEADS, HGRN_HEAD_DIM, HGRN_HEAD_DIM), F32)],
        scratch_shapes=[pltpu.VMEM((HGRN_HEADS, HGRN_HEAD_DIM, HGRN_HEAD_DIM), F32)],
        compiler_params=_params("parallel", "arbitrary"),
        name="hgrn_prompt",
    )(lbl, on, hg)


def _hgrn_sample_body(lbl_ref, on_ref, x_ref, s0_ref, o_ref, s_ref):
    bb = SAMPLE_BATCH_TILE
    r = bb * DEC_SEQ
    lb = _hgrn_lower_bound(lbl_ref, 0)
    qd, kd, k2, v, g, bl, tril = _hgrn_chunk_terms(x_ref[...], lb, DEC_SEQ)
    cols = _iota((HGRN_HEAD_DIM, r), 1)
    for h in range(HGRN_HEADS):
        hs = slice(h * HGRN_HEAD_DIM, (h + 1) * HGRN_HEAD_DIM)
        o_intra = _hgrn_intra(qd[:, hs], kd[:, hs], v[:, hs], tril)
        qdb = qd[:, hs].astype(BF16)
        vb = v[:, hs].astype(BF16)
        k2t = k2[:, hs].T
        decay_t = jnp.exp(bl[:, hs].T)
        o_inter = []
        for b in range(bb):
            r0 = b * DEC_SEQ
            s0 = s0_ref[b, h]
            o_inter.append(_dot(qdb[r0:r0 + DEC_SEQ], s0.astype(BF16)))
            in_seq = (cols >= r0) & (cols < r0 + DEC_SEQ)
            d_s = _dot(jnp.where(in_seq, k2t, 0.0).astype(BF16), vb)
            s_ref[b, h] = s0 * decay_t[:, r0:r0 + 1] + d_s
        o = o_intra + jnp.concatenate(o_inter, axis=0)
        o_ref[:, hs] = _hgrn_finish(o, g[:, hs], on_ref[...])


def _hgrn_sample(lbl, on, hg, s0):
    bb = SAMPLE_BATCH_TILE
    r = bb * DEC_SEQ
    st = pl.BlockSpec((bb, HGRN_HEADS, HGRN_HEAD_DIM, HGRN_HEAD_DIM), lambda i: (i, 0, 0, 0))
    return pl.pallas_call(
        _hgrn_sample_body,
        grid=(DEC_BATCH // bb,),
        in_specs=[_resident(lbl.shape), _resident((1, HGRN_HEAD_DIM)),
                  pl.BlockSpec((r, 4 * HGRN_W), lambda i: (i, 0)), st],
        out_specs=[pl.BlockSpec((r, HGRN_W), lambda i: (i, 0)), st],
        out_shape=[jax.ShapeDtypeStruct((DEC_BATCH * DEC_SEQ, HGRN_W), BF16),
                   jax.ShapeDtypeStruct(s0.shape, F32)],
        compiler_params=_params("parallel"),
        name="hgrn_sample",
    )(lbl, on, hg, s0)


FFN_CHUNK = D_FF // 2


def _post_body(n_mix, *refs):
    mix_refs = refs[:n_mix]
    (h_ref, p_ref, wo_ref, g_mix, g_pre, wg_ref, wu_ref, wd_ref, g_post, wpp_ref, wpg_ref, g_ple, o_ref) = refs[n_mix:]
    y = None
    r0 = 0
    for m_ref in mix_refs:
        w = m_ref.shape[-1]
        part = _dot(m_ref[...].astype(BF16), wo_ref[r0:r0 + w, :])
        y = part if y is None else y + part
        r0 += w
    h = h_ref[...] + _rms(y, g_mix[...])
    u = _rms(h, g_pre[...]).astype(BF16)
    ff = None
    for c in range(0, D_FF, FFN_CHUNK):
        gate = _dot(u, wg_ref[:, c:c + FFN_CHUNK])
        up = _dot(u, wu_ref[:, c:c + FFN_CHUNK])
        part = _dot((_silu(gate) * up).astype(BF16), wd_ref[c:c + FFN_CHUNK, :])
        ff = part if ff is None else ff + part
    h = h + _rms(ff, g_post[...])
    e = _dot(p_ref[...].astype(BF16), wpp_ref[...]) * _sigmoid(_dot(h.astype(BF16), wpg_ref[...]))
    o_ref[...] = h + _rms(e, g_ple[...])


def _post(layer, mix, h, p, wo, g_mix, g_pre, wg, wu, wd, g_post, wpp, wpg, g_ple):
    t = h.shape[0]
    tm = TOKEN_TILE
    row = lambda n: pl.BlockSpec((tm, n), lambda i: (i, 0))
    gain = _resident((1, D_MODEL))
    of_layer = lambda w: pl.BlockSpec((None,) + w.shape[1:], lambda i: (layer, 0, 0), pipeline_mode=pl.Buffered(1))
    return pl.pallas_call(
        functools.partial(_post_body, len(mix)),
        grid=(t // tm,),
        in_specs=[row(m.shape[-1]) for m in mix] + [
            row(D_MODEL), pl.BlockSpec((None, tm, PLE_DIM), lambda i: (layer, i, 0)), _resident(wo.shape), gain, gain,
            of_layer(wg), of_layer(wu), of_layer(wd), gain, of_layer(wpp), of_layer(wpg), gain],
        out_specs=row(D_MODEL),
        out_shape=jax.ShapeDtypeStruct((t, D_MODEL), F32),
        compiler_params=_params("parallel"),
        name="post",
    )(*mix, h, p, wo, g_mix, g_pre, wg, wu, wd, g_post, wpp, wpg, g_ple)


def _rope_tables(pos, invf, sgn):
    ang = pos * invf
    return jnp.cos(ang), jnp.sin(ang) * sgn


def _mla_latents(h_ref, g_pre, wdq_ref, g_q, wkv_ref, g_kv, wpe_ref, wpes_ref, cos32, sin32):
    u = _rms(h_ref[...], g_pre[...]).astype(BF16)
    cqn = _rms(_dot(u, wdq_ref[...]), g_q[...]).astype(BF16)
    ckv = _rms(_dot(u, wkv_ref[...]), g_kv[...])
    kpe = _dot(u, wpe_ref[...]) * cos32 + _dot(u, wpes_ref[...]) * sin32
    return cqn, ckv, kpe


def _mla_pre_prompt_body(h_ref, g_pre, wdq_ref, g_q, wq_ref, wkvp_ref, g_kv,
                         wuk_ref, wuvt_ref, ones_ref, invf_ref, sgn_ref,
                         q_ref, k_ref, vt_ref, ckv_ref, kpe_ref):
    tm = TOKEN_TILE
    i = pl.program_id(0)
    pos = ((i * tm + _iota((tm, 1), 0)) & (SEQ - 1)).astype(F32)
    cos128, sin128 = _rope_tables(pos, invf_ref[...], sgn_ref[...])
    rope = slice(MLA_NOPE_DIM, MLA_NOPE_DIM + MLA_ROPE_DIM)
    half = MLA_ROPE_DIM // 2
    first_half = _iota((tm, MLA_HEAD_PAD), 1) < MLA_NOPE_DIM + half

    def swap_rope_halves(x):
        return jnp.where(first_half, pltpu.roll(x, MLA_HEAD_PAD - half, 1), pltpu.roll(x, half, 1))

    u = _rms(h_ref[...], g_pre[...]).astype(BF16)
    cqn = _rms(_dot(u, wdq_ref[...]), g_q[...]).astype(BF16)
    kvp = _dot(u, wkvp_ref[...])
    ckv = _rms(kvp[:, :MLA_KV_RANK], g_kv[...])
    kpe_raw = kvp[:, MLA_KV_RANK:]
    kpe_pad = kpe_raw * cos128 + swap_rope_halves(kpe_raw) * sin128
    ckv_ref[...] = ckv
    kpe_ref[...] = kpe_pad[:, rope]
    ckv_b = ckv.astype(BF16)
    vt_ref[0, 0] = (_dot_nt(wuvt_ref[...], ckv_b) + ones_ref[...]).astype(BF16)
    for h in range(MLA_HEADS):
        hs = slice(h * MLA_HEAD_PAD, (h + 1) * MLA_HEAD_PAD)
        q = _dot(cqn, wq_ref[:, hs])
        q = q * cos128 + swap_rope_halves(q) * sin128
        q_ref[0, h] = (q * (MLA_SCALE * LOG2_E)).astype(BF16)
        k_ref[0, h] = (_dot(ckv_b, wuk_ref[:, hs]) + kpe_pad).astype(BF16)


def _mla_pre_prompt(h, g_pre, wdq, g_q, wq, wkvp, g_kv, wuk, wuvt, ones_col, invf, sgn):
    tm = TOKEN_TILE
    assert tm == MLA_TK
    per_seq = SEQ // tm
    heads = pl.BlockSpec((1, MLA_HEADS, tm, MLA_HEAD_PAD), lambda i: (i // per_seq, 0, i % per_seq, 0))
    head_shape = jax.ShapeDtypeStruct((BATCH, MLA_HEADS, SEQ, MLA_HEAD_PAD), BF16)
    vt_rows = MLA_HEADS * MLA_VT_ROWS
    row = lambda n: pl.BlockSpec((tm, n), lambda i: (i, 0))
    consts = [g_pre, wdq, g_q, wq, wkvp, g_kv, wuk, wuvt, ones_col, invf, sgn]
    return pl.pallas_call(
        _mla_pre_prompt_body,
        grid=(h.shape[0] // tm,),
        in_specs=[row(D_MODEL)] + [_resident(c.shape) for c in consts],
        out_specs=[heads, heads,
                   pl.BlockSpec((1, 1, vt_rows, tm), lambda i: (i // per_seq, i % per_seq, 0, 0)),
                   row(MLA_KV_RANK), row(MLA_ROPE_DIM)],
        out_shape=[head_shape, head_shape,
                   jax.ShapeDtypeStruct((BATCH, per_seq, vt_rows, tm), BF16),
                   jax.ShapeDtypeStruct((h.shape[0], MLA_KV_RANK), F32),
                   jax.ShapeDtypeStruct((h.shape[0], MLA_ROPE_DIM), F32)],
        compiler_params=_params("parallel"),
        name="mla_pre_prompt",
    )(h, *consts)


def _mla_prompt_attn_body(q_ref, k_ref, vt_ref, o_ref, m_ref, acc_ref):
    i = pl.program_id(2)
    m_ref[...] = jnp.full_like(m_ref, NEG)
    acc_ref[...] = jnp.zeros_like(acc_ref)
    causal = _iota((MLA_TK, MLA_TK), 0) <= _iota((MLA_TK, MLA_TK), 1)
    splits = MLA_TQ // MLA_TK

    def kv_tile(j, units):
        k0 = pl.multiple_of(j * MLA_TK, MLA_TK)
        cols = lambda blk: slice(blk * MLA_TK, (blk + 1) * MLA_TK)

        def scores(hh, blk, masked):
            s = _dot_nt(k_ref[0, hh, pl.ds(k0, MLA_TK), :], q_ref[0, hh, cols(blk), :])
            return jnp.where(causal, s, NEG) if masked else s

        def softmax(hh, blk, s):
            m_prev = m_ref[hh, :, cols(blk)]
            m_new = jnp.maximum(m_prev, jnp.max(s, axis=0, keepdims=True))
            m_ref[hh, :, cols(blk)] = m_new
            return jnp.exp2(s - m_new).astype(BF16), jnp.exp2(m_prev - m_new)

        def values(hh, blk, p, alpha):
            vt = vt_ref[0, j, hh * MLA_VT_ROWS:(hh + 1) * MLA_VT_ROWS, :]
            acc_ref[hh, :, cols(blk)] = alpha * acc_ref[hh, :, cols(blk)] + _dot(vt, p)

        n = len(units)
        s_of, p_of = {}, {}
        for step in range(n + 2):
            if step < n:
                s_of[step] = scores(*units[step])
            if 0 <= step - 1 < n:
                hh, blk, _ = units[step - 1]
                p_of[step - 1] = softmax(hh, blk, s_of.pop(step - 1))
            if 0 <= step - 2 < n:
                hh, blk, _ = units[step - 2]
                values(hh, blk, *p_of.pop(step - 2))

    heads = range(MLA_HEAD_GROUP)

    def below_diagonal(j, carry):
        kv_tile(j, [(hh, blk, False) for hh in heads for blk in range(splits)])
        return carry

    lax.fori_loop(0, i * splits, below_diagonal, 0)
    for d in range(splits):
        kv_tile(i * splits + d, [(hh, blk, blk == d) for hh in heads for blk in range(d, splits)])
    for pair in range(MLA_HEAD_GROUP // 2):
        heads = []
        for hh in (2 * pair, 2 * pair + 1):
            acc = acc_ref[hh]
            heads.append(acc[:MLA_V_DIM] / acc[MLA_V_DIM:MLA_V_DIM + 1])
        o_t = jnp.concatenate(heads, axis=0)
        o_ref[0, :, pair * 2 * MLA_V_DIM:(pair + 1) * 2 * MLA_V_DIM] = o_t.T.astype(BF16)


def _mla_prompt_attn(q, k, vt):
    hg = MLA_HEAD_GROUP
    assert MLA_TQ % MLA_TK == 0 and hg % 2 == 0
    return pl.pallas_call(
        _mla_prompt_attn_body,
        grid=(BATCH, MLA_HEADS // hg, SEQ // MLA_TQ),
        in_specs=[pl.BlockSpec((1, hg, MLA_TQ, MLA_HEAD_PAD), lambda b, g, i: (b, g, i, 0)),
                  pl.BlockSpec((1, hg, SEQ, MLA_HEAD_PAD), lambda b, g, i: (b, g, 0, 0),
                               pipeline_mode=pl.Buffered(1)),
                  pl.BlockSpec((1, SEQ // MLA_TK, hg * MLA_VT_ROWS, MLA_TK), lambda b, g, i: (b, 0, g, 0),
                               pipeline_mode=pl.Buffered(1))],
        out_specs=pl.BlockSpec((1, MLA_TQ, hg * MLA_V_DIM), lambda b, g, i: (b, i, g)),
        out_shape=jax.ShapeDtypeStruct((BATCH, SEQ, MLA_HEADS * MLA_V_DIM), BF16),
        scratch_shapes=[pltpu.VMEM((hg, 1, MLA_TQ), F32), pltpu.VMEM((hg, MLA_VT_ROWS, MLA_TQ), F32)],
        compiler_params=_params("parallel", "parallel", "arbitrary"),
        name="mla_prompt_attn",
    )(q, k, vt)


def _mla_pre_sample_body(h_ref, g_pre, wdq_ref, g_q, wn_ref, wr_ref, wrs_ref, wukt_ref, wkv_ref, g_kv,
                         wpe_ref, wpes_ref, invf_ref, sgn_ref,
                         ql_ref, qp_ref, ckv_ref, kpe_ref):
    tm = TOKEN_TILE
    pos = (PAST_LEN + (_iota((tm, 1), 0) & (DEC_SEQ - 1))).astype(F32)
    cos32, sin32 = _rope_tables(pos, invf_ref[...], sgn_ref[...])
    cqn, ckv, kpe = _mla_latents(h_ref, g_pre, wdq_ref, g_q, wkv_ref, g_kv, wpe_ref, wpes_ref, cos32, sin32)
    ckv_ref[...] = ckv
    kpe_ref[...] = kpe
    n_seq = tm // DEC_SEQ
    for h in range(MLA_HEADS):
        rows = slice(h * DEC_SEQ, (h + 1) * DEC_SEQ)
        q_nope = _dot(cqn, wn_ref[h]).astype(BF16)
        q_lat = _dot(q_nope, wukt_ref[h]) * (MLA_SCALE * LOG2_E)
        ql_ref[:, rows, :] = q_lat.reshape(n_seq, DEC_SEQ, MLA_KV_RANK)
        q_pe = (_dot(cqn, wr_ref[h]) * cos32 + _dot(cqn, wrs_ref[h]) * sin32) * (MLA_SCALE * LOG2_E)
        qp_ref[:, rows, :] = q_pe.reshape(n_seq, DEC_SEQ, MLA_ROPE_DIM)


def _mla_pre_sample(h, g_pre, wdq, g_q, wn, wr, wrs, wukt, wkv, g_kv, wpe, wpes, invf, sgn):
    tm = TOKEN_TILE
    n_seq = tm // DEC_SEQ
    rows = MLA_HEADS * DEC_SEQ
    row = lambda n: pl.BlockSpec((tm, n), lambda i: (i, 0))
    consts = [g_pre, wdq, g_q, wn, wr, wrs, wukt, wkv, g_kv, wpe, wpes, invf, sgn]
    return pl.pallas_call(
        _mla_pre_sample_body,
        grid=(h.shape[0] // tm,),
        in_specs=[row(D_MODEL)] + [_resident(c.shape) for c in consts],
        out_specs=[pl.BlockSpec((n_seq, rows, MLA_KV_RANK), lambda i: (i, 0, 0)),
                   pl.BlockSpec((n_seq, rows, MLA_ROPE_DIM), lambda i: (i, 0, 0)),
                   row(MLA_KV_RANK), row(MLA_ROPE_DIM)],
        out_shape=[jax.ShapeDtypeStruct((DEC_BATCH, rows, MLA_KV_RANK), F32),
                   jax.ShapeDtypeStruct((DEC_BATCH, rows, MLA_ROPE_DIM), F32),
                   jax.ShapeDtypeStruct((h.shape[0], MLA_KV_RANK), F32),
                   jax.ShapeDtypeStruct((h.shape[0], MLA_ROPE_DIM), F32)],
        compiler_params=_params("parallel"),
        name="mla_pre_sample",
    )(h, *consts)


def _mla_sample_attn_body(pt_ref, ql_ref, qp_ref, cn_ref, pn_ref, ckv_hbm, kpe_hbm, o_ref,
                          m_ref, l_ref, acc_ref, cnp_ref, pnp_ref, ck_buf, kp_buf, ck_sem, kp_sem):
    npg = PAGES_PER_GROUP
    groups_per_seq = N_PAGES // npg
    total_groups = (DEC_BATCH // SAMPLE_STREAMS) * groups_per_seq
    b = pl.program_id(0)
    rows = MLA_HEADS * DEC_SEQ
    streams = range(SAMPLE_STREAMS)

    def page_copies(page_of, st, slot):
        copies = []
        for j in range(npg):
            page = page_of(j)
            copies.append(pltpu.make_async_copy(
                ckv_hbm.at[0, page], ck_buf.at[st, slot, pl.ds(j * PAGE_SIZE, PAGE_SIZE), :], ck_sem.at[st, slot]))
            copies.append(pltpu.make_async_copy(kpe_hbm.at[0, page], kp_buf.at[st, slot, j], kp_sem.at[st, slot]))
        return copies

    def start_group(t, slot):
        step, g = t // groups_per_seq, t % groups_per_seq
        for st in streams:
            base = (step * SAMPLE_STREAMS + st) * N_PAGES + g * npg
            for c in page_copies(lambda j: pt_ref[base + j], st, slot):
                c.start()

    def wait_group(slot):
        for st in streams:
            for c in page_copies(lambda j: 0, st, slot):
                c.wait()

    @pl.when(b == 0)
    def _():
        for t0 in range(PAGE_RING):
            start_group(t0, t0)

    ql = [ql_ref[st].astype(BF16) for st in streams]
    qp = [qp_ref[st].astype(BF16) for st in streams]

    cnp_ref[...] = jnp.zeros_like(cnp_ref)
    pnp_ref[...] = jnp.zeros_like(pnp_ref)
    cnp_ref[:, :DEC_SEQ, :] = cn_ref[...]
    pnp_ref[:, :DEC_SEQ, :] = pn_ref[...]
    t_q = _iota((rows, PAGE_SIZE), 0) & (DEC_SEQ - 1)
    new_valid = _iota((rows, PAGE_SIZE), 1) <= t_q
    cn = [cnp_ref[st].astype(BF16) for st in streams]
    s_new = [jnp.where(new_valid, _dot_nt(ql[st], cn[st]) + _dot_nt(qp[st], pnp_ref[st].astype(BF16)), NEG)
             for st in streams]

    def update(st, s, keys, first):
        m_cur = jnp.max(s, axis=-1, keepdims=True)
        if first:
            p = jnp.exp2(s - m_cur)
            m_ref[st] = m_cur
            l_ref[st] = jnp.sum(p, axis=-1, keepdims=True)
            acc_ref[st] = _dot(p.astype(BF16), keys)
            return
        m_prev = m_ref[st]
        m_new = jnp.maximum(m_prev, m_cur)
        alpha = jnp.exp2(m_prev - m_new)
        p = jnp.exp2(s - m_new)
        l_ref[st] = alpha * l_ref[st] + jnp.sum(p, axis=-1, keepdims=True)
        acc_ref[st] = alpha * acc_ref[st] + _dot(p.astype(BF16), keys)
        m_ref[st] = m_new

    def group(g, first):
        t = b * groups_per_seq + g
        slot = t % PAGE_RING
        wait_group(slot)

        def scores(st, c):
            pages = range(c * PAGES_PER_CHUNK, (c + 1) * PAGES_PER_CHUNK)
            kl = ck_buf[st, slot, pages[0] * PAGE_SIZE:(pages[-1] + 1) * PAGE_SIZE, :].astype(BF16)
            kpt = jnp.concatenate([kp_buf[st, slot, j].astype(BF16) for j in pages], axis=1)
            s = _dot_nt(ql[st], kl) + _dot(qp[st], kpt)
            if first and c == 0:
                return jnp.concatenate([s, s_new[st]], axis=1), jnp.concatenate([kl, cn[st]], axis=0)
            return s, kl

        units = [(st, c) for c in range(npg // PAGES_PER_CHUNK) for st in streams]
        nxt = scores(*units[0])
        for k, (st, c) in enumerate(units):
            cur = nxt
            if k + 1 < len(units):
                nxt = scores(*units[k + 1])
            update(st, *cur, first and c == 0)

        start_group(jnp.minimum(t + PAGE_RING, total_groups - 1), slot)

    group(0, True)

    def later_group(g, carry):
        group(g, False)
        return carry

    lax.fori_loop(1, groups_per_seq, later_group, 0)
    o_ref[...] = (acc_ref[...] / l_ref[...]).astype(BF16)

    @pl.when(b == pl.num_programs(0) - 1)
    def _():
        for slot in range(PAGE_RING):
            wait_group(slot)


def _mla_sample_attn(page_table, ql, qp, ckv_new, kpe_new, ckv_pool, kpe_pool):
    npg = PAGES_PER_GROUP
    ns = SAMPLE_STREAMS
    rows = MLA_HEADS * DEC_SEQ
    assert N_PAGES % npg == 0 and npg % PAGES_PER_CHUNK == 0 and DEC_BATCH % ns == 0
    seq = lambda r, w: pl.BlockSpec((ns, r, w), lambda b, pt: (b, 0, 0))
    hbm = pl.BlockSpec(memory_space=pl.ANY)
    grid_spec = pltpu.PrefetchScalarGridSpec(
        num_scalar_prefetch=1,
        grid=(DEC_BATCH // ns,),
        in_specs=[seq(rows, MLA_KV_RANK), seq(rows, MLA_ROPE_DIM), seq(DEC_SEQ, MLA_KV_RANK),
                  seq(DEC_SEQ, MLA_ROPE_DIM), hbm, hbm],
        out_specs=seq(rows, MLA_KV_RANK),
        scratch_shapes=[pltpu.VMEM((ns, rows, 1), F32), pltpu.VMEM((ns, rows, 1), F32),
                        pltpu.VMEM((ns, rows, MLA_KV_RANK), F32),
                        pltpu.VMEM((ns, PAGE_SIZE, MLA_KV_RANK), F32), pltpu.VMEM((ns, PAGE_SIZE, MLA_ROPE_DIM), F32),
                        pltpu.VMEM((ns, PAGE_RING, npg * PAGE_SIZE, MLA_KV_RANK), F32),
                        pltpu.VMEM((ns, PAGE_RING, npg, MLA_ROPE_DIM, PAGE_SIZE), F32),
                        pltpu.SemaphoreType.DMA((ns, PAGE_RING)), pltpu.SemaphoreType.DMA((ns, PAGE_RING))],
    )
    return pl.pallas_call(
        _mla_sample_attn_body,
        grid_spec=grid_spec,
        out_shape=jax.ShapeDtypeStruct((DEC_BATCH, rows, MLA_KV_RANK), BF16),
        compiler_params=_params("arbitrary"),
        name="mla_sample_attn",
    )(page_table, ql, qp, ckv_new, kpe_new, ckv_pool, kpe_pool)


def _mla_sample_out_body(x_ref, w_ref, o_ref):
    x = x_ref[...].reshape(DEC_BATCH * 2 * DEC_SEQ, MLA_KV_RANK)
    y = _dot(x, w_ref[0]).reshape(DEC_BATCH, 2 * DEC_SEQ, 2 * MLA_V_DIM)
    first_head = _iota((DEC_BATCH, DEC_SEQ, 2 * MLA_V_DIM), 2) < MLA_V_DIM
    o = jnp.where(first_head, y[:, :DEC_SEQ], y[:, DEC_SEQ:])
    o_ref[...] = o.reshape(DEC_BATCH * DEC_SEQ, 2 * MLA_V_DIM)


def _mla_sample_out(o_lat, wuv_pairs):
    return pl.pallas_call(
        _mla_sample_out_body,
        grid=(MLA_HEADS // 2,),
        in_specs=[pl.BlockSpec((DEC_BATCH, 2 * DEC_SEQ, MLA_KV_RANK), lambda j: (0, j, 0)),
                  pl.BlockSpec((1, MLA_KV_RANK, 2 * MLA_V_DIM), lambda j: (j, 0, 0))],
        out_specs=pl.BlockSpec((DEC_BATCH * DEC_SEQ, 2 * MLA_V_DIM), lambda j: (0, j)),
        out_shape=jax.ShapeDtypeStruct((DEC_BATCH * DEC_SEQ, MLA_HEADS * MLA_V_DIM), F32),
        compiler_params=_params("parallel"),
        name="mla_sample_out",
    )(o_lat, wuv_pairs)


def _swap_halves(w):
    half = w.shape[-1] // 2
    return jnp.concatenate([w[..., half:], w[..., :half]], axis=-1)


def _rope_consts():
    half = MLA_ROPE_DIM // 2
    inv_freq = ROPE_THETA ** (-jnp.arange(0, half, dtype=F32) * 2.0 / MLA_ROPE_DIM)
    invf32 = jnp.concatenate([inv_freq, inv_freq])[None]
    sgn32 = jnp.concatenate([-jnp.ones((half,), F32), jnp.ones((half,), F32)])[None]
    pad = lambda a: jnp.pad(a, ((0, 0), (MLA_NOPE_DIM, MLA_HEAD_PAD - MLA_NOPE_DIM - MLA_ROPE_DIM)))
    return invf32, sgn32, pad(invf32), pad(sgn32)


def _mla_weights(w_uq, w_dkv, w_uk, w_uv):
    wq = w_uq.reshape(MLA_Q_RANK, MLA_HEADS, MLA_NOPE_DIM + MLA_ROPE_DIM)
    wq_nope, wq_rope = wq[..., :MLA_NOPE_DIM], wq[..., MLA_NOPE_DIM:]
    tail = MLA_HEAD_PAD - MLA_NOPE_DIM - MLA_ROPE_DIM
    pad_last = lambda a, lo, hi: jnp.pad(a, [(0, 0)] * (a.ndim - 1) + [(lo, hi)])
    flat = lambda a: a.reshape(a.shape[0], MLA_HEADS * MLA_HEAD_PAD).astype(BF16)
    w = {}
    w["wq"] = flat(pad_last(wq, 0, tail))
    w["wkvp"] = jnp.concatenate([w_dkv[:, :MLA_KV_RANK], pad_last(w_dkv[:, MLA_KV_RANK:], MLA_NOPE_DIM, tail)],
                                axis=1).astype(BF16)
    w["wuk"] = flat(pad_last(w_uk, 0, MLA_HEAD_PAD - MLA_NOPE_DIM))
    wuvt = jnp.transpose(w_uv, (1, 2, 0))
    w["wuvt"] = jnp.pad(wuvt, ((0, 0), (0, MLA_VT_ROWS - MLA_V_DIM), (0, 0))).reshape(
        MLA_HEADS * MLA_VT_ROWS, MLA_KV_RANK).astype(BF16)
    w["ones_col"] = jnp.tile((jnp.arange(MLA_VT_ROWS) == MLA_V_DIM).astype(F32), MLA_HEADS)[:, None]
    w["wn"] = jnp.moveaxis(wq_nope, 1, 0).astype(BF16)
    w["wr"] = jnp.moveaxis(wq_rope, 1, 0).astype(BF16)
    w["wrs"] = jnp.moveaxis(_swap_halves(wq_rope), 1, 0).astype(BF16)
    w["wukt"] = jnp.transpose(w_uk, (1, 2, 0)).astype(BF16)
    w["wkv"] = w_dkv[:, :MLA_KV_RANK].astype(BF16)
    w["wpe"] = w_dkv[:, MLA_KV_RANK:].astype(BF16)
    w["wpes"] = _swap_halves(w_dkv[:, MLA_KV_RANK:]).astype(BF16)
    w["wuv_pairs"] = jnp.transpose(w_uv.reshape(MLA_KV_RANK, MLA_HEADS // 2, 2 * MLA_V_DIM),
                                   (1, 0, 2)).astype(BF16)
    return w


def kernel(x_prompt, x_sample, cache_swa_k, cache_swa_v, state_hgrn, cache_mla_ckv, cache_mla_kpe, page_table, p_prompt, p_sample, ln_mix_pre, ln_mix_post, ln_ffn_pre, ln_ffn_post, ln_ple, w_ab_in, w_ab_out, swa_sinks, hgrn_lb_logits, hgrn_out_norm, w_mla_dq, mla_q_norm, w_mla_uq, w_mla_dkv, mla_kv_norm, w_mla_uk, w_mla_uv, w_mla_o, w_ffn_gate, w_ffn_up, w_ffn_down, w_ple_proj, w_ple_gate):
    tp = BATCH * SEQ
    ts = DEC_BATCH * DEC_SEQ
    bf = lambda a: a.astype(BF16)
    gain = lambda a, i: a[i][None].astype(F32)

    ffn_w = [bf(w) for w in (w_ffn_gate, w_ffn_up, w_ffn_down)]
    ple_w = [bf(w) for w in (w_ple_proj, w_ple_gate)]

    def finish(mix, h, p, i, wo):
        return _post(i, mix, h, p, wo, gain(ln_mix_post, i), gain(ln_ffn_pre, i), *ffn_w, gain(ln_ffn_post, i),
                     *ple_w, gain(ln_ple, i))

    hp = x_prompt.reshape(tp, D_MODEL)
    hs = x_sample.reshape(ts, D_MODEL)
    pp = p_prompt.reshape(2, tp, PLE_DIM)
    ps = p_sample.reshape(2, ts, PLE_DIM)

    w_in = bf(w_ab_in[0])
    w_out = bf(w_ab_out[0])
    g0 = gain(ln_mix_pre, 0)
    sinks = swa_sinks[0].astype(F32)
    lbl = hgrn_lb_logits.astype(F32)
    on = hgrn_out_norm[0][None].astype(F32)

    qa, ka, va, hg = _ab_in(hp, g0, w_in, BF16)
    oa = _swa_prompt(sinks, qa.reshape(BATCH, SEQ, SWA_Q_W), ka.reshape(BATCH, SEQ, SWA_KV_W),
                     va.reshape(BATCH, SEQ, SWA_KV_W))
    ob, hgrn_p = _hgrn_prompt(lbl, on, hg.reshape(BATCH, SEQ, 4 * HGRN_W))
    swa_k_p = ka.reshape(BATCH, SEQ, SWA_KV_HEADS, SWA_HEAD_DIM)[:, -WINDOW:]
    swa_v_p = va.reshape(BATCH, SEQ, SWA_KV_HEADS, SWA_HEAD_DIM)[:, -WINDOW:]
    hp = finish([oa.reshape(tp, SWA_Q_W), ob.reshape(tp, HGRN_W)], hp, pp, 0, w_out)

    qa, ka, va, hg = _ab_in(hs, g0, w_in, F32)
    oa, swa_k_s, swa_v_s = _swa_sample(
        sinks, qa, ka.reshape(DEC_BATCH, DEC_SEQ, SWA_KV_W),
        va.reshape(DEC_BATCH, DEC_SEQ, SWA_KV_W), cache_swa_k[0].reshape(DEC_BATCH, WINDOW, SWA_KV_W),
        cache_swa_v[0].reshape(DEC_BATCH, WINDOW, SWA_KV_W))
    ob, hgrn_s = _hgrn_sample(lbl, on, hg, state_hgrn[0])
    hs = finish([oa.reshape(ts, SWA_Q_W), ob], hs, ps, 0, w_out)

    mw = _mla_weights(w_mla_uq[0], w_mla_dkv[0], w_mla_uk[0], w_mla_uv[0])
    invf32, sgn32, invf128, sgn128 = _rope_consts()
    g1 = gain(ln_mix_pre, 1)
    wdq = bf(w_mla_dq[0])
    g_q = mla_q_norm[0][None].astype(F32)
    g_kv = mla_kv_norm[0][None].astype(F32)
    w_o = bf(w_mla_o[0])

    q, k, vt, ckv_p, kpe_p = _mla_pre_prompt(hp, g1, wdq, g_q, mw["wq"], mw["wkvp"], g_kv, mw["wuk"], mw["wuvt"],
                                             mw["ones_col"], invf128, sgn128)
    o = _mla_prompt_attn(q, k, vt)
    hp = finish([o.reshape(tp, MLA_HEADS * MLA_V_DIM)], hp, pp, 1, w_o)

    ql, qp, ckv_s, kpe_s = _mla_pre_sample(hs, g1, wdq, g_q, mw["wn"], mw["wr"], mw["wrs"], mw["wukt"], mw["wkv"],
                                           g_kv, mw["wpe"], mw["wpes"], invf32, sgn32)
    o_lat = _mla_sample_attn(page_table.reshape(-1), ql, qp, ckv_s.reshape(DEC_BATCH, DEC_SEQ, MLA_KV_RANK),
                             kpe_s.reshape(DEC_BATCH, DEC_SEQ, MLA_ROPE_DIM), cache_mla_ckv,
                             jnp.swapaxes(cache_mla_kpe, 2, 3))
    o = _mla_sample_out(o_lat, mw["wuv_pairs"])
    hs = finish([o], hs, ps, 1, w_o)

    kv5 = lambda a, n: a.reshape(1, n, WINDOW, SWA_KV_HEADS, SWA_HEAD_DIM)
    return (hp.reshape(BATCH, SEQ, D_MODEL), hs.reshape(DEC_BATCH, DEC_SEQ, D_MODEL),
            swa_k_p[None], swa_v_p[None], hgrn_p[None],
            ckv_p.reshape(1, BATCH, SEQ, MLA_KV_RANK), kpe_p.reshape(1, BATCH, SEQ, MLA_ROPE_DIM),
            kv5(swa_k_s, DEC_BATCH), kv5(swa_v_s, DEC_BATCH), hgrn_s[None],
            ckv_s.reshape(1, DEC_BATCH, DEC_SEQ, MLA_KV_RANK), kpe_s.reshape(1, DEC_BATCH, DEC_SEQ, MLA_ROPE_DIM))
```
